```python
import math
import jax
import jax.numpy as jnp
from jax import lax
import numpy as np

D_MODEL = 1024
BATCH = 2
SEQ = 16384
DEPTH = 4

CTX_LEN = 256
GRID_W = 64
N_MIXERS = 3
EPS = 1e-6
NEG_INF = -1e30
ROPE_BASE = 10000.0

ML_HEADS = 8
ML_V = D_MODEL // ML_HEADS
ML_QK = ML_V // 2
ML_CHUNK = 128
ML_QKW = ML_HEADS * ML_QK
ML_IN = 2 * ML_QKW + 2 * D_MODEL + 4 * ML_HEADS
ML_SPLITS = (ML_QKW, 2 * ML_QKW, 2 * ML_QKW + D_MODEL, 2 * ML_QKW + 2 * D_MODEL)

DF_HD = 64
DF_HEADS = D_MODEL // (2 * DF_HD)
DF_VD = 2 * DF_HD
DF_QKW = 2 * DF_HEADS * DF_HD
DF_IN = 2 * DF_QKW + DF_HEADS * DF_VD

SW_HD = 64
SW_HEADS = D_MODEL // SW_HD
SW_KV = 4
SW_GROUP = SW_HEADS // SW_KV
SW_WIN = 128
SW_BLOCK = 128
SW_IN = SW_HEADS * SW_HD + 2 * SW_KV * SW_HD

Q_BLOCK = 128

FF_DENSE = 11 * D_MODEL // 4
N_EXPERTS = 8
TOP_K = 2
FF_EXPERT = 7 * D_MODEL // 2

N_A = (DEPTH + 2) // 3
N_B = (DEPTH + 1) // 3
N_C = DEPTH // 3
N_DENSE = (DEPTH + 1) // 2
N_MOE = DEPTH // 2

kernel_name = "hybrid_mlstm_diffattn_swa_moe_dit"


def _rmsnorm(x, g):
    xf = x.astype(jnp.float32)
    y = xf * lax.rsqrt(jnp.mean(xf * xf, axis=-1, keepdims=True) + EPS)
    return (y * g.astype(jnp.float32)).astype(x.dtype)


def _modulate(h, shift, scale):
    return h * (1 + scale) + shift


def _rope_2d(L, hd):
    rows = L // GRID_W
    row = jnp.repeat(jnp.arange(rows, dtype=jnp.float32), GRID_W)
    col = jnp.tile(jnp.arange(GRID_W, dtype=jnp.float32), rows)
    nf = hd // 4
    inv = jnp.power(ROPE_BASE, -jnp.arange(nf, dtype=jnp.float32) / nf)
    ang = jnp.concatenate([row[:, None] * inv, col[:, None] * inv], axis=-1)
    return jnp.cos(ang), jnp.sin(ang)


def _apply_rope(x, cos, sin):
    x1, x2 = jnp.split(x.astype(jnp.float32), 2, axis=-1)
    c, s = cos[:, None, :], sin[:, None, :]
    return jnp.concatenate([x1 * c - x2 * s, x1 * s + x2 * c], axis=-1).astype(x.dtype)


def _mlstm_scan(q, k, v, i_pre, f_pre, state, with_out):
    B, n, H, dqk = q.shape
    dv = v.shape[-1]
    nc = n // ML_CHUNK
    qc = q.reshape(B, nc, ML_CHUNK, H, dqk)
    kc = k.reshape(B, nc, ML_CHUNK, H, dqk)
    vc = v.reshape(B, nc, ML_CHUNK, H, dv)
    ig = i_pre.astype(jnp.float32).reshape(B, nc, ML_CHUNK, H)
    b = jnp.cumsum(jax.nn.log_sigmoid(f_pre.astype(jnp.float32)).reshape(B, nc, ML_CHUNK, H), axis=2)
    b_end = b[:, :, -1]
    w_end = b_end[:, :, None] + ig - b
    m_loc = jnp.max(w_end, axis=2)
    e_end = jnp.exp(w_end - m_loc[:, :, None])
    C_loc = jnp.einsum('bcshv,bcshd->bchvd', e_end[..., None] * vc, kc)
    n_loc = jnp.einsum('bcsh,bcshd->bchd', e_end, kc)

    def step(carry, xs):
        C, nv, m = carry
        Cl, nl, ml, bl = xs
        m_new = jnp.maximum(bl + m, ml)
        a = jnp.exp(bl + m - m_new)
        s = jnp.exp(ml - m_new)
        C_new = a[..., None, None] * C + s[..., None, None] * Cl
        n_new = a[..., None] * nv + s[..., None] * nl
        return (C_new, n_new, m_new), (C, nv, m)

    xs = (jnp.moveaxis(C_loc, 1, 0), jnp.moveaxis(n_loc, 1, 0),
          jnp.moveaxis(m_loc, 1, 0), jnp.moveaxis(b_end, 1, 0))
    final, (C_in, n_in, m_in) = lax.scan(step, state, xs)
    if not with_out:
        return None, final
    C_in = jnp.moveaxis(C_in, 0, 1)
    n_in = jnp.moveaxis(n_in, 0, 1)
    m_in = jnp.moveaxis(m_in, 0, 1)
    t_idx = jnp.arange(ML_CHUNK)
    lower = t_idx[:, None] >= t_idx[None, :]
    d_log = b[:, :, :, None, :] - b[:, :, None, :, :] + ig[:, :, None, :, :]
    d_log = jnp.where(lower[None, None, :, :, None], d_log, -jnp.inf)
    m_inter = b + m_in[:, :, None, :]
    m_t = jnp.maximum(m_inter, jnp.max(d_log, axis=3))
    w = jnp.exp(d_log - m_t[:, :, :, None, :]) * jnp.einsum('bcthd,bcshd->bctsh', qc, kc)
    sc = jnp.exp(m_inter - m_t)
    num = jnp.einsum('bctsh,bcshv->bcthv', w, vc) + sc[..., None] * jnp.einsum('bchvd,bcthd->bcthv', C_in, qc)
    den = jnp.sum(w, axis=3) + sc * jnp.einsum('bchd,bcthd->bcth', n_in, qc)
    h = num / jnp.maximum(jnp.abs(den), jnp.exp(-m_t))[..., None]
    return h.reshape(B, n, H, dv).astype(v.dtype), final


def _mlstm_mixer(h_lat, h_ctx, w_in, gate_b, hnorm_g, w_out, need_ctx_out):
    B = h_lat.shape[0]

    def proj(h):
        n = h.shape[1]
        q, k, v, o, g = jnp.split(h @ w_in, ML_SPLITS, axis=-1)
        q = q.reshape(B, n, ML_HEADS, ML_QK)
        k = k.reshape(B, n, ML_HEADS, ML_QK) * (ML_QK ** -0.5)
        v = v.reshape(B, n, ML_HEADS, ML_V)
        g = g.reshape(B, n, 4, ML_HEADS).astype(jnp.float32) + gate_b.astype(jnp.float32)
        return q, k, v, o, g

    flip = lambda a: a[:, ::-1]

    def bidir(h, st_f, st_b, with_out):
        q, k, v, o, g = proj(h)
        hf, sf = _mlstm_scan(q, k, v, g[:, :, 0], g[:, :, 1], st_f, with_out)
        hb, sb = _mlstm_scan(flip(q), flip(k), flip(v), flip(g[:, :, 2]), flip(g[:, :, 3]), st_b, with_out)
        if not with_out:
            return None, sf, sb
        y = _rmsnorm(hf + flip(hb), hnorm_g.reshape(ML_HEADS, ML_V)).reshape(B, h.shape[1], D_MODEL)
        return (y * jax.nn.sigmoid(o)) @ w_out, sf, sb

    zero = (jnp.zeros((B, ML_HEADS, ML_V, ML_QK), jnp.float32),
            jnp.zeros((B, ML_HEADS, ML_QK), jnp.float32),
            jnp.zeros((B, ML_HEADS), jnp.float32))
    y_c, sf, sb = bidir(h_ctx, zero, zero, need_ctx_out)
    y_l, _, _ = bidir(h_lat, sf, sb, True)
    return y_l, y_c


def _diff_attend(q, k, v, lam):
    s = jnp.einsum('bqhd,bkhd->bhqk', q, k).astype(jnp.float32) * (DF_HD ** -0.5)
    p = jax.nn.softmax(s, axis=-1)
    B, H2, nq, nk = p.shape
    p = p.reshape(B, H2 // 2, 2, nq, nk)
    a = p[:, :, 0] - lam * p[:, :, 1]
    return jnp.einsum('bhqk,bkhd->bqhd', a.astype(v.dtype), v)


def _diff_mixer(h_lat, h_ctx, w_in, lam, hnorm_g, w_out, lambda_init, cos, sin, need_ctx_out):
    B, L, _ = h_lat.shape
    lam = lam.astype(jnp.float32)
    lam_full = jnp.exp(jnp.sum(lam[0] * lam[1])) - jnp.exp(jnp.sum(lam[2] * lam[3])) + lambda_init

    def proj(h):
        n = h.shape[1]
        q, k, v = jnp.split(h @ w_in, (DF_QKW, 2 * DF_QKW), axis=-1)
        return (q.reshape(B, n, 2 * DF_HEADS, DF_HD), k.reshape(B, n, 2 * DF_HEADS, DF_HD),
                v.reshape(B, n, DF_HEADS, DF_VD))

    def finish(o):
        n = o.shape[1]
        o = _rmsnorm(o, hnorm_g.reshape(DF_HEADS, DF_VD)) * (1 - lambda_init)
        return o.reshape(B, n, D_MODEL) @ w_out

    qc, kc, vc = proj(h_ctx)
    ql, kl, vl = proj(h_lat)
    ql, kl = _apply_rope(ql, cos, sin), _apply_rope(kl, cos, sin)
    k_all = jnp.concatenate([kc, kl], axis=1)
    v_all = jnp.concatenate([vc, vl], axis=1)
    nb = L // Q_BLOCK
    q_blocks = jnp.moveaxis(ql.reshape(B, nb, Q_BLOCK, 2 * DF_HEADS, DF_HD), 1, 0)
    o_l = lax.map(lambda qb: _diff_attend(qb, k_all, v_all, lam_full), q_blocks)
    y_l = finish(jnp.moveaxis(o_l, 0, 1).reshape(B, L, DF_HEADS, DF_VD))
    y_c = finish(_diff_attend(qc, kc, vc, lam_full)) if need_ctx_out else None
    return y_l, y_c


def _sink_attend(q, k, v, sinks, mask):
    s = jnp.einsum('bqkgd,bnkd->bkgqn', q, k).astype(jnp.float32) * (SW_HD ** -0.5)
    if mask is not None:
        s = jnp.where(mask, s, NEG_INF)
    sink = jnp.broadcast_to(sinks.astype(jnp.float32)[None, :, :, None, None], s.shape[:-1] + (1,))
    p = jax.nn.softmax(jnp.concatenate([sink, s], axis=-1), axis=-1)[..., 1:]
    return jnp.einsum('bkgqn,bnkd->bqkgd', p.astype(v.dtype), v)


def _swa_mixer(h_lat, h_ctx, w_in, sinks, w_out, cos, sin, need_ctx_out):
    B, L, _ = h_lat.shape
    sinks = sinks.reshape(SW_KV, SW_GROUP)

    def proj(h):
        n = h.shape[1]
        q, k, v = jnp.split(h @ w_in, (SW_HEADS * SW_HD, SW_HEADS * SW_HD + SW_KV * SW_HD), axis=-1)
        return (q.reshape(B, n, SW_HEADS, SW_HD), k.reshape(B, n, SW_KV, SW_HD), v.reshape(B, n, SW_KV, SW_HD))

    qc, kc, vc = proj(h_ctx)
    ql, kl, vl = proj(h_lat)
    ql, kl = _apply_rope(ql, cos, sin), _apply_rope(kl, cos, sin)
    nb = L // SW_BLOCK
    pad = ((0, 0), (SW_BLOCK, SW_BLOCK), (0, 0), (0, 0))
    kp = jnp.pad(kl, pad).reshape(B, nb + 2, SW_BLOCK, SW_KV, SW_HD)
    vp = jnp.pad(vl, pad).reshape(B, nb + 2, SW_BLOCK, SW_KV, SW_HD)
    k_band = jnp.concatenate([kp[:, :-2], kp[:, 1:-1], kp[:, 2:]], axis=2)
    v_band = jnp.concatenate([vp[:, :-2], vp[:, 1:-1], vp[:, 2:]], axis=2)
    blk = jnp.arange(nb)[:, None, None] * SW_BLOCK
    qpos = blk + jnp.arange(SW_BLOCK)[None, :, None]
    kpos = blk - SW_BLOCK + jnp.arange(3 * SW_BLOCK)[None, None, :]
    band_mask = (jnp.abs(qpos - kpos) <= SW_WIN) & (kpos >= 0) & (kpos < L)
    ctx_mask = jnp.ones((SW_BLOCK, kc.shape[1]), dtype=bool)
    q_blocks = jnp.moveaxis(ql.reshape(B, nb, SW_BLOCK, SW_KV, SW_GROUP, SW_HD), 1, 0)

    def one_block(xs):
        qb, kb, vb, mb = xs
        k = jnp.concatenate([kc, kb], axis=1)
        v = jnp.concatenate([vc, vb], axis=1)
        return _sink_attend(qb, k, v, sinks, jnp.concatenate([ctx_mask, mb], axis=1))

    o_l = lax.map(one_block, (q_blocks, jnp.moveaxis(k_band, 1, 0), jnp.moveaxis(v_band, 1, 0), band_mask))
    y_l = jnp.moveaxis(o_l, 0, 1).reshape(B, L, D_MODEL) @ w_out
    y_c = None
    if need_ctx_out:
        n_c = h_ctx.shape[1]
        o_c = _sink_attend(qc.reshape(B, n_c, SW_KV, SW_GROUP, SW_HD), kc, vc, sinks, None)
        y_c = o_c.reshape(B, n_c, D_MODEL) @ w_out
    return y_l, y_c


def _swiglu(h, w_in, w_out):
    a, b = jnp.split(h @ w_in, 2, axis=-1)
    return (jax.nn.silu(a) * b) @ w_out


def _moe(h, router, w_in, w_out):
    logits = (h @ router).astype(jnp.float32)
    top_v, top_i = lax.top_k(logits, TOP_K)
    p = jax.nn.softmax(top_v, axis=-1)
    gates = jnp.sum(jax.nn.one_hot(top_i, N_EXPERTS, dtype=jnp.float32) * p[..., None], axis=-2)
    y = jnp.zeros_like(h)
    for e in range(N_EXPERTS):
        y = y + gates[..., e:e + 1].astype(h.dtype) * _swiglu(h, w_in[e], w_out[e])
    return y


def setup_inputs(seed: int = 0) -> dict:
    key = jax.random.key(seed)
    ks = jax.random.split(key, 25)
    D = D_MODEL

    def nrm(k, shape, scale):
        return jax.random.normal(k, shape, jnp.float32) * scale

    forget_off = jnp.linspace(3.0, 6.0, ML_HEADS, dtype=jnp.float32)
    gate_off = jnp.stack([jnp.zeros_like(forget_off), forget_off, jnp.zeros_like(forget_off), forget_off])
    return {
        'x': nrm(ks[0], (BATCH, SEQ, D), 1.0),
        'c': nrm(ks[1], (BATCH, D), 1.0),
        'ctx': nrm(ks[2], (BATCH, CTX_LEN, D), 1.0),
        'c_ctx': nrm(ks[3], (D,), 1.0),
        'ada_w': nrm(ks[4], (DEPTH, D, 6 * D), 0.5 * D ** -0.5),
        'ada_b': nrm(ks[5], (DEPTH, 6 * D), 0.02),
        'norm1': 1.0 + nrm(ks[6], (DEPTH, D), 0.02),
        'norm2': 1.0 + nrm(ks[7], (DEPTH, D), 0.02),
        'ml_w_in': nrm(ks[8], (N_A, D, ML_IN), D ** -0.5),
        'ml_gate_b': gate_off + nrm(ks[9], (N_A, 4, ML_HEADS), 0.1),
        'ml_hnorm': 1.0 + nrm(ks[10], (N_A, D), 0.02),
        'ml_w_out': nrm(ks[11], (N_A, D, D), D ** -0.5),
        'df_w_in': nrm(ks[12], (N_B, D, DF_IN), D ** -0.5),
        'df_lam': nrm(ks[13], (N_B, 4, DF_HD), 0.1),
        'df_hnorm': 1.0 + nrm(ks[14], (N_B, D), 0.02),
        'df_w_out': nrm(ks[15], (N_B, D, D), D ** -0.5),
        'sw_w_in': nrm(ks[16], (N_C, D, SW_IN), D ** -0.5),
        'sw_sinks': nrm(ks[17], (N_C, SW_HEADS), 1.0),
        'sw_w_out': nrm(ks[18], (N_C, D, D), D ** -0.5),
        'ffn_w_in': nrm(ks[19], (N_DENSE, D, 2 * FF_DENSE), D ** -0.5),
        'ffn_w_out': nrm(ks[20], (N_DENSE, FF_DENSE, D), FF_DENSE ** -0.5),
        'moe_router': nrm(ks[21], (N_MOE, D, N_EXPERTS), D ** -0.5),
        'moe_w_in': nrm(ks[22], (N_MOE, N_EXPERTS, D, 2 * FF_EXPERT), D ** -0.5),
        'moe_w_out': nrm(ks[23], (N_MOE, N_EXPERTS, FF_EXPERT, D), FF_EXPERT ** -0.5),
        'final_norm': 1.0 + nrm(ks[24], (D,), 0.02),
    }


def reference(x, c, ctx, c_ctx, ada_w, ada_b, norm1, norm2, ml_w_in, ml_gate_b, ml_hnorm, ml_w_out,
              df_w_in, df_lam, df_hnorm, df_w_out, sw_w_in, sw_sinks, sw_w_out,
              ffn_w_in, ffn_w_out, moe_router, moe_w_in, moe_w_out, final_norm):
    L = x.shape[1]
    cos, sin = _rope_2d(L, DF_HD)
    cond_lat = jax.nn.silu(c)
    cond_ctx = jax.nn.silu(c_ctx)
    xc = ctx
    n_ctx = ctx.shape[1]
    for i in range(DEPTH):
        last = i == DEPTH - 1
        mod_l = [m[:, None] for m in jnp.split(cond_lat @ ada_w[i] + ada_b[i], 6, axis=-1)]
        mod_c = jnp.split(cond_ctx @ ada_w[i] + ada_b[i], 6, axis=-1)
        h_l = _modulate(_rmsnorm(x, norm1[i]), mod_l[0], mod_l[1])
        h_c = _modulate(_rmsnorm(xc, norm1[i]), mod_c[0], mod_c[1])
        kind, j = i % N_MIXERS, i // N_MIXERS
        if kind == 0:
            y_l, y_c = _mlstm_mixer(h_l, h_c, ml_w_in[j], ml_gate_b[j], ml_hnorm[j], ml_w_out[j], not last)
        elif kind == 1:
            lambda_init = 0.8 - 0.6 * math.exp(-0.3 * i)
            y_l, y_c = _diff_mixer(h_l, h_c, df_w_in[j], df_lam[j], df_hnorm[j], df_w_out[j],
                                   lambda_init, cos, sin, not last)
        else:
            y_l, y_c = _swa_mixer(h_l, h_c, sw_w_in[j], sw_sinks[j], sw_w_out[j], cos, sin, not last)
        x = x + mod_l[2] * y_l
        if not last:
            xc = xc + mod_c[2] * y_c
        h2 = _modulate(_rmsnorm(x, norm2[i]), mod_l[3], mod_l[4])
        if not last:
            h2_c = _modulate(_rmsnorm(xc, norm2[i]), mod_c[3], mod_c[4])
            h2 = jnp.concatenate([h2_c, h2], axis=1)
        jf = i // 2
        if i % 2 == 0:
            y2 = _swiglu(h2, ffn_w_in[jf], ffn_w_out[jf])
        else:
            y2 = _moe(h2, moe_router[jf], moe_w_in[jf], moe_w_out[jf])
        if last:
            x = x + mod_l[5] * y2
        else:
            xc = xc + mod_c[5] * y2[:, :n_ctx]
            x = x + mod_l[5] * y2[:, n_ctx:]
    return _rmsnorm(x, final_norm)
```

```python
import functools
import math

import jax
import jax.numpy as jnp
from jax import lax
from jax.experimental import pallas as pl
from jax.experimental.pallas import tpu as pltpu

F32 = jnp.float32
BF16 = jnp.bfloat16

D_MODEL = 1024
GRID_W = 64
EPS = 1e-6
NEG_INF = -1e30
ROPE_BASE = 10000.0
N_MIXERS = 3

ML_HEADS = 8
ML_V = D_MODEL // ML_HEADS
ML_QK = ML_V // 2
ML_QKW = ML_HEADS * ML_QK
ML_CHUNK = 128

DF_HD = 64
DF_HEADS = D_MODEL // (2 * DF_HD)
DF_VD = 2 * DF_HD
DF_QKW = 2 * DF_HEADS * DF_HD

SW_HD = 64
SW_HEADS = D_MODEL // SW_HD
SW_KV = 4
SW_GROUP = SW_HEADS // SW_KV
SW_BLOCK = 128

N_EXPERTS = 8
TOP_K = 2

LANES = 128
SUBLANES = 8
V7X_VMEM_BYTES = 64 * 1024 * 1024
VMEM_LIMIT = V7X_VMEM_BYTES - 8 * 1024 * 1024

TOKEN_TILE = 640
ATT_Q_TILE = 256
ATT_KV_TILE = 640
MOE_TILE = 512
MOE_FF_CHUNK = 896


def _cparams(semantics):
    return pltpu.CompilerParams(dimension_semantics=semantics, vmem_limit_bytes=VMEM_LIMIT)


def _const_spec(shape):
    nd = len(shape)
    return pl.BlockSpec(shape, lambda *_: (0,) * nd, pipeline_mode=pl.Buffered(1))


def _sigmoid(x):
    return 1.0 / (1.0 + jnp.exp(-x))


def _log_sigmoid(x):
    return jnp.minimum(x, 0.0) - jnp.log(1.0 + jnp.exp(-jnp.abs(x)))


def _norm_mod(x, a_rows, b_rows, row0, n_ctx):
    xn = x * lax.rsqrt(jnp.mean(x * x, axis=-1, keepdims=True) + EPS)
    is_ctx = (row0 + lax.broadcasted_iota(jnp.int32, (x.shape[0], 1), 0)) < n_ctx
    a = jnp.where(is_ctx, a_rows[1:2], a_rows[0:1])
    b = jnp.where(is_ctx, b_rows[1:2], b_rows[0:1])
    return xn * a + b


def _row_select(rows, row0, n_rows, n_ctx):
    is_ctx = (row0 + lax.broadcasted_iota(jnp.int32, (n_rows, 1), 0)) < n_ctx
    return jnp.where(is_ctx, rows[1:2], rows[0:1])


def _ada_kernel(c_ref, w_ref, b_ref, o_ref):
    c = c_ref[...]
    s = c * _sigmoid(c)
    o_ref[0] = jnp.dot(s, w_ref[0], preferred_element_type=F32,
                       precision=lax.Precision.HIGHEST) + b_ref[0]


def _ada_mod(cond, ada_w, ada_b):
    depth, d, six_d = ada_w.shape
    rows = cond.shape[0]
    col = D_MODEL
    return pl.pallas_call(
        _ada_kernel,
        grid=(depth, six_d // col),
        in_specs=[
            pl.BlockSpec((rows, d), lambda i, j: (0, 0)),
            pl.BlockSpec((1, d, col), lambda i, j: (i, 0, j)),
            pl.BlockSpec((1, 1, col), lambda i, j: (i, 0, j)),
        ],
        out_specs=pl.BlockSpec((1, rows, col), lambda i, j: (i, 0, j)),
        out_shape=jax.ShapeDtypeStruct((depth, rows, six_d), F32),
        compiler_params=_cparams(("arbitrary", "arbitrary")),
    )(cond, ada_w, ada_b.reshape(depth, 1, six_d))


def _rope(r, cos, sin):
    w = r.shape[1]
    lane = lax.broadcasted_iota(jnp.int32, r.shape, 1)
    swapped = jnp.where((lane & 32) == 0, pltpu.roll(r, w - 32, 1), pltpu.roll(r, 32, 1))
    reps = w // LANES
    return r * jnp.tile(cos, (1, reps)) + swapped * jnp.tile(sin, (1, reps))


def _proj_kernel(*refs, segs, n_ctx, tm, has_rope, has_grow):
    it = iter(refs)
    x_ref, a_ref, b_ref, w_ref = next(it), next(it), next(it), next(it)
    cs_ref = next(it) if has_rope else None
    wgt_ref = next(it) if has_grow else None
    outs = [next(it) for _ in segs]
    grow_ref = next(it) if has_grow else None
    h_ref = next(it)

    row0 = pl.program_id(1) * tm
    h_ref[...] = _norm_mod(x_ref[0], a_ref[0], b_ref[0], row0, n_ctx).astype(BF16)
    for (c0, c1, rope, scale), o_ref in zip(segs, outs):
        r = jnp.dot(h_ref[...], w_ref[:, c0:c1], preferred_element_type=F32)
        if rope:
            r = _rope(r, cs_ref[:, :LANES], cs_ref[:, LANES:])
        if scale != 1.0:
            r = r * scale
        o_ref[0] = r.astype(o_ref.dtype)
    if has_grow:
        grow_ref[0] = lax.dot_general(wgt_ref[...], h_ref[...], (((1,), (1,)), ((), ())),
                                      preferred_element_type=F32)


def _project(x, a_pair, b_pair, w, segs, out_dtypes, n_ctx, cs=None, wgt=None):
    bsz, n, d = x.shape
    tm = TOKEN_TILE
    assert n % tm == 0
    tok = lambda b, t: (b, t, 0)
    in_specs = [
        pl.BlockSpec((1, tm, d), tok),
        pl.BlockSpec((1, 2, d), lambda b, t: (b, 0, 0)),
        pl.BlockSpec((1, 2, d), lambda b, t: (b, 0, 0)),
        _const_spec(w.shape),
    ]
    args = [x, a_pair, b_pair, w]
    if cs is not None:
        in_specs.append(pl.BlockSpec((tm, 2 * LANES), lambda b, t: (t, 0)))
        args.append(cs)
    if wgt is not None:
        in_specs.append(_const_spec(wgt.shape))
        args.append(wgt)
    out_specs = [pl.BlockSpec((1, tm, c1 - c0), tok) for (c0, c1, _, _) in segs]
    out_shape = [jax.ShapeDtypeStruct((bsz, n, c1 - c0), dt) for (c0, c1, _, _), dt in zip(segs, out_dtypes)]
    if wgt is not None:
        out_specs.append(pl.BlockSpec((1, wgt.shape[0], tm), lambda b, t: (b, 0, t)))
        out_shape.append(jax.ShapeDtypeStruct((bsz, wgt.shape[0], n), F32))
    kern = functools.partial(_proj_kernel, segs=segs, n_ctx=n_ctx, tm=tm,
                             has_rope=cs is not None, has_grow=wgt is not None)
    return pl.pallas_call(
        kern,
        grid=(bsz, n // tm),
        in_specs=in_specs,
        out_specs=out_specs,
        out_shape=out_shape,
        scratch_shapes=[pltpu.VMEM((tm, d), BF16)],
        compiler_params=_cparams(("arbitrary", "arbitrary")),
    )(*args)


def _split3(x):
    hi = x.astype(BF16)
    r1 = x - hi.astype(F32)
    mid = r1.astype(BF16)
    lo = (r1 - mid.astype(F32)).astype(BF16)
    return hi, mid, lo


def _mlstm_kernel(q_ref, k_ref, v_ref, gc_ref, gr_ref, bc_ref, br_ref, o_ref, ct_ref, m_ref, *, t):
    nh = ML_HEADS
    d = pl.program_id(1)
    step = pl.program_id(2)

    @pl.when(step == 0)
    def _():
        ct_ref[...] = jnp.zeros_like(ct_ref)
        m_ref[...] = jnp.zeros_like(m_ref)

    fwd = d == 0
    gc = gc_ref[0][:, :4 * nh] + bc_ref[...]
    gc = jnp.where(fwd, gc[:, :2 * nh], gc[:, 2 * nh:])
    ig = gc[:, :nh]
    fg = _log_sigmoid(gc[:, nh:])
    gr = gr_ref[0] + br_ref[...]
    gr = jnp.where(fwd, gr[:2 * nh], gr[2 * nh:])
    ig_r = gr[:nh]
    fg_r = _log_sigmoid(gr[nh:])

    sgn = 1 - 2 * d
    row = lax.broadcasted_iota(jnp.int32, (t, t), 0)
    col = lax.broadcasted_iota(jnp.int32, (t, t), 1)
    mask = (col - row) * sgn <= 0
    tri = jnp.where(mask, 1.0, 0.0).astype(BF16)
    tri_t = jnp.where((row - col) * sgn <= 0, 1.0, 0.0).astype(BF16)
    bs = jnp.dot(tri, jnp.concatenate(_split3(fg), axis=1), preferred_element_type=F32)
    b = bs[:, :nh] + bs[:, nh:2 * nh] + bs[:, 2 * nh:]
    bs_r = jnp.dot(jnp.concatenate(_split3(fg_r), axis=0), tri_t, preferred_element_type=F32)
    b_r = bs_r[:nh] + bs_r[nh:2 * nh] + bs_r[2 * nh:]
    b_end = jnp.sum(fg, axis=0, keepdims=True)

    m_in = m_ref[:, :nh]
    w_end = b_end + ig - b
    m_loc = jnp.max(w_end, axis=0, keepdims=True)
    e_end = jnp.exp(w_end - m_loc)
    m_new = jnp.maximum(b_end + m_in, m_loc)
    a_dec = jnp.exp(b_end + m_in - m_new)
    s_dec = jnp.exp(m_loc - m_new)
    m_inter = b + m_in
    src_r = ig_r - b_r

    q = q_ref[0]
    k = k_ref[0] * (ML_QK ** -0.5)
    v = v_ref[0]
    ones = jnp.ones((t, ML_V), BF16)
    for h in range(nh):
        qh = q[:, h * ML_QK:(h + 1) * ML_QK]
        kh = k[:, h * ML_QK:(h + 1) * ML_QK]
        va = jnp.concatenate([v[:, h * ML_V:(h + 1) * ML_V], ones], axis=1)
        s = lax.dot_general(qh, kh, (((1,), (1,)), ((), ())), preferred_element_type=F32)
        d_log = jnp.where(mask, b[:, h:h + 1] + src_r[h:h + 1, :], -jnp.inf)
        m_t = jnp.maximum(m_inter[:, h:h + 1], jnp.max(d_log, axis=1, keepdims=True))
        w = (jnp.exp(d_log - m_t) * s).astype(BF16)
        sc = jnp.exp(m_inter[:, h:h + 1] - m_t)
        ct = ct_ref[h]
        r = (jnp.dot(w, va, preferred_element_type=F32)
             + sc * jnp.dot(qh, ct.astype(BF16), preferred_element_type=F32))
        num = r[:, :ML_V]
        den = r[:, ML_V:]
        o_ref[0, 0, :, h * ML_V:(h + 1) * ML_V] = num / jnp.maximum(jnp.abs(den), jnp.exp(-m_t))
        ev = (e_end[:, h:h + 1] * va.astype(F32)).astype(BF16)
        c_loc = lax.dot_general(kh, ev, (((0,), (0,)), ((), ())), preferred_element_type=F32)
        ct_ref[h] = a_dec[:, h:h + 1] * ct + s_dec[:, h:h + 1] * c_loc
    m_ref[:, :nh] = m_new


def _mlstm_scan(q, k, v, gcol, grow, gate_b, n_ctx):
    bsz, n, _ = q.shape
    t = ML_CHUNK
    nc = n // t
    ncc = n_ctx // t

    def chunk(d, i):
        bwd = jnp.where(i < ncc, ncc - 1 - i, nc - 1 - (i - ncc))
        return jnp.where(d == 0, i, bwd)

    bias = gate_b.reshape(1, 4 * ML_HEADS).astype(F32)
    bias_col = bias
    bias_row = jnp.broadcast_to(bias.reshape(4 * ML_HEADS, 1), (4 * ML_HEADS, t))
    tok = lambda b, d, i: (b, chunk(d, i), 0)
    return pl.pallas_call(
        functools.partial(_mlstm_kernel, t=t),
        grid=(bsz, 2, nc),
        in_specs=[
            pl.BlockSpec((1, t, ML_QKW), tok),
            pl.BlockSpec((1, t, ML_QKW), tok),
            pl.BlockSpec((1, t, D_MODEL), tok),
            pl.BlockSpec((1, t, LANES), tok),
            pl.BlockSpec((1, 4 * ML_HEADS, t), lambda b, d, i: (b, 0, chunk(d, i))),
            pl.BlockSpec((1, 4 * ML_HEADS), lambda b, d, i: (0, 0)),
            pl.BlockSpec((4 * ML_HEADS, t), lambda b, d, i: (0, 0)),
        ],
        out_specs=pl.BlockSpec((1, 1, t, D_MODEL), lambda b, d, i: (d, b, chunk(d, i), 0)),
        out_shape=jax.ShapeDtypeStruct((2, bsz, n, D_MODEL), F32),
        scratch_shapes=[pltpu.VMEM((ML_HEADS, ML_QK, 2 * ML_V), F32), pltpu.VMEM((1, LANES), F32)],
        compiler_params=_cparams(("arbitrary", "arbitrary", "arbitrary")),
    )(q, k, v, gcol, grow, bias_col, bias_row)


def _diff_kernel(lam_ref, q_ref, k_ref, v_ref, o_ref, acc_ref, m_ref, *, tq, tk, n, n_ctx, lambda_init):
    qi = pl.program_id(2)
    q = q_ref[0]
    lane = lax.broadcasted_iota(jnp.int32, q.shape, 1)
    zero = jnp.zeros_like(q)
    q1 = jnp.where(lane < DF_HD, q, zero)
    q2 = jnp.where(lane < DF_HD, zero, q)
    acc_ref[...] = jnp.zeros_like(acc_ref)
    m_ref[...] = jnp.full_like(m_ref, NEG_INF)

    def attend(start, size):
        kb = k_ref[0, pl.ds(start, size), :]
        vb = v_ref[0, pl.ds(start, size), :]
        va = jnp.concatenate([vb, jnp.ones((size, DF_VD), BF16)], axis=1)
        dn = (((1,), (1,)), ((), ()))
        s = jnp.concatenate([lax.dot_general(q1, kb, dn, preferred_element_type=F32),
                             lax.dot_general(q2, kb, dn, preferred_element_type=F32)], axis=0)
        m_old = m_ref[...]
        m_new = jnp.maximum(m_old, jnp.max(s, axis=1, keepdims=True))
        p = jnp.exp(s - m_new).astype(BF16)
        acc_ref[...] = jnp.exp(m_old - m_new) * acc_ref[...] + jnp.dot(p, va, preferred_element_type=F32)
        m_ref[...] = m_new

    is_ctx_tile = qi * tq < n_ctx

    @pl.when(is_ctx_tile)
    def _():
        for c in range(n_ctx // tq):
            attend(c * tq, tq)

    @pl.when(jnp.logical_not(is_ctx_tile))
    def _():
        def body(c, carry):
            attend(pl.multiple_of(c * tk, tk), tk)
            return carry
        lax.fori_loop(0, n // tk, body, 0)

    lam = lam_ref[...]
    lam_full = (jnp.exp(jnp.sum(lam[0:1] * lam[1:2], axis=1, keepdims=True))
                - jnp.exp(jnp.sum(lam[2:3] * lam[3:4], axis=1, keepdims=True)) + lambda_init)
    acc = acc_ref[...]
    o1 = acc[:tq, :DF_VD] / acc[:tq, DF_VD:]
    o2 = acc[tq:, :DF_VD] / acc[tq:, DF_VD:]
    o_ref[0] = o1 - lam_full * o2


def _diff_attention(q, k, v, lam, lambda_init, n_ctx):
    bsz, n, _ = q.shape
    tq, tk = ATT_Q_TILE, ATT_KV_TILE
    assert n_ctx % tq == 0 and n % tq == 0 and n % tk == 0
    kern = functools.partial(_diff_kernel, tq=tq, tk=tk, n=n, n_ctx=n_ctx, lambda_init=lambda_init)
    return pl.pallas_call(
        kern,
        grid=(bsz, DF_HEADS, n // tq),
        in_specs=[
            pl.BlockSpec(lam.shape, lambda b, h, i: (0, 0)),
            pl.BlockSpec((1, tq, 2 * DF_HD), lambda b, h, i: (b, i, h)),
            pl.BlockSpec((1, n, 2 * DF_HD), lambda b, h, i: (b, 0, h)),
            pl.BlockSpec((1, n, DF_VD), lambda b, h, i: (b, 0, h)),
        ],
        out_specs=pl.BlockSpec((1, tq, DF_VD), lambda b, h, i: (b, i, h)),
        out_shape=jax.ShapeDtypeStruct((bsz, n, D_MODEL), F32),
        scratch_shapes=[pltpu.VMEM((2 * tq, 2 * DF_VD), F32), pltpu.VMEM((2 * tq, 1), F32)],
        compiler_params=_cparams(("arbitrary", "arbitrary", "arbitrary")),
    )(lam.astype(F32), q, k, v)


def _swa_kernel(sink_ref, q_ref, kc_ref, vc_ref, kp_ref, vp_ref, kk_ref, vk_ref, kn_ref, vn_ref, o_ref,
                *, t, ncb, nb):
    j = pl.program_id(1)
    latent = j >= ncb
    row = lax.broadcasted_iota(jnp.int32, (t, t), 0)
    col = lax.broadcasted_iota(jnp.int32, (t, t), 1)
    ok_prev = jnp.logical_and(jnp.logical_and(latent, j - 1 >= ncb), row <= col)
    ok_cur = jnp.logical_and(latent, row >= 0)
    ok_next = jnp.logical_and(jnp.logical_and(latent, j + 1 < nb), col <= row)
    q = q_ref[0]
    sinks = sink_ref[...]
    dn = (((1,), (1,)), ((), ()))
    for g in range(SW_KV):
        ksl = slice(g * SW_HD, (g + 1) * SW_HD)
        kc, vc = kc_ref[0][:, ksl], vc_ref[0][:, ksl]
        kp, vp = kp_ref[0][:, ksl], vp_ref[0][:, ksl]
        kk, vk = kk_ref[0][:, ksl], vk_ref[0][:, ksl]
        kn, vn = kn_ref[0][:, ksl], vn_ref[0][:, ksl]
        for u in range(SW_GROUP):
            h = g * SW_GROUP + u
            qh = q[:, h * SW_HD:(h + 1) * SW_HD]
            s_c = lax.dot_general(qh, kc, dn, preferred_element_type=F32)
            s_p = jnp.where(ok_prev, lax.dot_general(qh, kp, dn, preferred_element_type=F32), NEG_INF)
            s_k = jnp.where(ok_cur, lax.dot_general(qh, kk, dn, preferred_element_type=F32), NEG_INF)
            s_n = jnp.where(ok_next, lax.dot_general(qh, kn, dn, preferred_element_type=F32), NEG_INF)
            sink = sinks[:, h:h + 1]
            m = jnp.maximum(jnp.max(s_c, axis=1, keepdims=True), sink)
            for s in (s_p, s_k, s_n):
                m = jnp.maximum(m, jnp.max(s, axis=1, keepdims=True))
            den = jnp.exp(sink - m)
            out = jnp.zeros((t, SW_HD), F32)
            for s, vv in ((s_c, vc), (s_p, vp), (s_k, vk), (s_n, vn)):
                p = jnp.exp(s - m)
                den = den + jnp.sum(p, axis=1, keepdims=True)
                out = out + jnp.dot(p.astype(BF16), vv, preferred_element_type=F32)
            o_ref[0, :, h * SW_HD:(h + 1) * SW_HD] = (out / den).astype(o_ref.dtype)


def _swa_attention(q, k, v, sinks, n_ctx):
    bsz, n, _ = q.shape
    t = SW_BLOCK
    nb = n // t
    ncb = n_ctx // t
    kvw = SW_KV * SW_HD
    ctx_spec = pl.BlockSpec((1, n_ctx, kvw), lambda b, j: (b, 0, 0))
    band = lambda off: pl.BlockSpec((1, t, kvw), lambda b, j: (b, jnp.clip(j + off, ncb, nb - 1), 0))
    return pl.pallas_call(
        functools.partial(_swa_kernel, t=t, ncb=ncb, nb=nb),
        grid=(bsz, nb),
        in_specs=[
            pl.BlockSpec((1, SW_HEADS), lambda b, j: (0, 0)),
            pl.BlockSpec((1, t, D_MODEL), lambda b, j: (b, j, 0)),
            ctx_spec, ctx_spec, band(-1), band(-1), band(0), band(0), band(1), band(1),
        ],
        out_specs=pl.BlockSpec((1, t, D_MODEL), lambda b, j: (b, j, 0)),
        out_shape=jax.ShapeDtypeStruct((bsz, n, D_MODEL), BF16),
        compiler_params=_cparams(("arbitrary", "arbitrary")),
    )(sinks.reshape(1, SW_HEADS).astype(F32), q, k, v, k, v, k, v, k, v)


def _head_norm(y, g, width):
    parts = []
    for h in range(y.shape[1] // width):
        yh = y[:, h * width:(h + 1) * width]
        parts.append(yh * lax.rsqrt(jnp.mean(yh * yh, axis=-1, keepdims=True) + EPS))
    return jnp.concatenate(parts, axis=1) * g


def _mixer_out_kernel(*refs, mode, n_ctx, tm, post_scale):
    it = iter(refs)
    x_ref, gate_ref, w_ref = next(it), next(it), next(it)
    if mode == "mlstm":
        hh_ref, og_ref, g_ref = next(it), next(it), next(it)
        y = _head_norm(hh_ref[0, 0] + hh_ref[1, 0], g_ref[...], ML_V)
        z = y * _sigmoid(og_ref[0].astype(F32))
    elif mode == "diff":
        a_ref, g_ref = next(it), next(it)
        z = _head_norm(a_ref[0], g_ref[...], DF_VD) * post_scale
    else:
        a_ref = next(it)
        z = a_ref[0]
    o_ref = next(it)
    y = jnp.dot(z.astype(BF16), w_ref[...], preferred_element_type=F32)
    gate = _row_select(gate_ref[0], pl.program_id(1) * tm, tm, n_ctx)
    o_ref[0] = x_ref[0] + gate * y


def _mixer_out(x, gate_pair, w_out, mode, n_ctx, extra, post_scale=1.0):
    bsz, n, d = x.shape
    tm = TOKEN_TILE
    tok = lambda b, t: (b, t, 0)
    in_specs = [pl.BlockSpec((1, tm, d), tok), pl.BlockSpec((1, 2, d), lambda b, t: (b, 0, 0)),
                _const_spec(w_out.shape)]
    args = [x, gate_pair, w_out]
    if mode == "mlstm":
        hh, og, g = extra
        in_specs += [pl.BlockSpec((2, 1, tm, d), lambda b, t: (0, b, t, 0)), pl.BlockSpec((1, tm, d), tok),
                     pl.BlockSpec((1, d), lambda b, t: (0, 0))]
        args += [hh, og, g.reshape(1, d)]
    elif mode == "diff":
        a, g = extra
        in_specs += [pl.BlockSpec((1, tm, d), tok), pl.BlockSpec((1, d), lambda b, t: (0, 0))]
        args += [a, g.reshape(1, d)]
    else:
        (a,) = extra
        in_specs += [pl.BlockSpec((1, tm, d), tok)]
        args += [a]
    kern = functools.partial(_mixer_out_kernel, mode=mode, n_ctx=n_ctx, tm=tm, post_scale=post_scale)
    return pl.pallas_call(
        kern,
        grid=(bsz, n // tm),
        in_specs=in_specs,
        out_specs=pl.BlockSpec((1, tm, d), tok),
        out_shape=jax.ShapeDtypeStruct((bsz, n, d), F32),
        compiler_params=_cparams(("arbitrary", "arbitrary")),
    )(*args)


def _dense_ffn_kernel(x_ref, a_ref, b_ref, gate_ref, wa_ref, wb_ref, wo_ref, o_ref, h_ref, g_ref,
                      *, n_ctx, tm, ff, chunk):
    row0 = pl.program_id(1) * tm
    x = x_ref[0]
    h_ref[...] = _norm_mod(x, a_ref[0], b_ref[0], row0, n_ctx).astype(BF16)
    for c0 in range(0, ff, chunk):
        a = jnp.dot(h_ref[...], wa_ref[:, c0:c0 + chunk], preferred_element_type=F32)
        b = jnp.dot(h_ref[...], wb_ref[:, c0:c0 + chunk], preferred_element_type=F32)
        g_ref[:, c0:c0 + chunk] = (a * _sigmoid(a) * b).astype(BF16)
    y = jnp.dot(g_ref[...], wo_ref[...], preferred_element_type=F32)
    o_ref[0] = x + _row_select(gate_ref[0], row0, tm, n_ctx) * y


def _dense_ffn(x, a_pair, b_pair, gate_pair, wa, wb, wo, n_ctx):
    bsz, n, d = x.shape
    tm = TOKEN_TILE
    ff = wa.shape[1]
    chunk = ff // 2 if (ff // 2) % LANES == 0 else ff
    tok = lambda b, t: (b, t, 0)
    pair = pl.BlockSpec((1, 2, d), lambda b, t: (b, 0, 0))
    kern = functools.partial(_dense_ffn_kernel, n_ctx=n_ctx, tm=tm, ff=ff, chunk=chunk)
    return pl.pallas_call(
        kern,
        grid=(bsz, n // tm),
        in_specs=[pl.BlockSpec((1, tm, d), tok), pair, pair, pair,
                  _const_spec(wa.shape), _const_spec(wb.shape), _const_spec(wo.shape)],
        out_specs=pl.BlockSpec((1, tm, d), tok),
        out_shape=jax.ShapeDtypeStruct((bsz, n, d), F32),
        scratch_shapes=[pltpu.VMEM((tm, d), BF16), pltpu.VMEM((tm, ff), BF16)],
        compiler_params=_cparams(("arbitrary", "arbitrary")),
    )(x, a_pair, b_pair, gate_pair, wa, wb, wo)


def _moe_route_kernel(x_ref, a_ref, b_ref, r_ref, h_ref, g_ref, i_ref, *, n_ctx, tm):
    row0 = pl.program_id(1) * tm
    h = _norm_mod(x_ref[0], a_ref[0], b_ref[0], row0, n_ctx)
    h_ref[0] = h.astype(BF16)
    logits = lax.dot_general(r_ref[...], h, (((1,), (1,)), ((), ())), preferred_element_type=F32,
                             precision=lax.Precision.HIGHEST)
    e_idx = lax.broadcasted_iota(jnp.int32, logits.shape, 0)
    v1 = jnp.max(logits, axis=0, keepdims=True)
    i1 = jnp.min(jnp.where(logits == v1, e_idx, N_EXPERTS), axis=0, keepdims=True)
    rest = jnp.where(e_idx == i1, -jnp.inf, logits)
    v2 = jnp.max(rest, axis=0, keepdims=True)
    i2 = jnp.min(jnp.where(rest == v2, e_idx, N_EXPERTS), axis=0, keepdims=True)
    p2 = jnp.exp(v2 - v1)
    den = 1.0 + p2
    g_ref[0] = jnp.concatenate([1.0 / den, p2 / den], axis=0)
    i_ref[0] = jnp.concatenate([i1, i2], axis=0)


def _moe_route(x, a_pair, b_pair, router_t, n_ctx):
    bsz, n, d = x.shape
    tm = TOKEN_TILE
    tok = lambda b, t: (b, t, 0)
    pair = pl.BlockSpec((1, 2, d), lambda b, t: (b, 0, 0))
    return pl.pallas_call(
        functools.partial(_moe_route_kernel, n_ctx=n_ctx, tm=tm),
        grid=(bsz, n // tm),
        in_specs=[pl.BlockSpec((1, tm, d), tok), pair, pair, _const_spec(router_t.shape)],
        out_specs=[pl.BlockSpec((1, tm, d), tok),
                   pl.BlockSpec((1, TOP_K, tm), lambda b, t: (b, 0, t)),
                   pl.BlockSpec((1, TOP_K, tm), lambda b, t: (b, 0, t))],
        out_shape=[jax.ShapeDtypeStruct((bsz, n, d), BF16),
                   jax.ShapeDtypeStruct((bsz, TOP_K, n), F32),
                   jax.ShapeDtypeStruct((bsz, TOP_K, n), jnp.int32)],
        compiler_params=_cparams(("arbitrary", "arbitrary")),
    )(x, a_pair, b_pair, router_t)


def _moe_expert_kernel(te_ref, nu_ref, x_ref, gt_ref, wa_ref, wb_ref, wo_ref, o_ref, acc_ref, *, nj):
    i = pl.program_id(0)
    j = pl.program_id(1)

    @pl.when(i < nu_ref[0])
    def _():
        x = x_ref[...]
        a = jnp.dot(x, wa_ref[0], preferred_element_type=F32)
        b = jnp.dot(x, wb_ref[0], preferred_element_type=F32)
        g = (a * _sigmoid(a) * b).astype(BF16)
        y = jnp.dot(g, wo_ref[0], preferred_element_type=F32)

        @pl.when(j == 0)
        def _():
            acc_ref[...] = y

        @pl.when(j > 0)
        def _():
            acc_ref[...] += y

        @pl.when(j == nj - 1)
        def _():
            o_ref[...] = (acc_ref[...] * gt_ref[...]).astype(o_ref.dtype)

    @pl.when(jnp.logical_and(i >= nu_ref[0], j == nj - 1))
    def _():
        o_ref[...] = jnp.zeros_like(o_ref)


def _moe_experts(xs, gates_sorted, w_in, w_out, tile_expert, n_used):
    p, d = xs.shape
    tm = MOE_TILE
    ff = w_out.shape[1]
    hc = MOE_FF_CHUNK
    assert ff % hc == 0 and hc % LANES == 0
    nj = ff // hc
    nt = p // tm

    def jeff(i, j, nu):
        return jnp.where(i < nu[0], j, nj - 1)

    def teff(i, nu):
        return jnp.minimum(i, jnp.maximum(nu[0] - 1, 0))

    grid_spec = pltpu.PrefetchScalarGridSpec(
        num_scalar_prefetch=2,
        grid=(nt, nj),
        in_specs=[
            pl.BlockSpec((tm, d), lambda i, j, te, nu: (teff(i, nu), 0)),
            pl.BlockSpec((tm, 1), lambda i, j, te, nu: (teff(i, nu), 0)),
            pl.BlockSpec((1, d, hc), lambda i, j, te, nu: (te[teff(i, nu)], 0, jeff(i, j, nu))),
            pl.BlockSpec((1, d, hc), lambda i, j, te, nu: (te[teff(i, nu)], 0, nj + jeff(i, j, nu))),
            pl.BlockSpec((1, hc, d), lambda i, j, te, nu: (te[teff(i, nu)], jeff(i, j, nu), 0)),
        ],
        out_specs=pl.BlockSpec((tm, d), lambda i, j, te, nu: (i, 0)),
        scratch_shapes=[pltpu.VMEM((tm, d), F32)],
    )
    return pl.pallas_call(
        functools.partial(_moe_expert_kernel, nj=nj),
        grid_spec=grid_spec,
        out_shape=jax.ShapeDtypeStruct((p, d), BF16),
        compiler_params=_cparams(("arbitrary", "arbitrary")),
    )(tile_expert, n_used, xs, gates_sorted, w_in, w_in, w_out)


def _moe_combine_kernel(*refs, n_ctx, tm, final):
    it = iter(refs)
    x_ref, gate_ref, y0_ref, y1_ref = next(it), next(it), next(it), next(it)
    fg_ref = next(it) if final else None
    o_ref = next(it)
    y = y0_ref[0].astype(F32) + y1_ref[0].astype(F32)
    x = x_ref[0] + _row_select(gate_ref[0], pl.program_id(1) * tm, tm, n_ctx) * y
    if final:
        x = x * lax.rsqrt(jnp.mean(x * x, axis=-1, keepdims=True) + EPS) * fg_ref[...]
    o_ref[0] = x


def _moe_combine(x, gate_pair, y0, y1, n_ctx, final_g=None):
    bsz, n, d = x.shape
    tm = TOKEN_TILE
    tok = lambda b, t: (b, t, 0)
    in_specs = [pl.BlockSpec((1, tm, d), tok), pl.BlockSpec((1, 2, d), lambda b, t: (b, 0, 0)),
                pl.BlockSpec((1, tm, d), tok), pl.BlockSpec((1, tm, d), tok)]
    args = [x, gate_pair, y0, y1]
    if final_g is not None:
        in_specs.append(pl.BlockSpec((1, d), lambda b, t: (0, 0)))
        args.append(final_g.reshape(1, d))
    return pl.pallas_call(
        functools.partial(_moe_combine_kernel, n_ctx=n_ctx, tm=tm, final=final_g is not None),
        grid=(bsz, n // tm),
        in_specs=in_specs,
        out_specs=pl.BlockSpec((1, tm, d), tok),
        out_shape=jax.ShapeDtypeStruct((bsz, n, d), F32),
        compiler_params=_cparams(("arbitrary", "arbitrary")),
    )(*args)


def _moe_layer(x, a_pair, b_pair, gate_pair, router, w_in, w_out, n_ctx, final_g):
    bsz, n, d = x.shape
    tm = MOE_TILE
    ntok = bsz * n
    h2, gates, idx = _moe_route(x, a_pair, b_pair, router.T, n_ctx)
    e_flat = jnp.moveaxis(idx, 1, 0).reshape(TOP_K * ntok)
    g_flat = jnp.moveaxis(gates, 1, 0).reshape(TOP_K * ntok)
    onehot = (e_flat[:, None] == jnp.arange(N_EXPERTS, dtype=jnp.int32)[None, :]).astype(jnp.int32)
    csum = jnp.cumsum(onehot, axis=0)
    counts = csum[-1]
    rank = jnp.sum(onehot * csum, axis=1) - 1
    padded = ((counts + tm - 1) // tm) * tm
    ends = jnp.cumsum(padded)
    starts = ends - padded
    dest = jnp.sum(onehot * starts[None, :], axis=1) + rank
    n_rows = TOP_K * ntok + N_EXPERTS * tm
    n_rows = ((n_rows + tm - 1) // tm) * tm
    tok_of = jnp.tile(jnp.arange(ntok, dtype=jnp.int32), TOP_K)
    src = jnp.zeros((n_rows,), jnp.int32).at[dest].set(tok_of)
    g_sorted = jnp.zeros((n_rows,), F32).at[dest].set(g_flat)
    tile_start = jnp.arange(n_rows // tm, dtype=jnp.int32) * tm
    tile_expert = jnp.minimum(jnp.sum((tile_start[:, None] >= ends[None, :]).astype(jnp.int32), axis=1),
                              N_EXPERTS - 1).astype(jnp.int32)
    n_used = (ends[-1] // tm).astype(jnp.int32).reshape(1)
    xs = jnp.take(h2.reshape(ntok, d), src, axis=0)
    ys = _moe_experts(xs, g_sorted.reshape(n_rows, 1), w_in, w_out, tile_expert, n_used)
    y0 = jnp.take(ys, dest[:ntok], axis=0).reshape(bsz, n, d)
    y1 = jnp.take(ys, dest[ntok:], axis=0).reshape(bsz, n, d)
    return _moe_combine(x, gate_pair, y0, y1, n_ctx, final_g)


def _rope_table(n_ctx, seq):
    pos = jnp.arange(seq)
    row = (pos // GRID_W).astype(F32)
    col = (pos % GRID_W).astype(F32)
    nf = DF_HD // 4
    inv = jnp.power(ROPE_BASE, -jnp.arange(nf, dtype=F32) / nf)
    ang = jnp.concatenate([row[:, None] * inv, col[:, None] * inv], axis=-1)
    cos, sin = jnp.cos(ang), jnp.sin(ang)
    cos = jnp.concatenate([jnp.ones((n_ctx, 2 * nf), F32), cos], axis=0)
    sin = jnp.concatenate([jnp.zeros((n_ctx, 2 * nf), F32), sin], axis=0)
    c128 = jnp.tile(jnp.concatenate([cos, cos], axis=1), (1, 2))
    s128 = jnp.tile(jnp.concatenate([-sin, sin], axis=1), (1, 2))
    return jnp.concatenate([c128, s128], axis=1)


def _pair(v, bsz):
    return jnp.stack([v[:bsz], jnp.broadcast_to(v[bsz], (bsz, v.shape[1]))], axis=1)


def kernel(x, c, ctx, c_ctx, ada_w, ada_b, norm1, norm2, ml_w_in, ml_gate_b, ml_hnorm, ml_w_out,
           df_w_in, df_lam, df_hnorm, df_w_out, sw_w_in, sw_sinks, sw_w_out,
           ffn_w_in, ffn_w_out, moe_router, moe_w_in, moe_w_out, final_norm):
    bsz, seq, d = x.shape
    n_ctx = ctx.shape[1]
    depth = ada_w.shape[0]
    assert d == D_MODEL and n_ctx % ATT_Q_TILE == 0 and (n_ctx + seq) % TOKEN_TILE == 0

    xs = jnp.concatenate([ctx, x], axis=1)
    cond_rows = SUBLANES * ((bsz + 1 + SUBLANES - 1) // SUBLANES)
    cond = jnp.zeros((cond_rows, d), F32).at[:bsz].set(c).at[bsz].set(c_ctx)
    mods = _ada_mod(cond, ada_w, ada_b)
    cs = _rope_table(n_ctx, seq)

    for i in range(depth):
        m = [mods[i, :, k * d:(k + 1) * d] for k in range(6)]
        a1 = _pair(norm1[i][None, :] * (1.0 + m[1]), bsz)
        b1 = _pair(m[0], bsz)
        g1 = _pair(m[2], bsz)
        a2 = _pair(norm2[i][None, :] * (1.0 + m[4]), bsz)
        b2 = _pair(m[3], bsz)
        g2 = _pair(m[5], bsz)
        kind, j = i % N_MIXERS, i // N_MIXERS
        if kind == 0:
            w = ml_w_in[j]
            nmain = 2 * ML_QKW + 2 * D_MODEL
            w_main = jnp.concatenate([w, jnp.zeros((d, LANES - 4 * ML_HEADS), F32)], axis=1).astype(BF16)
            wgt = w[:, nmain:].T.astype(BF16)
            segs = ((0, ML_QKW, False, 1.0), (ML_QKW, 2 * ML_QKW, False, 1.0),
                    (2 * ML_QKW, 2 * ML_QKW + d, False, 1.0), (2 * ML_QKW + d, nmain, False, 1.0),
                    (nmain, nmain + LANES, False, 1.0))
            q, k, v, og, gcol, grow = _project(xs, a1, b1, w_main, segs, (BF16, BF16, BF16, BF16, F32),
                                               n_ctx, wgt=wgt)
            hh = _mlstm_scan(q, k, v, gcol, grow, ml_gate_b[j], n_ctx)
            xs = _mixer_out(xs, g1, ml_w_out[j].astype(BF16), "mlstm", n_ctx, (hh, og, ml_hnorm[j]))
        elif kind == 1:
            lambda_init = 0.8 - 0.6 * math.exp(-0.3 * i)
            segs = ((0, DF_QKW, True, DF_HD ** -0.5), (DF_QKW, 2 * DF_QKW, True, 1.0),
                    (2 * DF_QKW, 2 * DF_QKW + d, False, 1.0))
            q, k, v = _project(xs, a1, b1, df_w_in[j].astype(BF16), segs, (BF16, BF16, BF16), n_ctx, cs=cs)
            att = _diff_attention(q, k, v, df_lam[j], lambda_init, n_ctx)
            xs = _mixer_out(xs, g1, df_w_out[j].astype(BF16), "diff", n_ctx, (att, df_hnorm[j]),
                            post_scale=1.0 - lambda_init)
        else:
            nq = SW_HEADS * SW_HD
            nkv = SW_KV * SW_HD
            segs = ((0, nq, True, SW_HD ** -0.5), (nq, nq + nkv, True, 1.0), (nq + nkv, nq + 2 * nkv, False, 1.0))
            q, k, v = _project(xs, a1, b1, sw_w_in[j].astype(BF16), segs, (BF16, BF16, BF16), n_ctx, cs=cs)
            att = _swa_attention(q, k, v, sw_sinks[j], n_ctx)
            xs = _mixer_out(xs, g1, sw_w_out[j].astype(BF16), "plain", n_ctx, (att,))
        jf = i // 2
        last = i == depth - 1
        if i % 2 == 0:
            ff = ffn_w_out.shape[1]
            wi = ffn_w_in[jf].astype(BF16)
            xs = _dense_ffn(xs, a2, b2, g2, wi[:, :ff], wi[:, ff:], ffn_w_out[jf].astype(BF16), n_ctx)
            if last:
                raise NotImplementedError("final norm is fused into the MoE combine")
        else:
            xs = _moe_layer(xs, a2, b2, g2, moe_router[jf], moe_w_in[jf].astype(BF16),
                            moe_w_out[jf].astype(BF16), n_ctx, final_norm if last else None)
    return xs[:, n_ctx:]
```

```python
import functools
import math

import jax
import jax.numpy as jnp
from jax import lax
from jax.experimental import pallas as pl
from jax.experimental.pallas import tpu as pltpu

F32 = jnp.float32
BF16 = jnp.bfloat16

D_MODEL = 1024
GRID_W = 64
EPS = 1e-6
NEG_INF = -1e30
ROPE_BASE = 10000.0
LOG2E = math.log2(math.e)
N_MIXERS = 3

ML_HEADS = 8
ML_V = D_MODEL // ML_HEADS
ML_QK = ML_V // 2
ML_QKW = ML_HEADS * ML_QK
ML_CHUNK = 128

DF_HD = 64
DF_HEADS = D_MODEL // (2 * DF_HD)
DF_VD = 2 * DF_HD
DF_QKW = 2 * DF_HEADS * DF_HD

SW_HD = 64
SW_HEADS = D_MODEL // SW_HD
SW_KV = 4
SW_GROUP = SW_HEADS // SW_KV
SW_WIN = 128

N_EXPERTS = 8
TOP_K = 2

LANES = 128
SUBLANES = 8
V7X_VMEM_BYTES = 64 * 1024 * 1024
VMEM_LIMIT = V7X_VMEM_BYTES - 8 * 1024 * 1024

TOKEN_TILE = 640
ATT_Q_TILE = 256
ATT_KV_TILE = 1280
SWA_Q_TILE = 256
MOE_TILE = 512
MOE_FF_CHUNK = 896


def _cparams(semantics):
    return pltpu.CompilerParams(dimension_semantics=semantics, vmem_limit_bytes=VMEM_LIMIT)


def _const_spec(shape):
    nd = len(shape)
    return pl.BlockSpec(shape, lambda *_: (0,) * nd, pipeline_mode=pl.Buffered(1))


def _sigmoid(x):
    return 1.0 / (1.0 + jnp.exp(-x))


def _log_sigmoid(x):
    return jnp.minimum(x, 0.0) - jnp.log(1.0 + jnp.exp(-jnp.abs(x)))


def _norm_mod(x, a_rows, b_rows, row0, n_ctx):
    xn = x * lax.rsqrt(jnp.mean(x * x, axis=-1, keepdims=True) + EPS)
    is_ctx = (row0 + lax.broadcasted_iota(jnp.int32, (x.shape[0], 1), 0)) < n_ctx
    a = jnp.where(is_ctx, a_rows[1:2], a_rows[0:1])
    b = jnp.where(is_ctx, b_rows[1:2], b_rows[0:1])
    return xn * a + b


def _row_select(rows, row0, n_rows, n_ctx):
    is_ctx = (row0 + lax.broadcasted_iota(jnp.int32, (n_rows, 1), 0)) < n_ctx
    return jnp.where(is_ctx, rows[1:2], rows[0:1])


def _ada_kernel(c_ref, w_ref, b_ref, o_ref):
    c = c_ref[...]
    s = c * _sigmoid(c)
    o_ref[0] = jnp.dot(s, w_ref[0], preferred_element_type=F32,
                       precision=lax.Precision.HIGHEST) + b_ref[0]


def _ada_mod(cond, ada_w, ada_b):
    depth, d, six_d = ada_w.shape
    rows = cond.shape[0]
    col = D_MODEL
    return pl.pallas_call(
        _ada_kernel,
        grid=(depth, six_d // col),
        in_specs=[
            pl.BlockSpec((rows, d), lambda i, j: (0, 0)),
            pl.BlockSpec((1, d, col), lambda i, j: (i, 0, j)),
            pl.BlockSpec((1, 1, col), lambda i, j: (i, 0, j)),
        ],
        out_specs=pl.BlockSpec((1, rows, col), lambda i, j: (i, 0, j)),
        out_shape=jax.ShapeDtypeStruct((depth, rows, six_d), F32),
        compiler_params=_cparams(("arbitrary", "arbitrary")),
    )(cond, ada_w, ada_b.reshape(depth, 1, six_d))


def _rope(r, cos, sin):
    w = r.shape[1]
    lane = lax.broadcasted_iota(jnp.int32, r.shape, 1)
    swapped = jnp.where((lane & 32) == 0, pltpu.roll(r, w - 32, 1), pltpu.roll(r, 32, 1))
    reps = w // LANES
    return r * jnp.tile(cos, (1, reps)) + swapped * jnp.tile(sin, (1, reps))


def _proj_kernel(*refs, segs, n_ctx, tm, has_rope, has_grow):
    it = iter(refs)
    x_ref, a_ref, b_ref, w_ref = next(it), next(it), next(it), next(it)
    cs_ref = next(it) if has_rope else None
    wgt_ref = next(it) if has_grow else None
    outs = [next(it) for _ in segs]
    grow_ref = next(it) if has_grow else None
    h_ref = next(it)

    row0 = pl.program_id(1) * tm
    h_ref[...] = _norm_mod(x_ref[0], a_ref[0], b_ref[0], row0, n_ctx).astype(BF16)
    for (c0, c1, rope, scale), o_ref in zip(segs, outs):
        r = jnp.dot(h_ref[...], w_ref[:, c0:c1], preferred_element_type=F32)
        if rope:
            r = _rope(r, cs_ref[:, :LANES], cs_ref[:, LANES:])
        if scale != 1.0:
            r = r * scale
        o_ref[0] = r.astype(o_ref.dtype)
    if has_grow:
        grow_ref[0] = lax.dot_general(wgt_ref[...], h_ref[...], (((1,), (1,)), ((), ())),
                                      preferred_element_type=F32)


def _project(x, a_pair, b_pair, w, segs, out_dtypes, n_ctx, cs=None, wgt=None):
    bsz, n, d = x.shape
    tm = TOKEN_TILE
    assert n % tm == 0
    tok = lambda b, t: (b, t, 0)
    in_specs = [
        pl.BlockSpec((1, tm, d), tok),
        pl.BlockSpec((1, 2, d), lambda b, t: (b, 0, 0)),
        pl.BlockSpec((1, 2, d), lambda b, t: (b, 0, 0)),
        _const_spec(w.shape),
    ]
    args = [x, a_pair, b_pair, w]
    if cs is not None:
        in_specs.append(pl.BlockSpec((tm, 2 * LANES), lambda b, t: (t, 0)))
        args.append(cs)
    if wgt is not None:
        in_specs.append(_const_spec(wgt.shape))
        args.append(wgt)
    out_specs = [pl.BlockSpec((1, tm, c1 - c0), tok) for (c0, c1, _, _) in segs]
    out_shape = [jax.ShapeDtypeStruct((bsz, n, c1 - c0), dt) for (c0, c1, _, _), dt in zip(segs, out_dtypes)]
    if wgt is not None:
        out_specs.append(pl.BlockSpec((1, wgt.shape[0], tm), lambda b, t: (b, 0, t)))
        out_shape.append(jax.ShapeDtypeStruct((bsz, wgt.shape[0], n), F32))
    kern = functools.partial(_proj_kernel, segs=segs, n_ctx=n_ctx, tm=tm,
                             has_rope=cs is not None, has_grow=wgt is not None)
    return pl.pallas_call(
        kern,
        grid=(bsz, n // tm),
        in_specs=in_specs,
        out_specs=out_specs,
        out_shape=out_shape,
        scratch_shapes=[pltpu.VMEM((tm, d), BF16)],
        compiler_params=_cparams(("arbitrary", "arbitrary")),
    )(*args)


def _split3(x):
    hi = x.astype(BF16)
    r1 = x - hi.astype(F32)
    mid = r1.astype(BF16)
    lo = (r1 - mid.astype(F32)).astype(BF16)
    return hi, mid, lo


def _mlstm_kernel(q_ref, k_ref, v_ref, gc_ref, gr_ref, bc_ref, br_ref, o_ref, ct_ref, m_ref, *, t):
    nh = ML_HEADS
    d = pl.program_id(1)
    step = pl.program_id(2)

    @pl.when(step == 0)
    def _():
        ct_ref[...] = jnp.zeros_like(ct_ref)
        m_ref[...] = jnp.zeros_like(m_ref)

    fwd = d == 0
    gc = gc_ref[0][:, :4 * nh] + bc_ref[...]
    gc = jnp.where(fwd, gc[:, :2 * nh], gc[:, 2 * nh:])
    ig = gc[:, :nh]
    fg = _log_sigmoid(gc[:, nh:])
    gr = gr_ref[0] + br_ref[...]
    gr = jnp.where(fwd, gr[:2 * nh], gr[2 * nh:])
    ig_r = gr[:nh]
    fg_r = _log_sigmoid(gr[nh:])

    sgn = 1 - 2 * d
    row = lax.broadcasted_iota(jnp.int32, (t, t), 0)
    col = lax.broadcasted_iota(jnp.int32, (t, t), 1)
    mask = (col - row) * sgn <= 0
    tri = jnp.where(mask, 1.0, 0.0).astype(BF16)
    tri_t = jnp.where((row - col) * sgn <= 0, 1.0, 0.0).astype(BF16)
    bs = jnp.dot(tri, jnp.concatenate(_split3(fg), axis=1), preferred_element_type=F32)
    b = bs[:, :nh] + bs[:, nh:2 * nh] + bs[:, 2 * nh:]
    bs_r = jnp.dot(jnp.concatenate(_split3(fg_r), axis=0), tri_t, preferred_element_type=F32)
    b_r = bs_r[:nh] + bs_r[nh:2 * nh] + bs_r[2 * nh:]
    b_end = jnp.sum(fg, axis=0, keepdims=True)

    m_in = m_ref[:, :nh]
    w_end = b_end + ig - b
    m_loc = jnp.max(w_end, axis=0, keepdims=True)
    e_end = jnp.exp(w_end - m_loc)
    m_new = jnp.maximum(b_end + m_in, m_loc)
    a_dec = jnp.exp(b_end + m_in - m_new)
    s_dec = jnp.exp(m_loc - m_new)
    m_inter = b + m_in
    src_r = ig_r - b_r

    q = q_ref[0]
    k = k_ref[0] * (ML_QK ** -0.5)
    v = v_ref[0]
    ones = jnp.ones((t, ML_V), BF16)
    for h in range(nh):
        qh = q[:, h * ML_QK:(h + 1) * ML_QK]
        kh = k[:, h * ML_QK:(h + 1) * ML_QK]
        va = jnp.concatenate([v[:, h * ML_V:(h + 1) * ML_V], ones], axis=1)
        s = lax.dot_general(qh, kh, (((1,), (1,)), ((), ())), preferred_element_type=F32)
        d_log = jnp.where(mask, b[:, h:h + 1] + src_r[h:h + 1, :], -jnp.inf)
        m_t = jnp.maximum(m_inter[:, h:h + 1], jnp.max(d_log, axis=1, keepdims=True))
        w = (jnp.exp(d_log - m_t) * s).astype(BF16)
        sc = jnp.exp(m_inter[:, h:h + 1] - m_t)
        ct = ct_ref[h]
        r = (jnp.dot(w, va, preferred_element_type=F32)
             + sc * jnp.dot(qh, ct.astype(BF16), preferred_element_type=F32))
        num = r[:, :ML_V]
        den = r[:, ML_V:]
        o_ref[0, 0, :, h * ML_V:(h + 1) * ML_V] = num / jnp.maximum(jnp.abs(den), jnp.exp(-m_t))
        ev = (e_end[:, h:h + 1] * va.astype(F32)).astype(BF16)
        c_loc = lax.dot_general(kh, ev, (((0,), (0,)), ((), ())), preferred_element_type=F32)
        ct_ref[h] = a_dec[:, h:h + 1] * ct + s_dec[:, h:h + 1] * c_loc
    m_ref[:, :nh] = m_new


def _mlstm_scan(q, k, v, gcol, grow, gate_b, n_ctx):
    bsz, n, _ = q.shape
    t = ML_CHUNK
    nc = n // t
    ncc = n_ctx // t

    def chunk(d, i):
        bwd = jnp.where(i < ncc, ncc - 1 - i, nc - 1 - (i - ncc))
        return jnp.where(d == 0, i, bwd)

    bias = gate_b.reshape(1, 4 * ML_HEADS).astype(F32)
    bias_col = bias
    bias_row = jnp.broadcast_to(bias.reshape(4 * ML_HEADS, 1), (4 * ML_HEADS, t))
    tok = lambda b, d, i: (b, chunk(d, i), 0)
    return pl.pallas_call(
        functools.partial(_mlstm_kernel, t=t),
        grid=(bsz, 2, nc),
        in_specs=[
            pl.BlockSpec((1, t, ML_QKW), tok),
            pl.BlockSpec((1, t, ML_QKW), tok),
            pl.BlockSpec((1, t, D_MODEL), tok),
            pl.BlockSpec((1, t, LANES), tok),
            pl.BlockSpec((1, 4 * ML_HEADS, t), lambda b, d, i: (b, 0, chunk(d, i))),
            pl.BlockSpec((1, 4 * ML_HEADS), lambda b, d, i: (0, 0)),
            pl.BlockSpec((4 * ML_HEADS, t), lambda b, d, i: (0, 0)),
        ],
        out_specs=pl.BlockSpec((1, 1, t, D_MODEL), lambda b, d, i: (d, b, chunk(d, i), 0)),
        out_shape=jax.ShapeDtypeStruct((2, bsz, n, D_MODEL), F32),
        scratch_shapes=[pltpu.VMEM((ML_HEADS, ML_QK, 2 * ML_V), F32), pltpu.VMEM((1, LANES), F32)],
        compiler_params=_cparams(("arbitrary", "arbitrary", "arbitrary")),
    )(q, k, v, gcol, grow, bias_col, bias_row)


def _diff_kernel(lam_ref, q_ref, kt_ref, va_ref, o_ref, acc_ref, m_ref, s_ref, *, tq, tk, n, n_ctx,
                 lambda_init):
    qi = pl.program_id(2)
    q = q_ref[0]
    lane = lax.broadcasted_iota(jnp.int32, q.shape, 1)
    zero = jnp.zeros_like(q)
    qs = (jnp.where(lane < DF_HD, q, zero), jnp.where(lane < DF_HD, zero, q))
    acc_ref[...] = jnp.zeros_like(acc_ref)
    m_ref[...] = jnp.full_like(m_ref, NEG_INF)

    def scores(start, size, slot):
        kt = kt_ref[0, :, pl.ds(start, size)]
        for j in range(2):
            s_ref[slot, j, :, :size] = jnp.dot(qs[j], kt, preferred_element_type=F32)

    def accumulate(start, size, slot):
        va = va_ref[0, pl.ds(start, size), :]
        ss = [s_ref[slot, j, :, :size] for j in range(2)]
        m_old = [m_ref[j] for j in range(2)]
        m_new = [jnp.maximum(m_old[j], jnp.max(ss[j], axis=1, keepdims=True)) for j in range(2)]
        ps = [jnp.exp2((ss[j] - m_new[j]).astype(BF16)) for j in range(2)]
        pv = [jnp.dot(ps[j], va, preferred_element_type=F32) for j in range(2)]
        for j in range(2):
            acc_ref[j] = jnp.exp2(m_old[j] - m_new[j]) * acc_ref[j] + pv[j]
            m_ref[j] = m_new[j]

    is_ctx_tile = qi * tq < n_ctx

    @pl.when(is_ctx_tile)
    def _():
        scores(0, n_ctx, 0)
        accumulate(0, n_ctx, 0)

    @pl.when(jnp.logical_not(is_ctx_tile))
    def _():
        n_chunks = n // tk
        scores(0, tk, 0)

        def body(i, carry):
            c0 = pl.multiple_of(2 * i * tk, tk)
            scores(c0 + tk, tk, 1)
            accumulate(c0, tk, 0)
            scores(c0 + 2 * tk, tk, 0)
            accumulate(c0 + tk, tk, 1)
            return carry
        lax.fori_loop(0, (n_chunks - 1) // 2, body, 0)
        accumulate((n_chunks - 1) * tk, tk, 0)

    lam = lam_ref[...]
    lam_full = (jnp.exp(jnp.sum(lam[0:1] * lam[1:2], axis=1, keepdims=True))
                - jnp.exp(jnp.sum(lam[2:3] * lam[3:4], axis=1, keepdims=True)) + lambda_init)
    o1 = acc_ref[0, :, :DF_VD] / acc_ref[0, :, DF_VD:]
    o2 = acc_ref[1, :, :DF_VD] / acc_ref[1, :, DF_VD:]
    o_ref[0] = o1 - lam_full * o2


def _diff_attention(q, k, v, lam, lambda_init, n_ctx):
    bsz, n, _ = q.shape
    tq, tk = ATT_Q_TILE, ATT_KV_TILE
    assert n_ctx % tq == 0 and n % tq == 0 and n % tk == 0 and (n // tk) % 2 == 1 and n_ctx <= tk
    kt = jnp.swapaxes(k, 1, 2)
    va = jnp.concatenate([v.reshape(bsz, n, DF_HEADS, DF_VD),
                          jnp.ones((bsz, n, DF_HEADS, DF_VD), v.dtype)], axis=-1).reshape(bsz, n, 2 * D_MODEL)
    kern = functools.partial(_diff_kernel, tq=tq, tk=tk, n=n, n_ctx=n_ctx, lambda_init=lambda_init)
    return pl.pallas_call(
        kern,
        grid=(bsz, DF_HEADS, n // tq),
        in_specs=[
            pl.BlockSpec(lam.shape, lambda b, h, i: (0, 0)),
            pl.BlockSpec((1, tq, 2 * DF_HD), lambda b, h, i: (b, i, h)),
            pl.BlockSpec((1, 2 * DF_HD, n), lambda b, h, i: (b, h, 0)),
            pl.BlockSpec((1, n, 2 * DF_VD), lambda b, h, i: (b, 0, h)),
        ],
        out_specs=pl.BlockSpec((1, tq, DF_VD), lambda b, h, i: (b, i, h)),
        out_shape=jax.ShapeDtypeStruct((bsz, n, D_MODEL), F32),
        scratch_shapes=[pltpu.VMEM((2, tq, 2 * DF_VD), F32), pltpu.VMEM((2, tq, 1), F32),
                        pltpu.VMEM((2, 2, tq, tk), F32)],
        compiler_params=_cparams(("arbitrary", "arbitrary", "arbitrary")),
    )(lam.astype(F32), q, kt, va)


def _swa_kernel(sink_ref, q_ref, kt_ref, va_ref, o_ref, *, tq, n, n_ctx):
    g = pl.program_id(1)
    i = pl.program_id(2)
    bw = tq + 2 * SW_WIN
    q = q_ref[0]
    qh = jnp.concatenate([q[:, u * SW_HD:(u + 1) * SW_HD] for u in range(SW_GROUP)], axis=0)
    start = pl.multiple_of(jnp.clip(i * tq - SW_WIN, 0, n - bw), SW_WIN)
    s_c = jnp.dot(qh, kt_ref[0, :, 0:n_ctx], preferred_element_type=F32)
    s_b = jnp.dot(qh, kt_ref[0, :, pl.ds(start, bw)], preferred_element_type=F32)
    row = lax.broadcasted_iota(jnp.int32, s_b.shape, 0)
    q_abs = i * tq + (row & (tq - 1))
    k_abs = start + lax.broadcasted_iota(jnp.int32, s_b.shape, 1)
    ok = jnp.logical_and(jnp.logical_and(i * tq >= n_ctx, k_abs >= n_ctx), jnp.abs(q_abs - k_abs) <= SW_WIN)
    s_b = jnp.where(ok, s_b, NEG_INF)
    head = lax.broadcasted_iota(jnp.int32, (SW_GROUP * tq, 1), 0) // tq
    sink = jnp.zeros((SW_GROUP * tq, 1), F32)
    for u in range(SW_GROUP):
        sink = jnp.where(head == u, sink_ref[g * SW_GROUP + u], sink)
    m = jnp.maximum(sink, jnp.maximum(jnp.max(s_c, axis=1, keepdims=True), jnp.max(s_b, axis=1, keepdims=True)))
    acc = (jnp.dot(jnp.exp(s_c - m).astype(BF16), va_ref[0, 0:n_ctx, :], preferred_element_type=F32)
           + jnp.dot(jnp.exp(s_b - m).astype(BF16), va_ref[0, pl.ds(start, bw), :], preferred_element_type=F32))
    out = acc[:, :SW_HD] / (acc[:, SW_HD:] + jnp.exp(sink - m))
    o_ref[0] = jnp.concatenate([out[u * tq:(u + 1) * tq] for u in range(SW_GROUP)], axis=1).astype(o_ref.dtype)


def _swa_attention(q, k, v, sinks, n_ctx):
    bsz, n, _ = q.shape
    tq = SWA_Q_TILE
    assert tq & (tq - 1) == 0 and n_ctx % tq == 0 and n % tq == 0 and tq % SW_WIN == 0
    gw = SW_GROUP * SW_HD
    kt = jnp.swapaxes(k, 1, 2)
    va = jnp.concatenate([v.reshape(bsz, n, SW_KV, SW_HD),
                          jnp.ones((bsz, n, SW_KV, SW_HD), v.dtype)], axis=-1).reshape(bsz, n, SW_KV * 2 * SW_HD)
    return pl.pallas_call(
        functools.partial(_swa_kernel, tq=tq, n=n, n_ctx=n_ctx),
        grid=(bsz, SW_KV, n // tq),
        in_specs=[
            pl.BlockSpec(memory_space=pltpu.SMEM),
            pl.BlockSpec((1, tq, gw), lambda b, g, i: (b, i, g)),
            pl.BlockSpec((1, SW_HD, n), lambda b, g, i: (b, g, 0)),
            pl.BlockSpec((1, n, 2 * SW_HD), lambda b, g, i: (b, 0, g)),
        ],
        out_specs=pl.BlockSpec((1, tq, gw), lambda b, g, i: (b, i, g)),
        out_shape=jax.ShapeDtypeStruct((bsz, n, D_MODEL), BF16),
        compiler_params=_cparams(("arbitrary", "arbitrary", "arbitrary")),
    )(sinks.astype(F32), q, kt, va)


def _head_norm(y, g, width):
    parts = []
    for h in range(y.shape[1] // width):
        yh = y[:, h * width:(h + 1) * width]
        parts.append(yh * lax.rsqrt(jnp.mean(yh * yh, axis=-1, keepdims=True) + EPS))
    return jnp.concatenate(parts, axis=1) * g


def _mixer_out_kernel(*refs, mode, n_ctx, tm, post_scale):
    it = iter(refs)
    x_ref, gate_ref, w_ref = next(it), next(it), next(it)
    if mode == "mlstm":
        hh_ref, og_ref, g_ref = next(it), next(it), next(it)
        y = _head_norm(hh_ref[0, 0] + hh_ref[1, 0], g_ref[...], ML_V)
        z = y * _sigmoid(og_ref[0].astype(F32))
    elif mode == "diff":
        a_ref, g_ref = next(it), next(it)
        z = _head_norm(a_ref[0], g_ref[...], DF_VD) * post_scale
    else:
        a_ref = next(it)
        z = a_ref[0]
    o_ref = next(it)
    y = jnp.dot(z.astype(BF16), w_ref[...], preferred_element_type=F32)
    gate = _row_select(gate_ref[0], pl.program_id(1) * tm, tm, n_ctx)
    o_ref[0] = x_ref[0] + gate * y


def _mixer_out(x, gate_pair, w_out, mode, n_ctx, extra, post_scale=1.0):
    bsz, n, d = x.shape
    tm = TOKEN_TILE
    tok = lambda b, t: (b, t, 0)
    in_specs = [pl.BlockSpec((1, tm, d), tok), pl.BlockSpec((1, 2, d), lambda b, t: (b, 0, 0)),
                _const_spec(w_out.shape)]
    args = [x, gate_pair, w_out]
    if mode == "mlstm":
        hh, og, g = extra
        in_specs += [pl.BlockSpec((2, 1, tm, d), lambda b, t: (0, b, t, 0)), pl.BlockSpec((1, tm, d), tok),
                     pl.BlockSpec((1, d), lambda b, t: (0, 0))]
        args += [hh, og, g.reshape(1, d)]
    elif mode == "diff":
        a, g = extra
        in_specs += [pl.BlockSpec((1, tm, d), tok), pl.BlockSpec((1, d), lambda b, t: (0, 0))]
        args += [a, g.reshape(1, d)]
    else:
        (a,) = extra
        in_specs += [pl.BlockSpec((1, tm, d), tok)]
        args += [a]
    kern = functools.partial(_mixer_out_kernel, mode=mode, n_ctx=n_ctx, tm=tm, post_scale=post_scale)
    return pl.pallas_call(
        kern,
        grid=(bsz, n // tm),
        in_specs=in_specs,
        out_specs=pl.BlockSpec((1, tm, d), tok),
        out_shape=jax.ShapeDtypeStruct((bsz, n, d), F32),
        compiler_params=_cparams(("arbitrary", "arbitrary")),
    )(*args)


def _dense_ffn_kernel(x_ref, a_ref, b_ref, gate_ref, wa_ref, wb_ref, wo_ref, o_ref, h_ref, g_ref,
                      *, n_ctx, tm, ff, chunk):
    row0 = pl.program_id(1) * tm
    x = x_ref[0]
    h_ref[...] = _norm_mod(x, a_ref[0], b_ref[0], row0, n_ctx).astype(BF16)
    for c0 in range(0, ff, chunk):
        a = jnp.dot(h_ref[...], wa_ref[:, c0:c0 + chunk], preferred_element_type=F32)
        b = jnp.dot(h_ref[...], wb_ref[:, c0:c0 + chunk], preferred_element_type=F32)
        g_ref[:, c0:c0 + chunk] = (a * _sigmoid(a) * b).astype(BF16)
    y = jnp.dot(g_ref[...], wo_ref[...], preferred_element_type=F32)
    o_ref[0] = x + _row_select(gate_ref[0], row0, tm, n_ctx) * y


def _dense_ffn(x, a_pair, b_pair, gate_pair, wa, wb, wo, n_ctx):
    bsz, n, d = x.shape
    tm = TOKEN_TILE
    ff = wa.shape[1]
    chunk = ff // 2 if (ff // 2) % LANES == 0 else ff
    tok = lambda b, t: (b, t, 0)
    pair = pl.BlockSpec((1, 2, d), lambda b, t: (b, 0, 0))
    kern = functools.partial(_dense_ffn_kernel, n_ctx=n_ctx, tm=tm, ff=ff, chunk=chunk)
    return pl.pallas_call(
        kern,
        grid=(bsz, n // tm),
        in_specs=[pl.BlockSpec((1, tm, d), tok), pair, pair, pair,
                  _const_spec(wa.shape), _const_spec(wb.shape), _const_spec(wo.shape)],
        out_specs=pl.BlockSpec((1, tm, d), tok),
        out_shape=jax.ShapeDtypeStruct((bsz, n, d), F32),
        scratch_shapes=[pltpu.VMEM((tm, d), BF16), pltpu.VMEM((tm, ff), BF16)],
        compiler_params=_cparams(("arbitrary", "arbitrary")),
    )(x, a_pair, b_pair, gate_pair, wa, wb, wo)


def _moe_route_kernel(x_ref, a_ref, b_ref, r_ref, h_ref, g_ref, i_ref, *, n_ctx, tm):
    row0 = pl.program_id(1) * tm
    h = _norm_mod(x_ref[0], a_ref[0], b_ref[0], row0, n_ctx)
    h_ref[0] = h.astype(BF16)
    logits = lax.dot_general(r_ref[...], h, (((1,), (1,)), ((), ())), preferred_element_type=F32,
                             precision=lax.Precision.HIGHEST)
    e_idx = lax.broadcasted_iota(jnp.int32, logits.shape, 0)
    v1 = jnp.max(logits, axis=0, keepdims=True)
    i1 = jnp.min(jnp.where(logits == v1, e_idx, N_EXPERTS), axis=0, keepdims=True)
    rest = jnp.where(e_idx == i1, -jnp.inf, logits)
    v2 = jnp.max(rest, axis=0, keepdims=True)
    i2 = jnp.min(jnp.where(rest == v2, e_idx, N_EXPERTS), axis=0, keepdims=True)
    p2 = jnp.exp(v2 - v1)
    den = 1.0 + p2
    g_ref[0] = jnp.concatenate([1.0 / den, p2 / den], axis=0)
    i_ref[0] = jnp.concatenate([i1, i2], axis=0)


def _moe_route(x, a_pair, b_pair, router_t, n_ctx):
    bsz, n, d = x.shape
    tm = TOKEN_TILE
    tok = lambda b, t: (b, t, 0)
    pair = pl.BlockSpec((1, 2, d), lambda b, t: (b, 0, 0))
    return pl.pallas_call(
        functools.partial(_moe_route_kernel, n_ctx=n_ctx, tm=tm),
        grid=(bsz, n // tm),
        in_specs=[pl.BlockSpec((1, tm, d), tok), pair, pair, _const_spec(router_t.shape)],
        out_specs=[pl.BlockSpec((1, tm, d), tok),
                   pl.BlockSpec((1, TOP_K, tm), lambda b, t: (b, 0, t)),
                   pl.BlockSpec((1, TOP_K, tm), lambda b, t: (b, 0, t))],
        out_shape=[jax.ShapeDtypeStruct((bsz, n, d), BF16),
                   jax.ShapeDtypeStruct((bsz, TOP_K, n), F32),
                   jax.ShapeDtypeStruct((bsz, TOP_K, n), jnp.int32)],
        compiler_params=_cparams(("arbitrary", "arbitrary")),
    )(x, a_pair, b_pair, router_t)


def _moe_expert_kernel(te_ref, nu_ref, x_ref, wi_ref, wo_ref, o_ref, g_ref, *, ff, chunk):
    i = pl.program_id(0)

    @pl.when(i < nu_ref[0])
    def _():
        x = x_ref[...]
        for c0 in range(0, ff, chunk):
            a = jnp.dot(x, wi_ref[0, :, c0:c0 + chunk], preferred_element_type=F32)
            b = jnp.dot(x, wi_ref[0, :, ff + c0:ff + c0 + chunk], preferred_element_type=F32)
            g_ref[:, c0:c0 + chunk] = (a * _sigmoid(a) * b).astype(BF16)
        o_ref[...] = jnp.dot(g_ref[...], wo_ref[0], preferred_element_type=F32).astype(o_ref.dtype)

    @pl.when(i >= nu_ref[0])
    def _():
        o_ref[...] = jnp.zeros_like(o_ref)


def _moe_experts(xs, w_in, w_out, tile_expert, n_used):
    p, d = xs.shape
    tm = MOE_TILE
    ff = w_out.shape[1]
    chunk = MOE_FF_CHUNK
    assert ff % chunk == 0 and chunk % LANES == 0
    nt = p // tm

    def teff(i, nu):
        return jnp.minimum(i, jnp.maximum(nu[0] - 1, 0))

    grid_spec = pltpu.PrefetchScalarGridSpec(
        num_scalar_prefetch=2,
        grid=(nt,),
        in_specs=[
            pl.BlockSpec((tm, d), lambda i, te, nu: (teff(i, nu), 0)),
            pl.BlockSpec((1, d, 2 * ff), lambda i, te, nu: (te[teff(i, nu)], 0, 0),
                         pipeline_mode=pl.Buffered(1)),
            pl.BlockSpec((1, ff, d), lambda i, te, nu: (te[teff(i, nu)], 0, 0),
                         pipeline_mode=pl.Buffered(1)),
        ],
        out_specs=pl.BlockSpec((tm, d), lambda i, te, nu: (i, 0)),
        scratch_shapes=[pltpu.VMEM((tm, ff), BF16)],
    )
    return pl.pallas_call(
        functools.partial(_moe_expert_kernel, ff=ff, chunk=chunk),
        grid_spec=grid_spec,
        out_shape=jax.ShapeDtypeStruct((p, d), BF16),
        compiler_params=_cparams(("arbitrary",)),
    )(tile_expert, n_used, xs, w_in, w_out)


def _moe_combine_kernel(*refs, n_ctx, tm, final):
    it = iter(refs)
    x_ref, gate_ref, y0_ref, y1_ref, pk_ref = next(it), next(it), next(it), next(it), next(it)
    fg_ref = next(it) if final else None
    o_ref = next(it)
    pk = pk_ref[0]
    y = pk[:, 0:1] * y0_ref[0].astype(F32) + pk[:, 1:2] * y1_ref[0].astype(F32)
    x = x_ref[0] + _row_select(gate_ref[0], pl.program_id(1) * tm, tm, n_ctx) * y
    if final:
        x = x * lax.rsqrt(jnp.mean(x * x, axis=-1, keepdims=True) + EPS) * fg_ref[...]
    o_ref[0] = x


def _moe_combine(x, gate_pair, y0, y1, slot_w, n_ctx, final_g=None):
    bsz, n, d = x.shape
    tm = TOKEN_TILE
    tok = lambda b, t: (b, t, 0)
    in_specs = [pl.BlockSpec((1, tm, d), tok), pl.BlockSpec((1, 2, d), lambda b, t: (b, 0, 0)),
                pl.BlockSpec((1, tm, d), tok), pl.BlockSpec((1, tm, d), tok),
                pl.BlockSpec((1, tm, TOP_K), tok)]
    args = [x, gate_pair, y0, y1, slot_w]
    if final_g is not None:
        in_specs.append(pl.BlockSpec((1, d), lambda b, t: (0, 0)))
        args.append(final_g.reshape(1, d))
    return pl.pallas_call(
        functools.partial(_moe_combine_kernel, n_ctx=n_ctx, tm=tm, final=final_g is not None),
        grid=(bsz, n // tm),
        in_specs=in_specs,
        out_specs=pl.BlockSpec((1, tm, d), tok),
        out_shape=jax.ShapeDtypeStruct((bsz, n, d), F32),
        compiler_params=_cparams(("arbitrary", "arbitrary")),
    )(*args)


def _moe_layer(x, a_pair, b_pair, gate_pair, router, w_in, w_out, n_ctx, final_g):
    bsz, n, d = x.shape
    tm = MOE_TILE
    ntok = bsz * n
    h2, gates, idx = _moe_route(x, a_pair, b_pair, router.T, n_ctx)
    e_flat = jnp.moveaxis(idx, 1, 0).reshape(TOP_K * ntok)
    onehot = (e_flat[:, None] == jnp.arange(N_EXPERTS, dtype=jnp.int32)[None, :]).astype(jnp.int32)
    csum = jnp.cumsum(onehot, axis=0)
    counts = csum[-1]
    rank = jnp.sum(onehot * csum, axis=1) - 1
    padded = ((counts + tm - 1) // tm) * tm
    ends = jnp.cumsum(padded)
    starts = ends - padded
    dest = jnp.sum(onehot * starts[None, :], axis=1) + rank
    n_rows = TOP_K * ntok + N_EXPERTS * tm
    n_rows = ((n_rows + tm - 1) // tm) * tm
    tok_of = jnp.tile(jnp.arange(ntok, dtype=jnp.int32), TOP_K)
    src = jnp.zeros((n_rows,), jnp.int32).at[dest].set(tok_of)
    tile_start = jnp.arange(n_rows // tm, dtype=jnp.int32) * tm
    tile_expert = jnp.minimum(jnp.sum((tile_start[:, None] >= ends[None, :]).astype(jnp.int32), axis=1),
                              N_EXPERTS - 1).astype(jnp.int32)
    n_used = (ends[-1] // tm).astype(jnp.int32).reshape(1)
    xs = jnp.take(h2.reshape(ntok, d), src, axis=0)
    ys = _moe_experts(xs, w_in, w_out, tile_expert, n_used)
    y0 = jnp.take(ys, dest[:ntok], axis=0).reshape(bsz, n, d)
    y1 = jnp.take(ys, dest[ntok:], axis=0).reshape(bsz, n, d)
    return _moe_combine(x, gate_pair, y0, y1, jnp.swapaxes(gates, 1, 2), n_ctx, final_g)


def _rope_table(n_ctx, seq):
    pos = jnp.arange(seq)
    row = (pos // GRID_W).astype(F32)
    col = (pos % GRID_W).astype(F32)
    nf = DF_HD // 4
    inv = jnp.power(ROPE_BASE, -jnp.arange(nf, dtype=F32) / nf)
    ang = jnp.concatenate([row[:, None] * inv, col[:, None] * inv], axis=-1)
    cos, sin = jnp.cos(ang), jnp.sin(ang)
    cos = jnp.concatenate([jnp.ones((n_ctx, 2 * nf), F32), cos], axis=0)
    sin = jnp.concatenate([jnp.zeros((n_ctx, 2 * nf), F32), sin], axis=0)
    c128 = jnp.tile(jnp.concatenate([cos, cos], axis=1), (1, 2))
    s128 = jnp.tile(jnp.concatenate([-sin, sin], axis=1), (1, 2))
    return jnp.concatenate([c128, s128], axis=1)


def _pair(v, bsz):
    return jnp.stack([v[:bsz], jnp.broadcast_to(v[bsz], (bsz, v.shape[1]))], axis=1)


def kernel(x, c, ctx, c_ctx, ada_w, ada_b, norm1, norm2, ml_w_in, ml_gate_b, ml_hnorm, ml_w_out,
           df_w_in, df_lam, df_hnorm, df_w_out, sw_w_in, sw_sinks, sw_w_out,
           ffn_w_in, ffn_w_out, moe_router, moe_w_in, moe_w_out, final_norm):
    bsz, seq, d = x.shape
    n_ctx = ctx.shape[1]
    depth = ada_w.shape[0]
    assert d == D_MODEL and n_ctx % ATT_Q_TILE == 0 and (n_ctx + seq) % TOKEN_TILE == 0

    xs = jnp.concatenate([ctx, x], axis=1)
    cond_rows = SUBLANES * ((bsz + 1 + SUBLANES - 1) // SUBLANES)
    cond = jnp.zeros((cond_rows, d), F32).at[:bsz].set(c).at[bsz].set(c_ctx)
    mods = _ada_mod(cond, ada_w, ada_b)
    cs = _rope_table(n_ctx, seq)

    for i in range(depth):
        m = [mods[i, :, k * d:(k + 1) * d] for k in range(6)]
        a1 = _pair(norm1[i][None, :] * (1.0 + m[1]), bsz)
        b1 = _pair(m[0], bsz)
        g1 = _pair(m[2], bsz)
        a2 = _pair(norm2[i][None, :] * (1.0 + m[4]), bsz)
        b2 = _pair(m[3], bsz)
        g2 = _pair(m[5], bsz)
        kind, j = i % N_MIXERS, i // N_MIXERS
        if kind == 0:
            w = ml_w_in[j]
            nmain = 2 * ML_QKW + 2 * D_MODEL
            w_main = jnp.concatenate([w, jnp.zeros((d, LANES - 4 * ML_HEADS), F32)], axis=1).astype(BF16)
            wgt = w[:, nmain:].T.astype(BF16)
            segs = ((0, ML_QKW, False, 1.0), (ML_QKW, 2 * ML_QKW, False, 1.0),
                    (2 * ML_QKW, 2 * ML_QKW + d, False, 1.0), (2 * ML_QKW + d, nmain, False, 1.0),
                    (nmain, nmain + LANES, False, 1.0))
            q, k, v, og, gcol, grow = _project(xs, a1, b1, w_main, segs, (BF16, BF16, BF16, BF16, F32),
                                               n_ctx, wgt=wgt)
            hh = _mlstm_scan(q, k, v, gcol, grow, ml_gate_b[j], n_ctx)
            xs = _mixer_out(xs, g1, ml_w_out[j].astype(BF16), "mlstm", n_ctx, (hh, og, ml_hnorm[j]))
        elif kind == 1:
            lambda_init = 0.8 - 0.6 * math.exp(-0.3 * i)
            segs = ((0, DF_QKW, True, DF_HD ** -0.5 * LOG2E), (DF_QKW, 2 * DF_QKW, True, 1.0),
                    (2 * DF_QKW, 2 * DF_QKW + d, False, 1.0))
            q, k, v = _project(xs, a1, b1, df_w_in[j].astype(BF16), segs, (BF16, BF16, BF16), n_ctx, cs=cs)
            att = _diff_attention(q, k, v, df_lam[j], lambda_init, n_ctx)
            xs = _mixer_out(xs, g1, df_w_out[j].astype(BF16), "diff", n_ctx, (att, df_hnorm[j]),
                            post_scale=1.0 - lambda_init)
        else:
            nq = SW_HEADS * SW_HD
            nkv = SW_KV * SW_HD
            segs = ((0, nq, True, SW_HD ** -0.5), (nq, nq + nkv, True, 1.0), (nq + nkv, nq + 2 * nkv, False, 1.0))
            q, k, v = _project(xs, a1, b1, sw_w_in[j].astype(BF16), segs, (BF16, BF16, BF16), n_ctx, cs=cs)
            att = _swa_attention(q, k, v, sw_sinks[j], n_ctx)
            xs = _mixer_out(xs, g1, sw_w_out[j].astype(BF16), "plain", n_ctx, (att,))
        jf = i // 2
        last = i == depth - 1
        if i % 2 == 0:
            ff = ffn_w_out.shape[1]
            wi = ffn_w_in[jf].astype(BF16)
            xs = _dense_ffn(xs, a2, b2, g2, wi[:, :ff], wi[:, ff:], ffn_w_out[jf].astype(BF16), n_ctx)
            if last:
                raise NotImplementedError("final norm is fused into the MoE combine")
        else:
            xs = _moe_layer(xs, a2, b2, g2, moe_router[jf], moe_w_in[jf].astype(BF16),
                            moe_w_out[jf].astype(BF16), n_ctx, final_norm if last else None)
    return xs[:, n_ctx:]
```

```python
import functools
import math

import jax
import jax.numpy as jnp
from jax import lax
from jax.experimental import pallas as pl
from jax.experimental.pallas import tpu as pltpu

F32 = jnp.float32
BF16 = jnp.bfloat16

D_MODEL = 1024
GRID_W = 64
EPS = 1e-6
NEG_INF = -1e30
ROPE_BASE = 10000.0
LOG2E = math.log2(math.e)
N_MIXERS = 3

ML_HEADS = 8
ML_V = D_MODEL // ML_HEADS
ML_QK = ML_V // 2
ML_QKW = ML_HEADS * ML_QK
ML_CHUNK = 128

DF_HD = 64
DF_HEADS = D_MODEL // (2 * DF_HD)
DF_VD = 2 * DF_HD
DF_QKW = 2 * DF_HEADS * DF_HD

SW_HD = 64
SW_HEADS = D_MODEL // SW_HD
SW_KV = 4
SW_GROUP = SW_HEADS // SW_KV
SW_WIN = 128

N_EXPERTS = 8
TOP_K = 2

LANES = 128
SUBLANES = 8
MXU_WIDTH = 256
V7X_VMEM_BYTES = 64 * 1024 * 1024
VMEM_LIMIT = V7X_VMEM_BYTES - 8 * 1024 * 1024

TOKEN_TILE = 640
ATT_Q_TILE = 256
ATT_KV_TILE = 1280
SWA_Q_TILE = 256
MOE_TILE = 512
MOE_FF_CHUNK = 1792


def _cparams(semantics):
    return pltpu.CompilerParams(dimension_semantics=semantics, vmem_limit_bytes=VMEM_LIMIT)


def _const_spec(shape):
    nd = len(shape)
    return pl.BlockSpec(shape, lambda *_: (0,) * nd, pipeline_mode=pl.Buffered(1))


def _sigmoid(x):
    return 1.0 / (1.0 + jnp.exp(-x))


def _log_sigmoid(x):
    return jnp.minimum(x, 0.0) - jnp.log(1.0 + jnp.exp(-jnp.abs(x)))


def _norm_mod(x, a_rows, b_rows, row0, n_ctx):
    xn = x * lax.rsqrt(jnp.mean(x * x, axis=-1, keepdims=True) + EPS)
    is_ctx = (row0 + lax.broadcasted_iota(jnp.int32, (x.shape[0], 1), 0)) < n_ctx
    a = jnp.where(is_ctx, a_rows[1:2], a_rows[0:1])
    b = jnp.where(is_ctx, b_rows[1:2], b_rows[0:1])
    return xn * a + b


def _row_select(rows, row0, n_rows, n_ctx):
    is_ctx = (row0 + lax.broadcasted_iota(jnp.int32, (n_rows, 1), 0)) < n_ctx
    return jnp.where(is_ctx, rows[1:2], rows[0:1])


def _ada_kernel(c_ref, w_ref, b_ref, o_ref):
    c = c_ref[...]
    s = c * _sigmoid(c)
    o_ref[0] = jnp.dot(s, w_ref[0], preferred_element_type=F32,
                       precision=lax.Precision.HIGHEST) + b_ref[0]


def _ada_mod(cond, ada_w, ada_b):
    depth, d, six_d = ada_w.shape
    rows = cond.shape[0]
    col = D_MODEL
    return pl.pallas_call(
        _ada_kernel,
        grid=(depth, six_d // col),
        in_specs=[
            pl.BlockSpec((rows, d), lambda i, j: (0, 0)),
            pl.BlockSpec((1, d, col), lambda i, j: (i, 0, j)),
            pl.BlockSpec((1, 1, col), lambda i, j: (i, 0, j)),
        ],
        out_specs=pl.BlockSpec((1, rows, col), lambda i, j: (i, 0, j)),
        out_shape=jax.ShapeDtypeStruct((depth, rows, six_d), F32),
        compiler_params=_cparams(("arbitrary", "arbitrary")),
    )(cond, ada_w, ada_b.reshape(depth, 1, six_d))


def _rope(r, cos, sin):
    w = r.shape[1]
    lane = lax.broadcasted_iota(jnp.int32, r.shape, 1)
    swapped = jnp.where((lane & 32) == 0, pltpu.roll(r, w - 32, 1), pltpu.roll(r, 32, 1))
    reps = w // LANES
    return r * jnp.tile(cos, (1, reps)) + swapped * jnp.tile(sin, (1, reps))


def _proj_kernel(*refs, segs, n_ctx, tm, has_rope, has_grow):
    it = iter(refs)
    x_ref, a_ref, b_ref, w_ref = next(it), next(it), next(it), next(it)
    cs_ref = next(it) if has_rope else None
    wgt_ref = next(it) if has_grow else None
    outs = [next(it) for _ in segs]
    grow_ref = next(it) if has_grow else None
    h_ref = next(it)

    row0 = pl.program_id(1) * tm
    h_ref[...] = _norm_mod(x_ref[0], a_ref[0], b_ref[0], row0, n_ctx).astype(BF16)
    for (c0, c1, rope, scale), o_ref in zip(segs, outs):
        r = jnp.dot(h_ref[...], w_ref[:, c0:c1], preferred_element_type=F32)
        if rope:
            r = _rope(r, cs_ref[:, :LANES], cs_ref[:, LANES:])
        if scale != 1.0:
            r = r * scale
        o_ref[0] = r.astype(o_ref.dtype)
    if has_grow:
        grow_ref[0] = lax.dot_general(wgt_ref[...], h_ref[...], (((1,), (1,)), ((), ())),
                                      preferred_element_type=F32)


def _project(x, a_pair, b_pair, w, segs, out_dtypes, n_ctx, cs=None, wgt=None):
    bsz, n, d = x.shape
    tm = TOKEN_TILE
    assert n % tm == 0
    tok = lambda b, t: (b, t, 0)
    in_specs = [
        pl.BlockSpec((1, tm, d), tok),
        pl.BlockSpec((1, 2, d), lambda b, t: (b, 0, 0)),
        pl.BlockSpec((1, 2, d), lambda b, t: (b, 0, 0)),
        _const_spec(w.shape),
    ]
    args = [x, a_pair, b_pair, w]
    if cs is not None:
        in_specs.append(pl.BlockSpec((tm, 2 * LANES), lambda b, t: (t, 0)))
        args.append(cs)
    if wgt is not None:
        in_specs.append(_const_spec(wgt.shape))
        args.append(wgt)
    out_specs = [pl.BlockSpec((1, tm, c1 - c0), tok) for (c0, c1, _, _) in segs]
    out_shape = [jax.ShapeDtypeStruct((bsz, n, c1 - c0), dt) for (c0, c1, _, _), dt in zip(segs, out_dtypes)]
    if wgt is not None:
        out_specs.append(pl.BlockSpec((1, wgt.shape[0], tm), lambda b, t: (b, 0, t)))
        out_shape.append(jax.ShapeDtypeStruct((bsz, wgt.shape[0], n), F32))
    kern = functools.partial(_proj_kernel, segs=segs, n_ctx=n_ctx, tm=tm,
                             has_rope=cs is not None, has_grow=wgt is not None)
    return pl.pallas_call(
        kern,
        grid=(bsz, n // tm),
        in_specs=in_specs,
        out_specs=out_specs,
        out_shape=out_shape,
        scratch_shapes=[pltpu.VMEM((tm, d), BF16)],
        compiler_params=_cparams(("arbitrary", "arbitrary")),
    )(*args)


def _split3(x):
    hi = x.astype(BF16)
    r1 = x - hi.astype(F32)
    mid = r1.astype(BF16)
    lo = (r1 - mid.astype(F32)).astype(BF16)
    return hi, mid, lo


_TN = (((0,), (0,)), ((), ()))
_NT = (((1,), (1,)), ((), ()))


def _expand_heads(x, width):
    nh = x.shape[0]
    x3 = jnp.concatenate(_split3(x), axis=0)
    r = lax.broadcasted_iota(jnp.int32, (3 * nh, nh * width), 0)
    c = lax.broadcasted_iota(jnp.int32, (3 * nh, nh * width), 1)
    sel = jnp.where((r & (nh - 1)) == (c >> int(math.log2(width))), 1.0, 0.0).astype(BF16)
    return lax.dot_general(x3, sel, _TN, preferred_element_type=F32)


def _mlstm_kernel(q_ref, k_ref, v_ref, gr_ref, br_ref, o_ref, ct_ref, m_ref, *, t):
    nh = ML_HEADS
    d = pl.program_id(1)
    step = pl.program_id(2)

    @pl.when(step == 0)
    def _():
        ct_ref[...] = jnp.zeros_like(ct_ref)
        m_ref[...] = jnp.zeros_like(m_ref)

    fwd = d == 0
    gr = gr_ref[0] + br_ref[...]
    gr = jnp.where(fwd, gr[:2 * nh], gr[2 * nh:])
    ig = gr[:nh]
    fg = _log_sigmoid(gr[nh:])

    sgn = 1 - 2 * d
    row = lax.broadcasted_iota(jnp.int32, (t, t), 0)
    col = lax.broadcasted_iota(jnp.int32, (t, t), 1)
    mask = (col - row) * sgn <= 0
    tri_t = jnp.where((row - col) * sgn <= 0, 1.0, 0.0).astype(BF16)
    bs = jnp.dot(jnp.concatenate(_split3(fg), axis=0), tri_t, preferred_element_type=F32)
    b = bs[:nh] + bs[nh:2 * nh] + bs[2 * nh:]
    b_end = jnp.sum(fg, axis=1, keepdims=True)
    src = ig - b

    lane = lax.broadcasted_iota(jnp.int32, (nh, t), 1)
    scan_pos = jnp.where(fwd, lane, t - 1 - lane)
    cmax = src
    shift = 1
    while shift < t:
        moved = jnp.where(fwd, pltpu.roll(cmax, shift, 1), pltpu.roll(cmax, t - shift, 1))
        cmax = jnp.maximum(cmax, jnp.where(scan_pos >= shift, moved, -jnp.inf))
        shift *= 2

    m_in = m_ref[:, 0:1]
    big_m = jnp.maximum(m_in, cmax)
    w_end = b_end + src
    m_loc = jnp.max(w_end, axis=1, keepdims=True)
    e_end = jnp.exp(w_end - m_loc)
    m_new = jnp.maximum(b_end + m_in, m_loc)
    a_dec = jnp.exp(b_end + m_in - m_new)
    s_dec = jnp.exp(m_loc - m_new)

    q = q_ref[0]
    k = k_ref[0] * (ML_QK ** -0.5)
    v = v_ref[0]
    q_in = (q.astype(F32) * _expand_heads(jnp.exp(m_in - big_m), ML_QK)).astype(BF16)
    k_end = (k.astype(F32) * _expand_heads(e_end, ML_QK)).astype(BF16)
    floor = _expand_heads(jnp.exp(-(b + big_m)), ML_V)

    ones3 = jnp.ones((3 * nh, t), BF16)
    dec_l = jnp.concatenate(list(_split3(-big_m)) + [ones3], axis=0)
    dec_r = jnp.concatenate([ones3] + list(_split3(src)), axis=0)
    head_of_row = lax.broadcasted_iota(jnp.int32, (6 * nh, t), 0) & (nh - 1)
    ones = jnp.ones((t, ML_V), BF16)
    nums, dens = [], []
    for h in range(nh):
        qk = slice(h * ML_QK, (h + 1) * ML_QK)
        va = jnp.concatenate([v[:, h * ML_V:(h + 1) * ML_V], ones], axis=1)
        s = lax.dot_general(q[:, qk], k[:, qk], _NT, preferred_element_type=F32)
        w_log = lax.dot_general(dec_l, jnp.where(head_of_row == h, dec_r, jnp.zeros_like(dec_r)), _TN,
                                preferred_element_type=F32)
        w = (jnp.where(mask, jnp.exp(w_log), 0.0) * s).astype(BF16)
        ct = ct_ref[h]
        r = (jnp.dot(w, va, preferred_element_type=F32)
             + jnp.dot(q_in[:, qk], ct.astype(BF16), preferred_element_type=F32))
        nums.append(r[:, :ML_V])
        dens.append(r[:, ML_V:])
        c_loc = lax.dot_general(k_end[:, qk], va, _TN, preferred_element_type=F32)
        ct_ref[h] = a_dec[h:h + 1] * ct + s_dec[h:h + 1] * c_loc
    num = jnp.concatenate(nums, axis=1)
    den = jnp.concatenate(dens, axis=1)
    o_ref[0, 0] = num / jnp.maximum(jnp.abs(den), floor)
    m_ref[...] = jnp.broadcast_to(m_new, m_ref.shape)


def _mlstm_scan(q, k, v, grow, gate_b, n_ctx):
    bsz, n, _ = q.shape
    t = ML_CHUNK
    assert t & (t - 1) == 0
    nc = n // t
    ncc = n_ctx // t

    def chunk(d, i):
        bwd = jnp.where(i < ncc, ncc - 1 - i, nc - 1 - (i - ncc))
        return jnp.where(d == 0, i, bwd)

    bias_row = jnp.broadcast_to(gate_b.reshape(4 * ML_HEADS, 1).astype(F32), (4 * ML_HEADS, t))
    tok = lambda b, d, i: (b, chunk(d, i), 0)
    return pl.pallas_call(
        functools.partial(_mlstm_kernel, t=t),
        grid=(bsz, 2, nc),
        in_specs=[
            pl.BlockSpec((1, t, ML_QKW), tok),
            pl.BlockSpec((1, t, ML_QKW), tok),
            pl.BlockSpec((1, t, D_MODEL), tok),
            pl.BlockSpec((1, 4 * ML_HEADS, t), lambda b, d, i: (b, 0, chunk(d, i))),
            pl.BlockSpec((4 * ML_HEADS, t), lambda b, d, i: (0, 0)),
        ],
        out_specs=pl.BlockSpec((1, 1, t, D_MODEL), lambda b, d, i: (d, b, chunk(d, i), 0)),
        out_shape=jax.ShapeDtypeStruct((2, bsz, n, D_MODEL), F32),
        scratch_shapes=[pltpu.VMEM((ML_HEADS, ML_QK, 2 * ML_V), F32), pltpu.VMEM((ML_HEADS, LANES), F32)],
        compiler_params=_cparams(("arbitrary", "arbitrary", "arbitrary")),
    )(q, k, v, grow, bias_row)


def _diff_kernel(lam_ref, q_ref, kt_ref, v_ref, o_ref, acc_ref, m_ref, s_ref, va_ref, *, tq, tk, n, n_ctx,
                 lambda_init):
    qi = pl.program_id(2)

    @pl.when(qi == 0)
    def _():
        def fill(c, carry):
            rows = pl.ds(pl.multiple_of(c * tk, tk), tk)
            va_ref[rows, :DF_VD] = v_ref[0, rows, :]
            va_ref[rows, DF_VD:] = jnp.ones((tk, DF_VD), va_ref.dtype)
            return carry
        lax.fori_loop(0, n // tk, fill, 0)

    q = q_ref[0]
    lane = lax.broadcasted_iota(jnp.int32, q.shape, 1)
    zero = jnp.zeros_like(q)
    qs = (jnp.where(lane < DF_HD, q, zero), jnp.where(lane < DF_HD, zero, q))
    acc_ref[...] = jnp.zeros_like(acc_ref)
    m_ref[...] = jnp.full_like(m_ref, NEG_INF)

    def scores(start, size, slot):
        kt = kt_ref[0, :, pl.ds(start, size)]
        for j in range(2):
            s_ref[slot, j, :, :size] = jnp.dot(qs[j], kt, preferred_element_type=F32)

    def accumulate(start, size, slot):
        va = va_ref[pl.ds(start, size), :]
        ss = [s_ref[slot, j, :, :size] for j in range(2)]
        m_old = [m_ref[j] for j in range(2)]
        m_new = [jnp.maximum(m_old[j], jnp.max(ss[j], axis=1, keepdims=True)) for j in range(2)]
        ps = [jnp.exp2((ss[j] - m_new[j]).astype(BF16)) for j in range(2)]
        pv = [jnp.dot(ps[j], va, preferred_element_type=F32) for j in range(2)]
        for j in range(2):
            acc_ref[j] = jnp.exp2(m_old[j] - m_new[j]) * acc_ref[j] + pv[j]
            m_ref[j] = m_new[j]

    is_ctx_tile = qi * tq < n_ctx

    @pl.when(is_ctx_tile)
    def _():
        scores(0, n_ctx, 0)
        accumulate(0, n_ctx, 0)

    @pl.when(jnp.logical_not(is_ctx_tile))
    def _():
        n_chunks = n // tk
        scores(0, tk, 0)

        def body(i, carry):
            c0 = pl.multiple_of(2 * i * tk, tk)
            scores(c0 + tk, tk, 1)
            accumulate(c0, tk, 0)
            scores(c0 + 2 * tk, tk, 0)
            accumulate(c0 + tk, tk, 1)
            return carry
        lax.fori_loop(0, (n_chunks - 1) // 2, body, 0)
        accumulate((n_chunks - 1) * tk, tk, 0)

    lam = lam_ref[...]
    lam_full = (jnp.exp(jnp.sum(lam[0:1] * lam[1:2], axis=1, keepdims=True))
                - jnp.exp(jnp.sum(lam[2:3] * lam[3:4], axis=1, keepdims=True)) + lambda_init)
    o1 = acc_ref[0, :, :DF_VD] / acc_ref[0, :, DF_VD:]
    o2 = acc_ref[1, :, :DF_VD] / acc_ref[1, :, DF_VD:]
    o_ref[0] = o1 - lam_full * o2


def _diff_attention(q, k, v, lam, lambda_init, n_ctx):
    bsz, n, _ = q.shape
    tq, tk = ATT_Q_TILE, ATT_KV_TILE
    assert n_ctx % tq == 0 and n % tq == 0 and n % tk == 0 and (n // tk) % 2 == 1 and n_ctx <= tk
    kt = jnp.swapaxes(k, 1, 2)
    kern = functools.partial(_diff_kernel, tq=tq, tk=tk, n=n, n_ctx=n_ctx, lambda_init=lambda_init)
    return pl.pallas_call(
        kern,
        grid=(bsz, DF_HEADS, n // tq),
        in_specs=[
            pl.BlockSpec(lam.shape, lambda b, h, i: (0, 0)),
            pl.BlockSpec((1, tq, 2 * DF_HD), lambda b, h, i: (b, i, h)),
            pl.BlockSpec((1, 2 * DF_HD, n), lambda b, h, i: (b, h, 0)),
            pl.BlockSpec((1, n, DF_VD), lambda b, h, i: (b, 0, h)),
        ],
        out_specs=pl.BlockSpec((1, tq, DF_VD), lambda b, h, i: (b, i, h)),
        out_shape=jax.ShapeDtypeStruct((bsz, n, D_MODEL), F32),
        scratch_shapes=[pltpu.VMEM((2, tq, 2 * DF_VD), F32), pltpu.VMEM((2, tq, 1), F32),
                        pltpu.VMEM((2, 2, tq, tk), F32), pltpu.VMEM((n, 2 * DF_VD), v.dtype)],
        compiler_params=_cparams(("arbitrary", "arbitrary", "arbitrary")),
    )(lam.astype(F32), q, kt, v)


def _swa_kernel(sink_ref, q_ref, kt_ref, va_ref, o_ref, *, tq, n, n_ctx):
    g = pl.program_id(1)
    i = pl.program_id(2)
    bw = tq + 2 * SW_WIN
    q = q_ref[0]
    qh = jnp.concatenate([q[:, u * SW_HD:(u + 1) * SW_HD] for u in range(SW_GROUP)], axis=0)
    start = pl.multiple_of(jnp.clip(i * tq - SW_WIN, 0, n - bw), SW_WIN)
    s_c = jnp.dot(qh, kt_ref[0, :, 0:n_ctx], preferred_element_type=F32)
    s_b = jnp.dot(qh, kt_ref[0, :, pl.ds(start, bw)], preferred_element_type=F32)
    row = lax.broadcasted_iota(jnp.int32, s_b.shape, 0)
    q_abs = i * tq + (row & (tq - 1))
    k_abs = start + lax.broadcasted_iota(jnp.int32, s_b.shape, 1)
    ok = jnp.logical_and(jnp.logical_and(i * tq >= n_ctx, k_abs >= n_ctx), jnp.abs(q_abs - k_abs) <= SW_WIN)
    s_b = jnp.where(ok, s_b, NEG_INF)
    head = lax.broadcasted_iota(jnp.int32, (SW_GROUP * tq, 1), 0) // tq
    sink = jnp.zeros((SW_GROUP * tq, 1), F32)
    for u in range(SW_GROUP):
        sink = jnp.where(head == u, sink_ref[g * SW_GROUP + u], sink)
    m = jnp.maximum(sink, jnp.maximum(jnp.max(s_c, axis=1, keepdims=True), jnp.max(s_b, axis=1, keepdims=True)))
    acc = (jnp.dot(jnp.exp(s_c - m).astype(BF16), va_ref[0, 0:n_ctx, :], preferred_element_type=F32)
           + jnp.dot(jnp.exp(s_b - m).astype(BF16), va_ref[0, pl.ds(start, bw), :], preferred_element_type=F32))
    out = acc[:, :SW_HD] / (acc[:, SW_HD:] + jnp.exp(sink - m))
    o_ref[0] = jnp.concatenate([out[u * tq:(u + 1) * tq] for u in range(SW_GROUP)], axis=1).astype(o_ref.dtype)


def _swa_attention(q, k, v, sinks, n_ctx):
    bsz, n, _ = q.shape
    tq = SWA_Q_TILE
    assert tq & (tq - 1) == 0 and n_ctx % tq == 0 and n % tq == 0 and tq % SW_WIN == 0
    gw = SW_GROUP * SW_HD
    kt = jnp.swapaxes(k, 1, 2)
    va = jnp.concatenate([v.reshape(bsz, n, SW_KV, SW_HD),
                          jnp.ones((bsz, n, SW_KV, SW_HD), v.dtype)], axis=-1).reshape(bsz, n, SW_KV * 2 * SW_HD)
    return pl.pallas_call(
        functools.partial(_swa_kernel, tq=tq, n=n, n_ctx=n_ctx),
        grid=(bsz, SW_KV, n // tq),
        in_specs=[
            pl.BlockSpec(memory_space=pltpu.SMEM),
            pl.BlockSpec((1, tq, gw), lambda b, g, i: (b, i, g)),
            pl.BlockSpec((1, SW_HD, n), lambda b, g, i: (b, g, 0)),
            pl.BlockSpec((1, n, 2 * SW_HD), lambda b, g, i: (b, 0, g)),
        ],
        out_specs=pl.BlockSpec((1, tq, gw), lambda b, g, i: (b, i, g)),
        out_shape=jax.ShapeDtypeStruct((bsz, n, D_MODEL), BF16),
        compiler_params=_cparams(("arbitrary", "arbitrary", "arbitrary")),
    )(sinks.astype(F32), q, kt, va)


def _head_norm(y, g, width):
    parts = []
    for h in range(y.shape[1] // width):
        yh = y[:, h * width:(h + 1) * width]
        parts.append(yh * lax.rsqrt(jnp.mean(yh * yh, axis=-1, keepdims=True) + EPS))
    return jnp.concatenate(parts, axis=1) * g


def _mixer_out_kernel(*refs, mode, n_ctx, tm, post_scale):
    it = iter(refs)
    x_ref, gate_ref, w_ref = next(it), next(it), next(it)
    if mode == "mlstm":
        hh_ref, og_ref, g_ref = next(it), next(it), next(it)
        y = _head_norm(hh_ref[0, 0] + hh_ref[1, 0], g_ref[...], ML_V)
        z = y * _sigmoid(og_ref[0].astype(F32))
    elif mode == "diff":
        a_ref, g_ref = next(it), next(it)
        z = _head_norm(a_ref[0], g_ref[...], DF_VD) * post_scale
    else:
        a_ref = next(it)
        z = a_ref[0]
    o_ref = next(it)
    y = jnp.dot(z.astype(BF16), w_ref[...], preferred_element_type=F32)
    gate = _row_select(gate_ref[0], pl.program_id(1) * tm, tm, n_ctx)
    o_ref[0] = x_ref[0] + gate * y


def _mixer_out(x, gate_pair, w_out, mode, n_ctx, extra, post_scale=1.0):
    bsz, n, d = x.shape
    tm = TOKEN_TILE
    tok = lambda b, t: (b, t, 0)
    in_specs = [pl.BlockSpec((1, tm, d), tok), pl.BlockSpec((1, 2, d), lambda b, t: (b, 0, 0)),
                _const_spec(w_out.shape)]
    args = [x, gate_pair, w_out]
    if mode == "mlstm":
        hh, og, g = extra
        in_specs += [pl.BlockSpec((2, 1, tm, d), lambda b, t: (0, b, t, 0)), pl.BlockSpec((1, tm, d), tok),
                     pl.BlockSpec((1, d), lambda b, t: (0, 0))]
        args += [hh, og, g.reshape(1, d)]
    elif mode == "diff":
        a, g = extra
        in_specs += [pl.BlockSpec((1, tm, d), tok), pl.BlockSpec((1, d), lambda b, t: (0, 0))]
        args += [a, g.reshape(1, d)]
    else:
        (a,) = extra
        in_specs += [pl.BlockSpec((1, tm, d), tok)]
        args += [a]
    kern = functools.partial(_mixer_out_kernel, mode=mode, n_ctx=n_ctx, tm=tm, post_scale=post_scale)
    return pl.pallas_call(
        kern,
        grid=(bsz, n // tm),
        in_specs=in_specs,
        out_specs=pl.BlockSpec((1, tm, d), tok),
        out_shape=jax.ShapeDtypeStruct((bsz, n, d), F32),
        compiler_params=_cparams(("arbitrary", "arbitrary")),
    )(*args)


def _dense_ffn_kernel(x_ref, a_ref, b_ref, gate_ref, wa_ref, wb_ref, wo_ref, o_ref, h_ref, g_ref,
                      *, n_ctx, tm, bounds):
    row0 = pl.program_id(1) * tm
    x = x_ref[0]
    h_ref[...] = _norm_mod(x, a_ref[0], b_ref[0], row0, n_ctx).astype(BF16)
    for c0, c1 in zip(bounds[:-1], bounds[1:]):
        a = jnp.dot(h_ref[...], wa_ref[:, c0:c1], preferred_element_type=F32)
        b = jnp.dot(h_ref[...], wb_ref[:, c0:c1], preferred_element_type=F32)
        g_ref[:, c0:c1] = (a * _sigmoid(a) * b).astype(BF16)
    y = jnp.dot(g_ref[...], wo_ref[...], preferred_element_type=F32)
    o_ref[0] = x + _row_select(gate_ref[0], row0, tm, n_ctx) * y


def _dense_ffn(x, a_pair, b_pair, gate_pair, wa, wb, wo, n_ctx):
    bsz, n, d = x.shape
    tm = TOKEN_TILE
    ff = wa.shape[1]
    half = (pl.cdiv(ff // MXU_WIDTH, 2) * MXU_WIDTH) if ff % MXU_WIDTH == 0 else ff
    bounds = (0, half, ff) if half < ff else (0, ff)
    tok = lambda b, t: (b, t, 0)
    pair = pl.BlockSpec((1, 2, d), lambda b, t: (b, 0, 0))
    kern = functools.partial(_dense_ffn_kernel, n_ctx=n_ctx, tm=tm, bounds=bounds)
    return pl.pallas_call(
        kern,
        grid=(bsz, n // tm),
        in_specs=[pl.BlockSpec((1, tm, d), tok), pair, pair, pair,
                  _const_spec(wa.shape), _const_spec(wb.shape), _const_spec(wo.shape)],
        out_specs=pl.BlockSpec((1, tm, d), tok),
        out_shape=jax.ShapeDtypeStruct((bsz, n, d), F32),
        scratch_shapes=[pltpu.VMEM((tm, d), BF16), pltpu.VMEM((tm, ff), BF16)],
        compiler_params=_cparams(("arbitrary", "arbitrary")),
    )(x, a_pair, b_pair, gate_pair, wa, wb, wo)


def _moe_route_kernel(x_ref, a_ref, b_ref, r_ref, h_ref, g_ref, i_ref, *, n_ctx, tm):
    row0 = pl.program_id(1) * tm
    h = _norm_mod(x_ref[0], a_ref[0], b_ref[0], row0, n_ctx)
    h_ref[0] = h.astype(BF16)
    logits = lax.dot_general(r_ref[...], h, (((1,), (1,)), ((), ())), preferred_element_type=F32,
                             precision=lax.Precision.HIGHEST)
    e_idx = lax.broadcasted_iota(jnp.int32, logits.shape, 0)
    v1 = jnp.max(logits, axis=0, keepdims=True)
    i1 = jnp.min(jnp.where(logits == v1, e_idx, N_EXPERTS), axis=0, keepdims=True)
    rest = jnp.where(e_idx == i1, -jnp.inf, logits)
    v2 = jnp.max(rest, axis=0, keepdims=True)
    i2 = jnp.min(jnp.where(rest == v2, e_idx, N_EXPERTS), axis=0, keepdims=True)
    p2 = jnp.exp(v2 - v1)
    den = 1.0 + p2
    g_ref[0] = jnp.concatenate([1.0 / den, p2 / den], axis=0)
    i_ref[0] = jnp.concatenate([i1, i2], axis=0)


def _moe_route(x, a_pair, b_pair, router_t, n_ctx):
    bsz, n, d = x.shape
    tm = TOKEN_TILE
    tok = lambda b, t: (b, t, 0)
    pair = pl.BlockSpec((1, 2, d), lambda b, t: (b, 0, 0))
    return pl.pallas_call(
        functools.partial(_moe_route_kernel, n_ctx=n_ctx, tm=tm),
        grid=(bsz, n // tm),
        in_specs=[pl.BlockSpec((1, tm, d), tok), pair, pair, _const_spec(router_t.shape)],
        out_specs=[pl.BlockSpec((1, tm, d), tok),
                   pl.BlockSpec((1, TOP_K, tm), lambda b, t: (b, 0, t)),
                   pl.BlockSpec((1, TOP_K, tm), lambda b, t: (b, 0, t))],
        out_shape=[jax.ShapeDtypeStruct((bsz, n, d), BF16),
                   jax.ShapeDtypeStruct((bsz, TOP_K, n), F32),
                   jax.ShapeDtypeStruct((bsz, TOP_K, n), jnp.int32)],
        compiler_params=_cparams(("arbitrary", "arbitrary")),
    )(x, a_pair, b_pair, router_t)


def _moe_expert_kernel(te_ref, nu_ref, x_ref, wi_ref, wo_ref, o_ref, g_ref, *, ff, chunk):
    i = pl.program_id(0)

    @pl.when(i < nu_ref[0])
    def _():
        x = x_ref[...]
        for c0 in range(0, ff, chunk):
            a = jnp.dot(x, wi_ref[0, :, c0:c0 + chunk], preferred_element_type=F32)
            b = jnp.dot(x, wi_ref[0, :, ff + c0:ff + c0 + chunk], preferred_element_type=F32)
            g_ref[:, c0:c0 + chunk] = (a * _sigmoid(a) * b).astype(BF16)
        o_ref[...] = jnp.dot(g_ref[...], wo_ref[0], preferred_element_type=F32).astype(o_ref.dtype)

    @pl.when(i >= nu_ref[0])
    def _():
        o_ref[...] = jnp.zeros_like(o_ref)


def _moe_experts(xs, w_in, w_out, tile_expert, n_used):
    p, d = xs.shape
    tm = MOE_TILE
    ff = w_out.shape[1]
    chunk = MOE_FF_CHUNK
    assert ff % chunk == 0 and chunk % LANES == 0
    nt = p // tm

    def teff(i, nu):
        return jnp.minimum(i, jnp.maximum(nu[0] - 1, 0))

    grid_spec = pltpu.PrefetchScalarGridSpec(
        num_scalar_prefetch=2,
        grid=(nt,),
        in_specs=[
            pl.BlockSpec((tm, d), lambda i, te, nu: (teff(i, nu), 0)),
            pl.BlockSpec((1, d, 2 * ff), lambda i, te, nu: (te[teff(i, nu)], 0, 0),
                         pipeline_mode=pl.Buffered(1)),
            pl.BlockSpec((1, ff, d), lambda i, te, nu: (te[teff(i, nu)], 0, 0),
                         pipeline_mode=pl.Buffered(1)),
        ],
        out_specs=pl.BlockSpec((tm, d), lambda i, te, nu: (i, 0)),
        scratch_shapes=[pltpu.VMEM((tm, ff), BF16)],
    )
    return pl.pallas_call(
        functools.partial(_moe_expert_kernel, ff=ff, chunk=chunk),
        grid_spec=grid_spec,
        out_shape=jax.ShapeDtypeStruct((p, d), BF16),
        compiler_params=_cparams(("arbitrary",)),
    )(tile_expert, n_used, xs, w_in, w_out)


def _moe_combine_kernel(*refs, n_ctx, tm, final):
    it = iter(refs)
    x_ref, gate_ref, y0_ref, y1_ref, pk_ref = next(it), next(it), next(it), next(it), next(it)
    fg_ref = next(it) if final else None
    o_ref = next(it)
    pk = pk_ref[0]
    y = pk[:, 0:1] * y0_ref[0].astype(F32) + pk[:, 1:2] * y1_ref[0].astype(F32)
    x = x_ref[0] + _row_select(gate_ref[0], pl.program_id(1) * tm, tm, n_ctx) * y
    if final:
        x = x * lax.rsqrt(jnp.mean(x * x, axis=-1, keepdims=True) + EPS) * fg_ref[...]
    o_ref[0] = x


def _moe_combine(x, gate_pair, y0, y1, slot_w, n_ctx, final_g=None):
    bsz, n, d = x.shape
    tm = TOKEN_TILE
    tok = lambda b, t: (b, t, 0)
    in_specs = [pl.BlockSpec((1, tm, d), tok), pl.BlockSpec((1, 2, d), lambda b, t: (b, 0, 0)),
                pl.BlockSpec((1, tm, d), tok), pl.BlockSpec((1, tm, d), tok),
                pl.BlockSpec((1, tm, TOP_K), tok)]
    args = [x, gate_pair, y0, y1, slot_w]
    if final_g is not None:
        in_specs.append(pl.BlockSpec((1, d), lambda b, t: (0, 0)))
        args.append(final_g.reshape(1, d))
    return pl.pallas_call(
        functools.partial(_moe_combine_kernel, n_ctx=n_ctx, tm=tm, final=final_g is not None),
        grid=(bsz, n // tm),
        in_specs=in_specs,
        out_specs=pl.BlockSpec((1, tm, d), tok),
        out_shape=jax.ShapeDtypeStruct((bsz, n, d), F32),
        compiler_params=_cparams(("arbitrary", "arbitrary")),
    )(*args)


def _moe_layer(x, a_pair, b_pair, gate_pair, router, w_in, w_out, n_ctx, final_g):
    bsz, n, d = x.shape
    tm = MOE_TILE
    ntok = bsz * n
    h2, gates, idx = _moe_route(x, a_pair, b_pair, router.T, n_ctx)
    e_flat = jnp.moveaxis(idx, 1, 0).reshape(TOP_K * ntok)
    onehot = (e_flat[:, None] == jnp.arange(N_EXPERTS, dtype=jnp.int32)[None, :]).astype(jnp.int32)
    csum = jnp.cumsum(onehot, axis=0)
    counts = csum[-1]
    rank = jnp.sum(onehot * csum, axis=1) - 1
    padded = ((counts + tm - 1) // tm) * tm
    ends = jnp.cumsum(padded)
    starts = ends - padded
    dest = jnp.sum(onehot * starts[None, :], axis=1) + rank
    n_rows = TOP_K * ntok + N_EXPERTS * tm
    n_rows = ((n_rows + tm - 1) // tm) * tm
    tok_of = jnp.tile(jnp.arange(ntok, dtype=jnp.int32), TOP_K)
    src = jnp.zeros((n_rows,), jnp.int32).at[dest].set(tok_of)
    tile_start = jnp.arange(n_rows // tm, dtype=jnp.int32) * tm
    tile_expert = jnp.minimum(jnp.sum((tile_start[:, None] >= ends[None, :]).astype(jnp.int32), axis=1),
                              N_EXPERTS - 1).astype(jnp.int32)
    n_used = (ends[-1] // tm).astype(jnp.int32).reshape(1)
    rows = lambda a, i: a.at[i].get(mode="promise_in_bounds")
    xs = rows(h2.reshape(ntok, d), src)
    ys = _moe_experts(xs, w_in, w_out, tile_expert, n_used)
    y0 = rows(ys, dest[:ntok]).reshape(bsz, n, d)
    y1 = rows(ys, dest[ntok:]).reshape(bsz, n, d)
    return _moe_combine(x, gate_pair, y0, y1, jnp.swapaxes(gates, 1, 2), n_ctx, final_g)


def _rope_table(n_ctx, seq):
    pos = jnp.arange(seq)
    row = (pos // GRID_W).astype(F32)
    col = (pos % GRID_W).astype(F32)
    nf = DF_HD // 4
    inv = jnp.power(ROPE_BASE, -jnp.arange(nf, dtype=F32) / nf)
    ang = jnp.concatenate([row[:, None] * inv, col[:, None] * inv], axis=-1)
    cos, sin = jnp.cos(ang), jnp.sin(ang)
    cos = jnp.concatenate([jnp.ones((n_ctx, 2 * nf), F32), cos], axis=0)
    sin = jnp.concatenate([jnp.zeros((n_ctx, 2 * nf), F32), sin], axis=0)
    c128 = jnp.tile(jnp.concatenate([cos, cos], axis=1), (1, 2))
    s128 = jnp.tile(jnp.concatenate([-sin, sin], axis=1), (1, 2))
    return jnp.concatenate([c128, s128], axis=1)


def _pair(v, bsz):
    return jnp.stack([v[:bsz], jnp.broadcast_to(v[bsz], (bsz, v.shape[1]))], axis=1)


def kernel(x, c, ctx, c_ctx, ada_w, ada_b, norm1, norm2, ml_w_in, ml_gate_b, ml_hnorm, ml_w_out,
           df_w_in, df_lam, df_hnorm, df_w_out, sw_w_in, sw_sinks, sw_w_out,
           ffn_w_in, ffn_w_out, moe_router, moe_w_in, moe_w_out, final_norm):
    bsz, seq, d = x.shape
    n_ctx = ctx.shape[1]
    depth = ada_w.shape[0]
    assert d == D_MODEL and n_ctx % ATT_Q_TILE == 0 and (n_ctx + seq) % TOKEN_TILE == 0

    xs = jnp.concatenate([ctx, x], axis=1)
    cond_rows = SUBLANES * ((bsz + 1 + SUBLANES - 1) // SUBLANES)
    cond = jnp.zeros((cond_rows, d), F32).at[:bsz].set(c).at[bsz].set(c_ctx)
    mods = _ada_mod(cond, ada_w, ada_b)
    cs = _rope_table(n_ctx, seq)

    for i in range(depth):
        m = [mods[i, :, k * d:(k + 1) * d] for k in range(6)]
        a1 = _pair(norm1[i][None, :] * (1.0 + m[1]), bsz)
        b1 = _pair(m[0], bsz)
        g1 = _pair(m[2], bsz)
        a2 = _pair(norm2[i][None, :] * (1.0 + m[4]), bsz)
        b2 = _pair(m[3], bsz)
        g2 = _pair(m[5], bsz)
        kind, j = i % N_MIXERS, i // N_MIXERS
        if kind == 0:
            w = ml_w_in[j]
            nmain = 2 * ML_QKW + 2 * D_MODEL
            w_main = w[:, :nmain].astype(BF16)
            wgt = w[:, nmain:].T.astype(BF16)
            segs = ((0, ML_QKW, False, 1.0), (ML_QKW, 2 * ML_QKW, False, 1.0),
                    (2 * ML_QKW, 2 * ML_QKW + d, False, 1.0), (2 * ML_QKW + d, nmain, False, 1.0))
            q, k, v, og, grow = _project(xs, a1, b1, w_main, segs, (BF16, BF16, BF16, BF16), n_ctx, wgt=wgt)
            hh = _mlstm_scan(q, k, v, grow, ml_gate_b[j], n_ctx)
            xs = _mixer_out(xs, g1, ml_w_out[j].astype(BF16), "mlstm", n_ctx, (hh, og, ml_hnorm[j]))
        elif kind == 1:
            lambda_init = 0.8 - 0.6 * math.exp(-0.3 * i)
            segs = ((0, DF_QKW, True, DF_HD ** -0.5 * LOG2E), (DF_QKW, 2 * DF_QKW, True, 1.0),
                    (2 * DF_QKW, 2 * DF_QKW + d, False, 1.0))
            q, k, v = _project(xs, a1, b1, df_w_in[j].astype(BF16), segs, (BF16, BF16, BF16), n_ctx, cs=cs)
            att = _diff_attention(q, k, v, df_lam[j], lambda_init, n_ctx)
            xs = _mixer_out(xs, g1, df_w_out[j].astype(BF16), "diff", n_ctx, (att, df_hnorm[j]),
                            post_scale=1.0 - lambda_init)
        else:
            nq = SW_HEADS * SW_HD
            nkv = SW_KV * SW_HD
            segs = ((0, nq, True, SW_HD ** -0.5), (nq, nq + nkv, True, 1.0), (nq + nkv, nq + 2 * nkv, False, 1.0))
            q, k, v = _project(xs, a1, b1, sw_w_in[j].astype(BF16), segs, (BF16, BF16, BF16), n_ctx, cs=cs)
            att = _swa_attention(q, k, v, sw_sinks[j], n_ctx)
            xs = _mixer_out(xs, g1, sw_w_out[j].astype(BF16), "plain", n_ctx, (att,))
        jf = i // 2
        last = i == depth - 1
        if i % 2 == 0:
            ff = ffn_w_out.shape[1]
            wi = ffn_w_in[jf].astype(BF16)
            xs = _dense_ffn(xs, a2, b2, g2, wi[:, :ff], wi[:, ff:], ffn_w_out[jf].astype(BF16), n_ctx)
            if last:
                raise NotImplementedError("final norm is fused into the MoE combine")
        else:
            xs = _moe_layer(xs, a2, b2, g2, moe_router[jf], moe_w_in[jf].astype(BF16),
                            moe_w_out[jf].astype(BF16), n_ctx, final_norm if last else None)
    return xs[:, n_ctx:]
```

```python
import functools
import math

import jax
import jax.numpy as jnp
from jax import lax
from jax.experimental import pallas as pl
from jax.experimental.pallas import tpu as pltpu

F32 = jnp.float32
BF16 = jnp.bfloat16

D_MODEL = 1024
GRID_W = 64
EPS = 1e-6
NEG_INF = -1e30
ROPE_BASE = 10000.0
LOG2E = math.log2(math.e)
N_MIXERS = 3

ML_HEADS = 8
ML_V = D_MODEL // ML_HEADS
ML_QK = ML_V // 2
ML_QKW = ML_HEADS * ML_QK
ML_CHUNK = 128

DF_HD = 64
DF_HEADS = D_MODEL // (2 * DF_HD)
DF_VD = 2 * DF_HD
DF_QKW = 2 * DF_HEADS * DF_HD

SW_HD = 64
SW_HEADS = D_MODEL // SW_HD
SW_KV = 4
SW_GROUP = SW_HEADS // SW_KV
SW_WIN = 128

N_EXPERTS = 8
TOP_K = 2

LANES = 128
SUBLANES = 8
MXU_WIDTH = 256
V7X_VMEM_BYTES = 64 * 1024 * 1024
VMEM_LIMIT = V7X_VMEM_BYTES - 8 * 1024 * 1024

TOKEN_TILE = 640
ATT_Q_TILE = 256
ATT_KV_TILE = 1280
SWA_Q_TILE = 256
MOE_TILE = 512
MOE_FF_CHUNK = 1792


def _cparams(semantics):
    return pltpu.CompilerParams(dimension_semantics=semantics, vmem_limit_bytes=VMEM_LIMIT)


def _const_spec(shape):
    nd = len(shape)
    return pl.BlockSpec(shape, lambda *_: (0,) * nd, pipeline_mode=pl.Buffered(1))


def _sigmoid(x):
    return 1.0 / (1.0 + jnp.exp(-x))


def _log_sigmoid(x):
    return jnp.minimum(x, 0.0) - jnp.log(1.0 + jnp.exp(-jnp.abs(x)))


def _norm_mod(x, a_rows, b_rows, row0, n_ctx):
    xn = x * lax.rsqrt(jnp.mean(x * x, axis=-1, keepdims=True) + EPS)
    is_ctx = (row0 + lax.broadcasted_iota(jnp.int32, (x.shape[0], 1), 0)) < n_ctx
    a = jnp.where(is_ctx, a_rows[1:2], a_rows[0:1])
    b = jnp.where(is_ctx, b_rows[1:2], b_rows[0:1])
    return xn * a + b


def _row_select(rows, row0, n_rows, n_ctx):
    is_ctx = (row0 + lax.broadcasted_iota(jnp.int32, (n_rows, 1), 0)) < n_ctx
    return jnp.where(is_ctx, rows[1:2], rows[0:1])


def _ada_kernel(c_ref, w_ref, b_ref, o_ref):
    c = c_ref[...]
    s = c * _sigmoid(c)
    o_ref[0] = jnp.dot(s, w_ref[0], preferred_element_type=F32,
                       precision=lax.Precision.HIGHEST) + b_ref[0]


def _ada_mod(cond, ada_w, ada_b):
    depth, d, six_d = ada_w.shape
    rows = cond.shape[0]
    col = D_MODEL
    return pl.pallas_call(
        _ada_kernel,
        grid=(depth, six_d // col),
        in_specs=[
            pl.BlockSpec((rows, d), lambda i, j: (0, 0)),
            pl.BlockSpec((1, d, col), lambda i, j: (i, 0, j)),
            pl.BlockSpec((1, 1, col), lambda i, j: (i, 0, j)),
        ],
        out_specs=pl.BlockSpec((1, rows, col), lambda i, j: (i, 0, j)),
        out_shape=jax.ShapeDtypeStruct((depth, rows, six_d), F32),
        compiler_params=_cparams(("arbitrary", "arbitrary")),
    )(cond, ada_w, ada_b.reshape(depth, 1, six_d))


def _rope(r, cos, sin):
    w = r.shape[1]
    lane = lax.broadcasted_iota(jnp.int32, r.shape, 1)
    swapped = jnp.where((lane & 32) == 0, pltpu.roll(r, w - 32, 1), pltpu.roll(r, 32, 1))
    reps = w // LANES
    return r * jnp.tile(cos, (1, reps)) + swapped * jnp.tile(sin, (1, reps))


def _proj_kernel(*refs, segs, n_ctx, tm, has_rope, has_grow):
    it = iter(refs)
    x_ref, a_ref, b_ref, w_ref = next(it), next(it), next(it), next(it)
    cs_ref = next(it) if has_rope else None
    wgt_ref = next(it) if has_grow else None
    outs = [next(it) for _ in segs]
    grow_ref = next(it) if has_grow else None
    h_ref = next(it)

    row0 = pl.program_id(1) * tm
    h_ref[...] = _norm_mod(x_ref[0], a_ref[0], b_ref[0], row0, n_ctx).astype(BF16)
    for (c0, c1, rope, scale), o_ref in zip(segs, outs):
        r = jnp.dot(h_ref[...], w_ref[:, c0:c1], preferred_element_type=F32)
        if rope:
            r = _rope(r, cs_ref[:, :LANES], cs_ref[:, LANES:])
        if scale != 1.0:
            r = r * scale
        o_ref[0] = r.astype(o_ref.dtype)
    if has_grow:
        grow_ref[0] = lax.dot_general(wgt_ref[...], h_ref[...], (((1,), (1,)), ((), ())),
                                      preferred_element_type=F32)


def _project(x, a_pair, b_pair, w, segs, out_dtypes, n_ctx, cs=None, wgt=None):
    bsz, n, d = x.shape
    tm = TOKEN_TILE
    assert n % tm == 0
    tok = lambda b, t: (b, t, 0)
    in_specs = [
        pl.BlockSpec((1, tm, d), tok),
        pl.BlockSpec((1, 2, d), lambda b, t: (b, 0, 0)),
        pl.BlockSpec((1, 2, d), lambda b, t: (b, 0, 0)),
        _const_spec(w.shape),
    ]
    args = [x, a_pair, b_pair, w]
    if cs is not None:
        in_specs.append(pl.BlockSpec((tm, 2 * LANES), lambda b, t: (t, 0)))
        args.append(cs)
    if wgt is not None:
        in_specs.append(_const_spec(wgt.shape))
        args.append(wgt)
    out_specs = [pl.BlockSpec((1, tm, c1 - c0), tok) for (c0, c1, _, _) in segs]
    out_shape = [jax.ShapeDtypeStruct((bsz, n, c1 - c0), dt) for (c0, c1, _, _), dt in zip(segs, out_dtypes)]
    if wgt is not None:
        out_specs.append(pl.BlockSpec((1, wgt.shape[0], tm), lambda b, t: (b, 0, t)))
        out_shape.append(jax.ShapeDtypeStruct((bsz, wgt.shape[0], n), F32))
    kern = functools.partial(_proj_kernel, segs=segs, n_ctx=n_ctx, tm=tm,
                             has_rope=cs is not None, has_grow=wgt is not None)
    return pl.pallas_call(
        kern,
        grid=(bsz, n // tm),
        in_specs=in_specs,
        out_specs=out_specs,
        out_shape=out_shape,
        scratch_shapes=[pltpu.VMEM((tm, d), BF16)],
        compiler_params=_cparams(("arbitrary", "arbitrary")),
    )(*args)


def _split3(x):
    hi = x.astype(BF16)
    r1 = x - hi.astype(F32)
    mid = r1.astype(BF16)
    lo = (r1 - mid.astype(F32)).astype(BF16)
    return hi, mid, lo


_TN = (((0,), (0,)), ((), ()))
_NT = (((1,), (1,)), ((), ()))


def _expand_heads(x, width):
    nh = x.shape[0]
    x3 = jnp.concatenate(_split3(x), axis=0)
    r = lax.broadcasted_iota(jnp.int32, (3 * nh, nh * width), 0)
    c = lax.broadcasted_iota(jnp.int32, (3 * nh, nh * width), 1)
    sel = jnp.where((r & (nh - 1)) == (c >> int(math.log2(width))), 1.0, 0.0).astype(BF16)
    return lax.dot_general(x3, sel, _TN, preferred_element_type=F32)


ML_STAT_ROWS = 6 * ML_HEADS


def _mlstm_gate_kernel(gr_ref, br_ref, st_ref, *, t, chunks):
    nh = ML_HEADS
    d = pl.program_id(1)
    fwd = d == 0
    sgn = 1 - 2 * d
    row = lax.broadcasted_iota(jnp.int32, (t, t), 0)
    col = lax.broadcasted_iota(jnp.int32, (t, t), 1)
    tri_t = jnp.where((row - col) * sgn <= 0, 1.0, 0.0).astype(BF16)
    lane = lax.broadcasted_iota(jnp.int32, (nh, t), 1)
    scan_pos = jnp.where(fwd, lane, t - 1 - lane)
    for c in range(chunks):
        cols = slice(c * t, (c + 1) * t)
        gr = gr_ref[0, :, cols] + br_ref[...]
        gr = jnp.where(fwd, gr[:2 * nh], gr[2 * nh:])
        ig = gr[:nh]
        fg = _log_sigmoid(gr[nh:])
        bs = jnp.dot(jnp.concatenate(_split3(fg), axis=0), tri_t, preferred_element_type=F32)
        b = bs[:nh] + bs[nh:2 * nh] + bs[2 * nh:]
        b_end = jnp.sum(fg, axis=1, keepdims=True)
        src = ig - b
        cmax = src
        shift = 1
        while shift < t:
            moved = jnp.where(fwd, pltpu.roll(cmax, shift, 1), pltpu.roll(cmax, t - shift, 1))
            cmax = jnp.maximum(cmax, jnp.where(scan_pos >= shift, moved, -jnp.inf))
            shift *= 2
        w_end = b_end + src
        m_loc = jnp.max(w_end, axis=1, keepdims=True)
        e_end = jnp.exp(w_end - m_loc)
        st_ref[0, 0, :, cols] = jnp.concatenate(
            [src, cmax, b, e_end, jnp.broadcast_to(b_end, (nh, t)), jnp.broadcast_to(m_loc, (nh, t))], axis=0)


def _mlstm_gate_stats(grow, gate_b):
    bsz, rows, n = grow.shape
    t = ML_CHUNK
    assert t & (t - 1) == 0
    width = TOKEN_TILE * 2 if n % (TOKEN_TILE * 2) == 0 else t
    bias_row = jnp.broadcast_to(gate_b.reshape(rows, 1).astype(F32), (rows, t))
    return pl.pallas_call(
        functools.partial(_mlstm_gate_kernel, t=t, chunks=width // t),
        grid=(bsz, 2, n // width),
        in_specs=[pl.BlockSpec((1, rows, width), lambda b, d, i: (b, 0, i)),
                  pl.BlockSpec((rows, t), lambda b, d, i: (0, 0))],
        out_specs=pl.BlockSpec((1, 1, ML_STAT_ROWS, width), lambda b, d, i: (b, d, 0, i)),
        out_shape=jax.ShapeDtypeStruct((bsz, 2, ML_STAT_ROWS, n), F32),
        compiler_params=_cparams(("arbitrary", "arbitrary", "arbitrary")),
    )(grow, bias_row)


def _mlstm_kernel(q_ref, k_ref, v_ref, st_ref, o_ref, ct_ref, m_ref, *, t):
    nh = ML_HEADS
    d = pl.program_id(1)
    step = pl.program_id(2)

    @pl.when(step == 0)
    def _():
        ct_ref[...] = jnp.zeros_like(ct_ref)
        m_ref[...] = jnp.zeros_like(m_ref)

    sgn = 1 - 2 * d
    row = lax.broadcasted_iota(jnp.int32, (t, t), 0)
    col = lax.broadcasted_iota(jnp.int32, (t, t), 1)
    mask = (col - row) * sgn <= 0
    st = st_ref[0, 0]
    src, cmax, b, e_end = (st[j * nh:(j + 1) * nh] for j in range(4))
    b_end = st[4 * nh:5 * nh, 0:1]
    m_loc = st[5 * nh:6 * nh, 0:1]

    m_in = m_ref[:, 0:1]
    big_m = jnp.maximum(m_in, cmax)
    m_new = jnp.maximum(b_end + m_in, m_loc)
    a_dec = jnp.exp(b_end + m_in - m_new)
    s_dec = jnp.exp(m_loc - m_new)

    q = q_ref[0]
    k = k_ref[0] * (ML_QK ** -0.5)
    v = v_ref[0]
    q_in = (q.astype(F32) * _expand_heads(jnp.exp(m_in - big_m), ML_QK)).astype(BF16)
    k_end = (k.astype(F32) * _expand_heads(e_end, ML_QK)).astype(BF16)
    floor = _expand_heads(jnp.exp(-(b + big_m)), ML_V)

    ones3 = jnp.ones((3 * nh, t), BF16)
    dec_l = jnp.concatenate(list(_split3(-big_m)) + [ones3], axis=0)
    dec_r = jnp.concatenate([ones3] + list(_split3(src)), axis=0)
    head_of_row = lax.broadcasted_iota(jnp.int32, (6 * nh, t), 0) & (nh - 1)
    ones = jnp.ones((t, ML_V), BF16)
    nums, dens = [], []
    for h in range(nh):
        qk = slice(h * ML_QK, (h + 1) * ML_QK)
        va = jnp.concatenate([v[:, h * ML_V:(h + 1) * ML_V], ones], axis=1)
        s = lax.dot_general(q[:, qk], k[:, qk], _NT, preferred_element_type=F32)
        w_log = lax.dot_general(dec_l, jnp.where(head_of_row == h, dec_r, jnp.zeros_like(dec_r)), _TN,
                                preferred_element_type=F32)
        w = (jnp.where(mask, jnp.exp(w_log), 0.0) * s).astype(BF16)
        ct = ct_ref[h]
        r = (jnp.dot(w, va, preferred_element_type=F32)
             + jnp.dot(q_in[:, qk], ct.astype(BF16), preferred_element_type=F32))
        nums.append(r[:, :ML_V])
        dens.append(r[:, ML_V:])
        c_loc = lax.dot_general(k_end[:, qk], va, _TN, preferred_element_type=F32)
        ct_ref[h] = a_dec[h:h + 1] * ct + s_dec[h:h + 1] * c_loc
    num = jnp.concatenate(nums, axis=1)
    den = jnp.concatenate(dens, axis=1)
    o_ref[0, 0] = num / jnp.maximum(jnp.abs(den), floor)
    m_ref[...] = jnp.broadcast_to(m_new, m_ref.shape)


def _mlstm_scan(q, k, v, grow, gate_b, n_ctx):
    bsz, n, _ = q.shape
    t = ML_CHUNK
    assert t & (t - 1) == 0
    nc = n // t
    ncc = n_ctx // t

    def chunk(d, i):
        bwd = jnp.where(i < ncc, ncc - 1 - i, nc - 1 - (i - ncc))
        return jnp.where(d == 0, i, bwd)

    stats = _mlstm_gate_stats(grow, gate_b)
    tok = lambda b, d, i: (b, chunk(d, i), 0)
    return pl.pallas_call(
        functools.partial(_mlstm_kernel, t=t),
        grid=(bsz, 2, nc),
        in_specs=[
            pl.BlockSpec((1, t, ML_QKW), tok),
            pl.BlockSpec((1, t, ML_QKW), tok),
            pl.BlockSpec((1, t, D_MODEL), tok),
            pl.BlockSpec((1, 1, ML_STAT_ROWS, t), lambda b, d, i: (b, d, 0, chunk(d, i))),
        ],
        out_specs=pl.BlockSpec((1, 1, t, D_MODEL), lambda b, d, i: (d, b, chunk(d, i), 0)),
        out_shape=jax.ShapeDtypeStruct((2, bsz, n, D_MODEL), F32),
        scratch_shapes=[pltpu.VMEM((ML_HEADS, ML_QK, 2 * ML_V), F32), pltpu.VMEM((ML_HEADS, LANES), F32)],
        compiler_params=_cparams(("arbitrary", "arbitrary", "arbitrary")),
    )(q, k, v, stats)


def _diff_kernel(lam_ref, q_ref, kt_ref, v_ref, o_ref, acc_ref, m_ref, s_ref, va_ref, *, tq, tk, n, n_ctx,
                 lambda_init):
    qi = pl.program_id(2)

    @pl.when(qi == 0)
    def _():
        def fill(c, carry):
            rows = pl.ds(pl.multiple_of(c * tk, tk), tk)
            va_ref[rows, :DF_VD] = v_ref[0, rows, :]
            va_ref[rows, DF_VD:] = jnp.ones((tk, DF_VD), va_ref.dtype)
            return carry
        lax.fori_loop(0, n // tk, fill, 0)

    q = q_ref[0]
    lane = lax.broadcasted_iota(jnp.int32, q.shape, 1)
    zero = jnp.zeros_like(q)
    qs = (jnp.where(lane < DF_HD, q, zero), jnp.where(lane < DF_HD, zero, q))
    acc_ref[...] = jnp.zeros_like(acc_ref)
    m_ref[...] = jnp.full_like(m_ref, NEG_INF)

    def scores(start, size, slot):
        kt = kt_ref[0, :, pl.ds(start, size)]
        for j in range(2):
            s_ref[slot, j, :, :size] = jnp.dot(qs[j], kt, preferred_element_type=F32)

    def accumulate(start, size, slot):
        va = va_ref[pl.ds(start, size), :]
        ss = [s_ref[slot, j, :, :size] for j in range(2)]
        m_old = [m_ref[j] for j in range(2)]
        m_new = [jnp.maximum(m_old[j], jnp.max(ss[j], axis=1, keepdims=True)) for j in range(2)]
        ps = [jnp.exp2((ss[j] - m_new[j]).astype(BF16)) for j in range(2)]
        pv = [jnp.dot(ps[j], va, preferred_element_type=F32) for j in range(2)]
        for j in range(2):
            acc_ref[j] = jnp.exp2(m_old[j] - m_new[j]) * acc_ref[j] + pv[j]
            m_ref[j] = m_new[j]

    is_ctx_tile = qi * tq < n_ctx

    @pl.when(is_ctx_tile)
    def _():
        scores(0, n_ctx, 0)
        accumulate(0, n_ctx, 0)

    @pl.when(jnp.logical_not(is_ctx_tile))
    def _():
        n_chunks = n // tk
        scores(0, tk, 0)

        def body(i, carry):
            c0 = pl.multiple_of(2 * i * tk, tk)
            scores(c0 + tk, tk, 1)
            accumulate(c0, tk, 0)
            scores(c0 + 2 * tk, tk, 0)
            accumulate(c0 + tk, tk, 1)
            return carry
        lax.fori_loop(0, (n_chunks - 1) // 2, body, 0)
        accumulate((n_chunks - 1) * tk, tk, 0)

    lam = lam_ref[...]
    lam_full = (jnp.exp(jnp.sum(lam[0:1] * lam[1:2], axis=1, keepdims=True))
                - jnp.exp(jnp.sum(lam[2:3] * lam[3:4], axis=1, keepdims=True)) + lambda_init)
    o1 = acc_ref[0, :, :DF_VD] / acc_ref[0, :, DF_VD:]
    o2 = acc_ref[1, :, :DF_VD] / acc_ref[1, :, DF_VD:]
    o_ref[0] = o1 - lam_full * o2


def _diff_attention(q, k, v, lam, lambda_init, n_ctx):
    bsz, n, _ = q.shape
    tq, tk = ATT_Q_TILE, ATT_KV_TILE
    assert n_ctx % tq == 0 and n % tq == 0 and n % tk == 0 and (n // tk) % 2 == 1 and n_ctx <= tk
    kt = jnp.swapaxes(k, 1, 2)
    kern = functools.partial(_diff_kernel, tq=tq, tk=tk, n=n, n_ctx=n_ctx, lambda_init=lambda_init)
    return pl.pallas_call(
        kern,
        grid=(bsz, DF_HEADS, n // tq),
        in_specs=[
            pl.BlockSpec(lam.shape, lambda b, h, i: (0, 0)),
            pl.BlockSpec((1, tq, 2 * DF_HD), lambda b, h, i: (b, i, h)),
            pl.BlockSpec((1, 2 * DF_HD, n), lambda b, h, i: (b, h, 0)),
            pl.BlockSpec((1, n, DF_VD), lambda b, h, i: (b, 0, h)),
        ],
        out_specs=pl.BlockSpec((1, tq, DF_VD), lambda b, h, i: (b, i, h)),
        out_shape=jax.ShapeDtypeStruct((bsz, n, D_MODEL), F32),
        scratch_shapes=[pltpu.VMEM((2, tq, 2 * DF_VD), F32), pltpu.VMEM((2, tq, 1), F32),
                        pltpu.VMEM((2, 2, tq, tk), F32), pltpu.VMEM((n, 2 * DF_VD), v.dtype)],
        compiler_params=_cparams(("arbitrary", "arbitrary", "arbitrary")),
    )(lam.astype(F32), q, kt, v)


def _swa_kernel(sink_ref, q_ref, kt_ref, va_ref, o_ref, *, tq, n, n_ctx):
    g = pl.program_id(1)
    i = pl.program_id(2)
    bw = tq + 2 * SW_WIN
    q = q_ref[0]
    qh = jnp.concatenate([q[:, u * SW_HD:(u + 1) * SW_HD] for u in range(SW_GROUP)], axis=0)
    start = pl.multiple_of(jnp.clip(i * tq - SW_WIN, 0, n - bw), SW_WIN)
    s_c = jnp.dot(qh, kt_ref[0, :, 0:n_ctx], preferred_element_type=F32)
    s_b = jnp.dot(qh, kt_ref[0, :, pl.ds(start, bw)], preferred_element_type=F32)
    rel = ((lax.broadcasted_iota(jnp.int32, s_b.shape, 0) & (tq - 1))
           - lax.broadcasted_iota(jnp.int32, s_b.shape, 1))
    first_ok = jnp.where(i * tq >= n_ctx, n_ctx - start, bw)
    key_col = lax.broadcasted_iota(jnp.int32, (1, bw), 1)
    shift = jnp.where(key_col >= first_ok, i * tq - start, -(1 << 20))
    s_b = jnp.where(jnp.abs(rel + shift) <= SW_WIN, s_b, NEG_INF)
    head = lax.broadcasted_iota(jnp.int32, (SW_GROUP * tq, 1), 0) // tq
    sink = jnp.zeros((SW_GROUP * tq, 1), F32)
    for u in range(SW_GROUP):
        sink = jnp.where(head == u, sink_ref[g * SW_GROUP + u], sink)
    m = jnp.maximum(sink, jnp.maximum(jnp.max(s_c, axis=1, keepdims=True), jnp.max(s_b, axis=1, keepdims=True)))
    acc = (jnp.dot(jnp.exp(s_c - m).astype(BF16), va_ref[0, 0:n_ctx, :], preferred_element_type=F32)
           + jnp.dot(jnp.exp(s_b - m).astype(BF16), va_ref[0, pl.ds(start, bw), :], preferred_element_type=F32))
    out = acc[:, :SW_HD] / (acc[:, SW_HD:] + jnp.exp(sink - m))
    o_ref[0] = jnp.concatenate([out[u * tq:(u + 1) * tq] for u in range(SW_GROUP)], axis=1).astype(o_ref.dtype)


def _swa_attention(q, k, v, sinks, n_ctx):
    bsz, n, _ = q.shape
    tq = SWA_Q_TILE
    assert tq & (tq - 1) == 0 and n_ctx % tq == 0 and n % tq == 0 and tq % SW_WIN == 0
    gw = SW_GROUP * SW_HD
    kt = jnp.swapaxes(k, 1, 2)
    va = jnp.concatenate([v.reshape(bsz, n, SW_KV, SW_HD),
                          jnp.ones((bsz, n, SW_KV, SW_HD), v.dtype)], axis=-1).reshape(bsz, n, SW_KV * 2 * SW_HD)
    return pl.pallas_call(
        functools.partial(_swa_kernel, tq=tq, n=n, n_ctx=n_ctx),
        grid=(bsz, SW_KV, n // tq),
        in_specs=[
            pl.BlockSpec(memory_space=pltpu.SMEM),
            pl.BlockSpec((1, tq, gw), lambda b, g, i: (b, i, g)),
            pl.BlockSpec((1, SW_HD, n), lambda b, g, i: (b, g, 0)),
            pl.BlockSpec((1, n, 2 * SW_HD), lambda b, g, i: (b, 0, g)),
        ],
        out_specs=pl.BlockSpec((1, tq, gw), lambda b, g, i: (b, i, g)),
        out_shape=jax.ShapeDtypeStruct((bsz, n, D_MODEL), BF16),
        compiler_params=_cparams(("arbitrary", "arbitrary", "arbitrary")),
    )(sinks.astype(F32), q, kt, va)


def _head_norm(y, g, width):
    parts = []
    for h in range(y.shape[1] // width):
        yh = y[:, h * width:(h + 1) * width]
        parts.append(yh * lax.rsqrt(jnp.mean(yh * yh, axis=-1, keepdims=True) + EPS))
    return jnp.concatenate(parts, axis=1) * g


def _mixer_out_kernel(*refs, mode, n_ctx, tm, post_scale):
    it = iter(refs)
    x_ref, gate_ref, w_ref = next(it), next(it), next(it)
    if mode == "mlstm":
        hh_ref, og_ref, g_ref = next(it), next(it), next(it)
        y = _head_norm(hh_ref[0, 0] + hh_ref[1, 0], g_ref[...], ML_V)
        z = y * _sigmoid(og_ref[0].astype(F32))
    elif mode == "diff":
        a_ref, g_ref = next(it), next(it)
        z = _head_norm(a_ref[0], g_ref[...], DF_VD) * post_scale
    else:
        a_ref = next(it)
        z = a_ref[0]
    o_ref = next(it)
    y = jnp.dot(z.astype(BF16), w_ref[...], preferred_element_type=F32)
    gate = _row_select(gate_ref[0], pl.program_id(1) * tm, tm, n_ctx)
    o_ref[0] = x_ref[0] + gate * y


def _mixer_out(x, gate_pair, w_out, mode, n_ctx, extra, post_scale=1.0):
    bsz, n, d = x.shape
    tm = TOKEN_TILE
    tok = lambda b, t: (b, t, 0)
    in_specs = [pl.BlockSpec((1, tm, d), tok), pl.BlockSpec((1, 2, d), lambda b, t: (b, 0, 0)),
                _const_spec(w_out.shape)]
    args = [x, gate_pair, w_out]
    if mode == "mlstm":
        hh, og, g = extra
        in_specs += [pl.BlockSpec((2, 1, tm, d), lambda b, t: (0, b, t, 0)), pl.BlockSpec((1, tm, d), tok),
                     pl.BlockSpec((1, d), lambda b, t: (0, 0))]
        args += [hh, og, g.reshape(1, d)]
    elif mode == "diff":
        a, g = extra
        in_specs += [pl.BlockSpec((1, tm, d), tok), pl.BlockSpec((1, d), lambda b, t: (0, 0))]
        args += [a, g.reshape(1, d)]
    else:
        (a,) = extra
        in_specs += [pl.BlockSpec((1, tm, d), tok)]
        args += [a]
    kern = functools.partial(_mixer_out_kernel, mode=mode, n_ctx=n_ctx, tm=tm, post_scale=post_scale)
    return pl.pallas_call(
        kern,
        grid=(bsz, n // tm),
        in_specs=in_specs,
        out_specs=pl.BlockSpec((1, tm, d), tok),
        out_shape=jax.ShapeDtypeStruct((bsz, n, d), F32),
        compiler_params=_cparams(("arbitrary", "arbitrary")),
    )(*args)


def _dense_ffn_kernel(x_ref, a_ref, b_ref, gate_ref, wa_ref, wb_ref, wo_ref, o_ref, h_ref, g_ref,
                      *, n_ctx, tm, bounds):
    row0 = pl.program_id(1) * tm
    x = x_ref[0]
    h_ref[...] = _norm_mod(x, a_ref[0], b_ref[0], row0, n_ctx).astype(BF16)
    for c0, c1 in zip(bounds[:-1], bounds[1:]):
        a = jnp.dot(h_ref[...], wa_ref[:, c0:c1], preferred_element_type=F32)
        b = jnp.dot(h_ref[...], wb_ref[:, c0:c1], preferred_element_type=F32)
        g_ref[:, c0:c1] = (a * _sigmoid(a) * b).astype(BF16)
    y = jnp.dot(g_ref[...], wo_ref[...], preferred_element_type=F32)
    o_ref[0] = x + _row_select(gate_ref[0], row0, tm, n_ctx) * y


def _dense_ffn(x, a_pair, b_pair, gate_pair, wa, wb, wo, n_ctx):
    bsz, n, d = x.shape
    tm = TOKEN_TILE
    ff = wa.shape[1]
    half = (pl.cdiv(ff // MXU_WIDTH, 2) * MXU_WIDTH) if ff % MXU_WIDTH == 0 else ff
    bounds = (0, half, ff) if half < ff else (0, ff)
    tok = lambda b, t: (b, t, 0)
    pair = pl.BlockSpec((1, 2, d), lambda b, t: (b, 0, 0))
    kern = functools.partial(_dense_ffn_kernel, n_ctx=n_ctx, tm=tm, bounds=bounds)
    return pl.pallas_call(
        kern,
        grid=(bsz, n // tm),
        in_specs=[pl.BlockSpec((1, tm, d), tok), pair, pair, pair,
                  _const_spec(wa.shape), _const_spec(wb.shape), _const_spec(wo.shape)],
        out_specs=pl.BlockSpec((1, tm, d), tok),
        out_shape=jax.ShapeDtypeStruct((bsz, n, d), F32),
        scratch_shapes=[pltpu.VMEM((tm, d), BF16), pltpu.VMEM((tm, ff), BF16)],
        compiler_params=_cparams(("arbitrary", "arbitrary")),
    )(x, a_pair, b_pair, gate_pair, wa, wb, wo)


def _moe_route_kernel(x_ref, a_ref, b_ref, r_ref, h_ref, g_ref, i_ref, *, n_ctx, tm):
    row0 = pl.program_id(1) * tm
    h = _norm_mod(x_ref[0], a_ref[0], b_ref[0], row0, n_ctx)
    h_ref[0] = h.astype(BF16)
    logits = lax.dot_general(r_ref[...], h, (((1,), (1,)), ((), ())), preferred_element_type=F32,
                             precision=lax.Precision.HIGHEST)
    e_idx = lax.broadcasted_iota(jnp.int32, logits.shape, 0)
    v1 = jnp.max(logits, axis=0, keepdims=True)
    i1 = jnp.min(jnp.where(logits == v1, e_idx, N_EXPERTS), axis=0, keepdims=True)
    rest = jnp.where(e_idx == i1, -jnp.inf, logits)
    v2 = jnp.max(rest, axis=0, keepdims=True)
    i2 = jnp.min(jnp.where(rest == v2, e_idx, N_EXPERTS), axis=0, keepdims=True)
    p2 = jnp.exp(v2 - v1)
    den = 1.0 + p2
    g_ref[0] = jnp.concatenate([1.0 / den, p2 / den], axis=0)
    i_ref[0] = jnp.concatenate([i1, i2], axis=0)


def _moe_route(x, a_pair, b_pair, router_t, n_ctx):
    bsz, n, d = x.shape
    tm = TOKEN_TILE
    tok = lambda b, t: (b, t, 0)
    pair = pl.BlockSpec((1, 2, d), lambda b, t: (b, 0, 0))
    return pl.pallas_call(
        functools.partial(_moe_route_kernel, n_ctx=n_ctx, tm=tm),
        grid=(bsz, n // tm),
        in_specs=[pl.BlockSpec((1, tm, d), tok), pair, pair, _const_spec(router_t.shape)],
        out_specs=[pl.BlockSpec((1, tm, d), tok),
                   pl.BlockSpec((1, TOP_K, tm), lambda b, t: (b, 0, t)),
                   pl.BlockSpec((1, TOP_K, tm), lambda b, t: (b, 0, t))],
        out_shape=[jax.ShapeDtypeStruct((bsz, n, d), BF16),
                   jax.ShapeDtypeStruct((bsz, TOP_K, n), F32),
                   jax.ShapeDtypeStruct((bsz, TOP_K, n), jnp.int32)],
        compiler_params=_cparams(("arbitrary", "arbitrary")),
    )(x, a_pair, b_pair, router_t)


def _moe_expert_kernel(te_ref, nu_ref, x_ref, wi_ref, wo_ref, o_ref, g_ref, *, ff, chunk):
    i = pl.program_id(0)

    @pl.when(i < nu_ref[0])
    def _():
        x = x_ref[...]
        for c0 in range(0, ff, chunk):
            a = jnp.dot(x, wi_ref[0, :, c0:c0 + chunk], preferred_element_type=F32)
            b = jnp.dot(x, wi_ref[0, :, ff + c0:ff + c0 + chunk], preferred_element_type=F32)
            g_ref[:, c0:c0 + chunk] = (a * _sigmoid(a) * b).astype(BF16)
        o_ref[...] = jnp.dot(g_ref[...], wo_ref[0], preferred_element_type=F32).astype(o_ref.dtype)

    @pl.when(i >= nu_ref[0])
    def _():
        o_ref[...] = jnp.zeros_like(o_ref)


def _moe_experts(xs, w_in, w_out, tile_expert, n_used):
    p, d = xs.shape
    tm = MOE_TILE
    ff = w_out.shape[1]
    chunk = MOE_FF_CHUNK
    assert ff % chunk == 0 and chunk % LANES == 0
    nt = p // tm

    def teff(i, nu):
        return jnp.minimum(i, jnp.maximum(nu[0] - 1, 0))

    grid_spec = pltpu.PrefetchScalarGridSpec(
        num_scalar_prefetch=2,
        grid=(nt,),
        in_specs=[
            pl.BlockSpec((tm, d), lambda i, te, nu: (teff(i, nu), 0)),
            pl.BlockSpec((1, d, 2 * ff), lambda i, te, nu: (te[teff(i, nu)], 0, 0),
                         pipeline_mode=pl.Buffered(1)),
            pl.BlockSpec((1, ff, d), lambda i, te, nu: (te[teff(i, nu)], 0, 0),
                         pipeline_mode=pl.Buffered(1)),
        ],
        out_specs=pl.BlockSpec((tm, d), lambda i, te, nu: (i, 0)),
        scratch_shapes=[pltpu.VMEM((tm, ff), BF16)],
    )
    return pl.pallas_call(
        functools.partial(_moe_expert_kernel, ff=ff, chunk=chunk),
        grid_spec=grid_spec,
        out_shape=jax.ShapeDtypeStruct((p, d), BF16),
        compiler_params=_cparams(("arbitrary",)),
    )(tile_expert, n_used, xs, w_in, w_out)


def _moe_combine_kernel(*refs, n_ctx, tm, final):
    it = iter(refs)
    x_ref, gate_ref, y0_ref, y1_ref, pk_ref = next(it), next(it), next(it), next(it), next(it)
    fg_ref = next(it) if final else None
    o_ref = next(it)
    pk = pk_ref[0]
    y = pk[:, 0:1] * y0_ref[0].astype(F32) + pk[:, 1:2] * y1_ref[0].astype(F32)
    x = x_ref[0] + _row_select(gate_ref[0], pl.program_id(1) * tm, tm, n_ctx) * y
    if final:
        x = x * lax.rsqrt(jnp.mean(x * x, axis=-1, keepdims=True) + EPS) * fg_ref[...]
    o_ref[0] = x


def _moe_combine(x, gate_pair, y0, y1, slot_w, n_ctx, final_g=None):
    bsz, n, d = x.shape
    tm = TOKEN_TILE
    tok = lambda b, t: (b, t, 0)
    in_specs = [pl.BlockSpec((1, tm, d), tok), pl.BlockSpec((1, 2, d), lambda b, t: (b, 0, 0)),
                pl.BlockSpec((1, tm, d), tok), pl.BlockSpec((1, tm, d), tok),
                pl.BlockSpec((1, tm, TOP_K), tok)]
    args = [x, gate_pair, y0, y1, slot_w]
    if final_g is not None:
        in_specs.append(pl.BlockSpec((1, d), lambda b, t: (0, 0)))
        args.append(final_g.reshape(1, d))
    return pl.pallas_call(
        functools.partial(_moe_combine_kernel, n_ctx=n_ctx, tm=tm, final=final_g is not None),
        grid=(bsz, n // tm),
        in_specs=in_specs,
        out_specs=pl.BlockSpec((1, tm, d), tok),
        out_shape=jax.ShapeDtypeStruct((bsz, n, d), F32),
        compiler_params=_cparams(("arbitrary", "arbitrary")),
    )(*args)


def _moe_layer(x, a_pair, b_pair, gate_pair, router, w_in_all, w_out_all, layer, n_ctx, final_g):
    bsz, n, d = x.shape
    tm = MOE_TILE
    ntok = bsz * n
    h2, gates, idx = _moe_route(x, a_pair, b_pair, router.T, n_ctx)
    w_in_all, w_out_all, h2 = lax.optimization_barrier((w_in_all, w_out_all, h2))
    w_in = w_in_all[layer].astype(BF16)
    w_out = w_out_all[layer].astype(BF16)
    e_flat = jnp.moveaxis(idx, 1, 0).reshape(TOP_K * ntok)
    onehot = (e_flat[:, None] == jnp.arange(N_EXPERTS, dtype=jnp.int32)[None, :]).astype(jnp.int32)
    csum = jnp.cumsum(onehot, axis=0)
    counts = csum[-1]
    rank = jnp.sum(onehot * csum, axis=1) - 1
    padded = ((counts + tm - 1) // tm) * tm
    ends = jnp.cumsum(padded)
    starts = ends - padded
    dest = jnp.sum(onehot * starts[None, :], axis=1) + rank
    n_rows = TOP_K * ntok + N_EXPERTS * tm
    n_rows = ((n_rows + tm - 1) // tm) * tm
    tok_of = jnp.tile(jnp.arange(ntok, dtype=jnp.int32), TOP_K)
    src = (jnp.arange(n_rows, dtype=jnp.int32) % ntok).at[dest].set(tok_of)
    tile_start = jnp.arange(n_rows // tm, dtype=jnp.int32) * tm
    tile_expert = jnp.minimum(jnp.sum((tile_start[:, None] >= ends[None, :]).astype(jnp.int32), axis=1),
                              N_EXPERTS - 1).astype(jnp.int32)
    n_used = (ends[-1] // tm).astype(jnp.int32).reshape(1)
    rows = lambda a, i: a.at[i].get(mode="promise_in_bounds")
    xs = rows(h2.reshape(ntok, d), src)
    ys = _moe_experts(xs, w_in, w_out, tile_expert, n_used)
    y0 = rows(ys, dest[:ntok]).reshape(bsz, n, d)
    y1 = rows(ys, dest[ntok:]).reshape(bsz, n, d)
    return _moe_combine(x, gate_pair, y0, y1, jnp.swapaxes(gates, 1, 2), n_ctx, final_g)


def _rope_table(n_ctx, seq):
    pos = jnp.arange(seq)
    row = (pos // GRID_W).astype(F32)
    col = (pos % GRID_W).astype(F32)
    nf = DF_HD // 4
    inv = jnp.power(ROPE_BASE, -jnp.arange(nf, dtype=F32) / nf)
    ang = jnp.concatenate([row[:, None] * inv, col[:, None] * inv], axis=-1)
    cos, sin = jnp.cos(ang), jnp.sin(ang)
    cos = jnp.concatenate([jnp.ones((n_ctx, 2 * nf), F32), cos], axis=0)
    sin = jnp.concatenate([jnp.zeros((n_ctx, 2 * nf), F32), sin], axis=0)
    c128 = jnp.tile(jnp.concatenate([cos, cos], axis=1), (1, 2))
    s128 = jnp.tile(jnp.concatenate([-sin, sin], axis=1), (1, 2))
    return jnp.concatenate([c128, s128], axis=1)


def _pair(v, bsz):
    return jnp.stack([v[:bsz], jnp.broadcast_to(v[bsz], (bsz, v.shape[1]))], axis=1)


def kernel(x, c, ctx, c_ctx, ada_w, ada_b, norm1, norm2, ml_w_in, ml_gate_b, ml_hnorm, ml_w_out,
           df_w_in, df_lam, df_hnorm, df_w_out, sw_w_in, sw_sinks, sw_w_out,
           ffn_w_in, ffn_w_out, moe_router, moe_w_in, moe_w_out, final_norm):
    bsz, seq, d = x.shape
    n_ctx = ctx.shape[1]
    depth = ada_w.shape[0]
    assert d == D_MODEL and n_ctx % ATT_Q_TILE == 0 and (n_ctx + seq) % TOKEN_TILE == 0

    xs = jnp.concatenate([ctx, x], axis=1)
    cond_rows = SUBLANES * ((bsz + 1 + SUBLANES - 1) // SUBLANES)
    cond = jnp.zeros((cond_rows, d), F32).at[:bsz].set(c).at[bsz].set(c_ctx)
    mods = _ada_mod(cond, ada_w, ada_b)
    cs = _rope_table(n_ctx, seq)

    for i in range(depth):
        m = [mods[i, :, k * d:(k + 1) * d] for k in range(6)]
        a1 = _pair(norm1[i][None, :] * (1.0 + m[1]), bsz)
        b1 = _pair(m[0], bsz)
        g1 = _pair(m[2], bsz)
        a2 = _pair(norm2[i][None, :] * (1.0 + m[4]), bsz)
        b2 = _pair(m[3], bsz)
        g2 = _pair(m[5], bsz)
        kind, j = i % N_MIXERS, i // N_MIXERS
        if kind == 0:
            w = ml_w_in[j]
            nmain = 2 * ML_QKW + 2 * D_MODEL
            w_main = w[:, :nmain].astype(BF16)
            wgt = w[:, nmain:].T.astype(BF16)
            segs = ((0, ML_QKW, False, 1.0), (ML_QKW, 2 * ML_QKW, False, 1.0),
                    (2 * ML_QKW, 2 * ML_QKW + d, False, 1.0), (2 * ML_QKW + d, nmain, False, 1.0))
            q, k, v, og, grow = _project(xs, a1, b1, w_main, segs, (BF16, BF16, BF16, BF16), n_ctx, wgt=wgt)
            hh = _mlstm_scan(q, k, v, grow, ml_gate_b[j], n_ctx)
            xs = _mixer_out(xs, g1, ml_w_out[j].astype(BF16), "mlstm", n_ctx, (hh, og, ml_hnorm[j]))
        elif kind == 1:
            lambda_init = 0.8 - 0.6 * math.exp(-0.3 * i)
            segs = ((0, DF_QKW, True, DF_HD ** -0.5 * LOG2E), (DF_QKW, 2 * DF_QKW, True, 1.0),
                    (2 * DF_QKW, 2 * DF_QKW + d, False, 1.0))
            q, k, v = _project(xs, a1, b1, df_w_in[j].astype(BF16), segs, (BF16, BF16, BF16), n_ctx, cs=cs)
            att = _diff_attention(q, k, v, df_lam[j], lambda_init, n_ctx)
            xs = _mixer_out(xs, g1, df_w_out[j].astype(BF16), "diff", n_ctx, (att, df_hnorm[j]),
                            post_scale=1.0 - lambda_init)
        else:
            nq = SW_HEADS * SW_HD
            nkv = SW_KV * SW_HD
            segs = ((0, nq, True, SW_HD ** -0.5), (nq, nq + nkv, True, 1.0), (nq + nkv, nq + 2 * nkv, False, 1.0))
            q, k, v = _project(xs, a1, b1, sw_w_in[j].astype(BF16), segs, (BF16, BF16, BF16), n_ctx, cs=cs)
            att = _swa_attention(q, k, v, sw_sinks[j], n_ctx)
            xs = _mixer_out(xs, g1, sw_w_out[j].astype(BF16), "plain", n_ctx, (att,))
        jf = i // 2
        last = i == depth - 1
        if i % 2 == 0:
            ff = ffn_w_out.shape[1]
            wi = ffn_w_in[jf].astype(BF16)
            xs = _dense_ffn(xs, a2, b2, g2, wi[:, :ff], wi[:, ff:], ffn_w_out[jf].astype(BF16), n_ctx)
            if last:
                raise NotImplementedError("final norm is fused into the MoE combine")
        else:
            xs = _moe_layer(xs, a2, b2, g2, moe_router[jf], moe_w_in, moe_w_out, jf, n_ctx,
                            final_norm if last else None)
    return xs[:, n_ctx:]
```

```python
import functools
import math

import jax
import jax.numpy as jnp
from jax import lax
from jax.experimental import pallas as pl
from jax.experimental.pallas import tpu as pltpu

F32 = jnp.float32
BF16 = jnp.bfloat16

D_MODEL = 1024
GRID_W = 64
EPS = 1e-6
NEG_INF = -1e30
ROPE_BASE = 10000.0
LOG2E = math.log2(math.e)
N_MIXERS = 3

ML_HEADS = 8
ML_V = D_MODEL // ML_HEADS
ML_QK = ML_V // 2
ML_QKW = ML_HEADS * ML_QK
ML_CHUNK = 128

DF_HD = 64
DF_HEADS = D_MODEL // (2 * DF_HD)
DF_VD = 2 * DF_HD
DF_QKW = 2 * DF_HEADS * DF_HD

SW_HD = 64
SW_HEADS = D_MODEL // SW_HD
SW_KV = 4
SW_GROUP = SW_HEADS // SW_KV
SW_WIN = 128

N_EXPERTS = 8
TOP_K = 2

LANES = 128
SUBLANES = 8
MXU_WIDTH = 256
V7X_VMEM_BYTES = 64 * 1024 * 1024
VMEM_LIMIT = V7X_VMEM_BYTES - 8 * 1024 * 1024

TOKEN_TILE = 640
ATT_Q_TILE = 256
ATT_KV_TILE = 1280
ATT_UNROLL = 12
SWA_Q_TILE = 256
MOE_TILE = 512
MOE_FF_CHUNK = 1792


def _cparams(semantics):
    return pltpu.CompilerParams(dimension_semantics=semantics, vmem_limit_bytes=VMEM_LIMIT)


def _const_spec(shape):
    nd = len(shape)
    return pl.BlockSpec(shape, lambda *_: (0,) * nd, pipeline_mode=pl.Buffered(1))


def _sigmoid(x):
    return 1.0 / (1.0 + jnp.exp(-x))


def _log_sigmoid(x):
    return jnp.minimum(x, 0.0) - jnp.log(1.0 + jnp.exp(-jnp.abs(x)))


def _norm_mod(x, a_rows, b_rows, row0, n_ctx):
    xn = x * lax.rsqrt(jnp.mean(x * x, axis=-1, keepdims=True) + EPS)
    is_ctx = (row0 + lax.broadcasted_iota(jnp.int32, (x.shape[0], 1), 0)) < n_ctx
    a = jnp.where(is_ctx, a_rows[1:2], a_rows[0:1])
    b = jnp.where(is_ctx, b_rows[1:2], b_rows[0:1])
    return xn * a + b


def _row_select(rows, row0, n_rows, n_ctx):
    is_ctx = (row0 + lax.broadcasted_iota(jnp.int32, (n_rows, 1), 0)) < n_ctx
    return jnp.where(is_ctx, rows[1:2], rows[0:1])


def _ada_kernel(c_ref, w_ref, b_ref, o_ref):
    c = c_ref[...]
    s = c * _sigmoid(c)
    o_ref[0] = jnp.dot(s, w_ref[0], preferred_element_type=F32,
                       precision=lax.Precision.HIGHEST) + b_ref[0]


def _ada_mod(cond, ada_w, ada_b):
    depth, d, six_d = ada_w.shape
    rows = cond.shape[0]
    col = D_MODEL
    return pl.pallas_call(
        _ada_kernel,
        grid=(depth, six_d // col),
        in_specs=[
            pl.BlockSpec((rows, d), lambda i, j: (0, 0)),
            pl.BlockSpec((1, d, col), lambda i, j: (i, 0, j)),
            pl.BlockSpec((1, 1, col), lambda i, j: (i, 0, j)),
        ],
        out_specs=pl.BlockSpec((1, rows, col), lambda i, j: (i, 0, j)),
        out_shape=jax.ShapeDtypeStruct((depth, rows, six_d), F32),
        compiler_params=_cparams(("arbitrary", "arbitrary")),
    )(cond, ada_w, ada_b.reshape(depth, 1, six_d))


def _rope(r, cos, sin):
    w = r.shape[1]
    lane = lax.broadcasted_iota(jnp.int32, r.shape, 1)
    swapped = jnp.where((lane & 32) == 0, pltpu.roll(r, w - 32, 1), pltpu.roll(r, 32, 1))
    reps = w // LANES
    return r * jnp.tile(cos, (1, reps)) + swapped * jnp.tile(sin, (1, reps))


def _proj_kernel(*refs, segs, n_ctx, tm, has_rope, has_grow):
    it = iter(refs)
    x_ref, a_ref, b_ref, w_ref = next(it), next(it), next(it), next(it)
    cs_ref = next(it) if has_rope else None
    wgt_ref = next(it) if has_grow else None
    outs = [next(it) for _ in segs]
    grow_ref = next(it) if has_grow else None
    h_ref = next(it)

    row0 = pl.program_id(1) * tm
    h_ref[...] = _norm_mod(x_ref[0], a_ref[0], b_ref[0], row0, n_ctx).astype(BF16)
    for (c0, c1, rope, scale), o_ref in zip(segs, outs):
        r = jnp.dot(h_ref[...], w_ref[:, c0:c1], preferred_element_type=F32)
        if rope:
            r = _rope(r, cs_ref[:, :LANES], cs_ref[:, LANES:])
        if scale != 1.0:
            r = r * scale
        o_ref[0] = r.astype(o_ref.dtype)
    if has_grow:
        grow_ref[0] = lax.dot_general(wgt_ref[...], h_ref[...], (((1,), (1,)), ((), ())),
                                      preferred_element_type=F32)


def _project(x, a_pair, b_pair, w, segs, out_dtypes, n_ctx, cs=None, wgt=None):
    bsz, n, d = x.shape
    tm = TOKEN_TILE
    assert n % tm == 0
    tok = lambda b, t: (b, t, 0)
    in_specs = [
        pl.BlockSpec((1, tm, d), tok),
        pl.BlockSpec((1, 2, d), lambda b, t: (b, 0, 0)),
        pl.BlockSpec((1, 2, d), lambda b, t: (b, 0, 0)),
        _const_spec(w.shape),
    ]
    args = [x, a_pair, b_pair, w]
    if cs is not None:
        in_specs.append(pl.BlockSpec((tm, 2 * LANES), lambda b, t: (t, 0)))
        args.append(cs)
    if wgt is not None:
        in_specs.append(_const_spec(wgt.shape))
        args.append(wgt)
    out_specs = [pl.BlockSpec((1, tm, c1 - c0), tok) for (c0, c1, _, _) in segs]
    out_shape = [jax.ShapeDtypeStruct((bsz, n, c1 - c0), dt) for (c0, c1, _, _), dt in zip(segs, out_dtypes)]
    if wgt is not None:
        out_specs.append(pl.BlockSpec((1, wgt.shape[0], tm), lambda b, t: (b, 0, t)))
        out_shape.append(jax.ShapeDtypeStruct((bsz, wgt.shape[0], n), F32))
    kern = functools.partial(_proj_kernel, segs=segs, n_ctx=n_ctx, tm=tm,
                             has_rope=cs is not None, has_grow=wgt is not None)
    return pl.pallas_call(
        kern,
        grid=(bsz, n // tm),
        in_specs=in_specs,
        out_specs=out_specs,
        out_shape=out_shape,
        scratch_shapes=[pltpu.VMEM((tm, d), BF16)],
        compiler_params=_cparams(("arbitrary", "arbitrary")),
    )(*args)


def _split3(x):
    hi = x.astype(BF16)
    r1 = x - hi.astype(F32)
    mid = r1.astype(BF16)
    lo = (r1 - mid.astype(F32)).astype(BF16)
    return hi, mid, lo


_TN = (((0,), (0,)), ((), ()))
_NT = (((1,), (1,)), ((), ()))


def _expand_heads(x, width):
    nh = x.shape[0]
    x3 = jnp.concatenate(_split3(x), axis=0)
    r = lax.broadcasted_iota(jnp.int32, (3 * nh, nh * width), 0)
    c = lax.broadcasted_iota(jnp.int32, (3 * nh, nh * width), 1)
    sel = jnp.where((r & (nh - 1)) == (c >> int(math.log2(width))), 1.0, 0.0).astype(BF16)
    return lax.dot_general(x3, sel, _TN, preferred_element_type=F32)


ML_STAT_ROWS = 6 * ML_HEADS


def _mlstm_gate_kernel(gr_ref, br_ref, st_ref, *, t, chunks):
    nh = ML_HEADS
    d = pl.program_id(1)
    fwd = d == 0
    sgn = 1 - 2 * d
    row = lax.broadcasted_iota(jnp.int32, (t, t), 0)
    col = lax.broadcasted_iota(jnp.int32, (t, t), 1)
    tri_t = jnp.where((row - col) * sgn <= 0, 1.0, 0.0).astype(BF16)
    lane = lax.broadcasted_iota(jnp.int32, (nh, t), 1)
    scan_pos = jnp.where(fwd, lane, t - 1 - lane)
    for c in range(chunks):
        cols = slice(c * t, (c + 1) * t)
        gr = gr_ref[0, :, cols] + br_ref[...]
        gr = jnp.where(fwd, gr[:2 * nh], gr[2 * nh:])
        ig = gr[:nh]
        fg = _log_sigmoid(gr[nh:])
        bs = jnp.dot(jnp.concatenate(_split3(fg), axis=0), tri_t, preferred_element_type=F32)
        b = bs[:nh] + bs[nh:2 * nh] + bs[2 * nh:]
        b_end = jnp.sum(fg, axis=1, keepdims=True)
        src = ig - b
        cmax = src
        shift = 1
        while shift < t:
            moved = jnp.where(fwd, pltpu.roll(cmax, shift, 1), pltpu.roll(cmax, t - shift, 1))
            cmax = jnp.maximum(cmax, jnp.where(scan_pos >= shift, moved, -jnp.inf))
            shift *= 2
        w_end = b_end + src
        m_loc = jnp.max(w_end, axis=1, keepdims=True)
        e_end = jnp.exp(w_end - m_loc)
        st_ref[0, 0, :, cols] = jnp.concatenate(
            [src, cmax, b, e_end, jnp.broadcast_to(b_end, (nh, t)), jnp.broadcast_to(m_loc, (nh, t))], axis=0)


def _mlstm_gate_stats(grow, gate_b):
    bsz, rows, n = grow.shape
    t = ML_CHUNK
    assert t & (t - 1) == 0
    width = TOKEN_TILE * 2 if n % (TOKEN_TILE * 2) == 0 else t
    bias_row = jnp.broadcast_to(gate_b.reshape(rows, 1).astype(F32), (rows, t))
    return pl.pallas_call(
        functools.partial(_mlstm_gate_kernel, t=t, chunks=width // t),
        grid=(bsz, 2, n // width),
        in_specs=[pl.BlockSpec((1, rows, width), lambda b, d, i: (b, 0, i)),
                  pl.BlockSpec((rows, t), lambda b, d, i: (0, 0))],
        out_specs=pl.BlockSpec((1, 1, ML_STAT_ROWS, width), lambda b, d, i: (b, d, 0, i)),
        out_shape=jax.ShapeDtypeStruct((bsz, 2, ML_STAT_ROWS, n), F32),
        compiler_params=_cparams(("arbitrary", "arbitrary", "arbitrary")),
    )(grow, bias_row)


def _mlstm_kernel(qf_ref, kf_ref, vf_ref, stf_ref, qb_ref, kb_ref, vb_ref, stb_ref, of_ref, ob_ref,
                  ct_ref, m_ref, *, t):
    @pl.when(pl.program_id(1) == 0)
    def _():
        ct_ref[...] = jnp.zeros_like(ct_ref)
        m_ref[...] = jnp.zeros_like(m_ref)

    _mlstm_chunk(qf_ref, kf_ref, vf_ref, stf_ref, of_ref, ct_ref.at[0], m_ref.at[0], t=t, forward=True)
    _mlstm_chunk(qb_ref, kb_ref, vb_ref, stb_ref, ob_ref, ct_ref.at[1], m_ref.at[1], t=t, forward=False)


def _mlstm_chunk(q_ref, k_ref, v_ref, st_ref, o_ref, ct_ref, m_ref, *, t, forward):
    nh = ML_HEADS
    row = lax.broadcasted_iota(jnp.int32, (t, t), 0)
    col = lax.broadcasted_iota(jnp.int32, (t, t), 1)
    mask = (col <= row) if forward else (col >= row)
    st = st_ref[0, 0]
    src, cmax, b, e_end = (st[j * nh:(j + 1) * nh] for j in range(4))
    b_end = st[4 * nh:5 * nh, 0:1]
    m_loc = st[5 * nh:6 * nh, 0:1]

    m_in = m_ref[:, 0:1]
    big_m = jnp.maximum(m_in, cmax)
    m_new = jnp.maximum(b_end + m_in, m_loc)
    a_dec = jnp.exp(b_end + m_in - m_new)
    s_dec = jnp.exp(m_loc - m_new)

    q = q_ref[0]
    k = k_ref[0] * (ML_QK ** -0.5)
    v = v_ref[0]
    q_in = (q.astype(F32) * _expand_heads(jnp.exp(m_in - big_m), ML_QK)).astype(BF16)
    k_end = (k.astype(F32) * _expand_heads(e_end, ML_QK)).astype(BF16)
    floor = _expand_heads(jnp.exp(-(b + big_m)), ML_V)

    ones3 = jnp.ones((3 * nh, t), BF16)
    dec_l = jnp.concatenate(list(_split3(-big_m)) + [ones3], axis=0)
    dec_r = jnp.concatenate([ones3] + list(_split3(src)), axis=0)
    head_of_row = lax.broadcasted_iota(jnp.int32, (6 * nh, t), 0) & (nh - 1)
    ones = jnp.ones((t, ML_V), BF16)
    nums, dens = [], []
    for h in range(nh):
        qk = slice(h * ML_QK, (h + 1) * ML_QK)
        va = jnp.concatenate([v[:, h * ML_V:(h + 1) * ML_V], ones], axis=1)
        s = lax.dot_general(q[:, qk], k[:, qk], _NT, preferred_element_type=F32)
        w_log = lax.dot_general(dec_l, jnp.where(head_of_row == h, dec_r, jnp.zeros_like(dec_r)), _TN,
                                preferred_element_type=F32)
        w = (jnp.where(mask, jnp.exp(w_log), 0.0) * s).astype(BF16)
        ct = ct_ref[h]
        r = (jnp.dot(w, va, preferred_element_type=F32)
             + jnp.dot(q_in[:, qk], ct.astype(BF16), preferred_element_type=F32))
        nums.append(r[:, :ML_V])
        dens.append(r[:, ML_V:])
        c_loc = lax.dot_general(k_end[:, qk], va, _TN, preferred_element_type=F32)
        ct_ref[h] = a_dec[h:h + 1] * ct + s_dec[h:h + 1] * c_loc
    num = jnp.concatenate(nums, axis=1)
    den = jnp.concatenate(dens, axis=1)
    o_ref[0] = num / jnp.maximum(jnp.abs(den), floor)
    m_ref[...] = jnp.broadcast_to(m_new, m_ref.shape)


def _mlstm_scan(q, k, v, grow, gate_b, n_ctx):
    bsz, n, _ = q.shape
    t = ML_CHUNK
    assert t & (t - 1) == 0
    nc = n // t
    ncc = n_ctx // t

    def back(i):
        return jnp.where(i < ncc, ncc - 1 - i, nc - 1 - (i - ncc))

    stats = _mlstm_gate_stats(grow, gate_b)
    fwd = lambda b, i: (b, i, 0)
    bwd = lambda b, i: (b, back(i), 0)
    specs = lambda tok, d: [
        pl.BlockSpec((1, t, ML_QKW), tok), pl.BlockSpec((1, t, ML_QKW), tok), pl.BlockSpec((1, t, D_MODEL), tok),
        pl.BlockSpec((1, 1, ML_STAT_ROWS, t), lambda b, i: (b, d, 0, tok(b, i)[1]))]
    out = jax.ShapeDtypeStruct((bsz, n, D_MODEL), F32)
    return pl.pallas_call(
        functools.partial(_mlstm_kernel, t=t),
        grid=(bsz, nc),
        in_specs=specs(fwd, 0) + specs(bwd, 1),
        out_specs=[pl.BlockSpec((1, t, D_MODEL), fwd), pl.BlockSpec((1, t, D_MODEL), bwd)],
        out_shape=[out, out],
        scratch_shapes=[pltpu.VMEM((2, ML_HEADS, ML_QK, 2 * ML_V), F32), pltpu.VMEM((2, ML_HEADS, LANES), F32)],
        compiler_params=_cparams(("arbitrary", "arbitrary")),
    )(q, k, v, stats, q, k, v, stats)


def _diff_kernel(lam_ref, q_ref, kt_ref, v_ref, o_ref, acc_ref, m_ref, s_ref, smax_ref, va_ref, *, tq, tk, n,
                 n_ctx, lambda_init):
    qi = pl.program_id(2)

    @pl.when(qi == 0)
    def _():
        def fill(c, carry):
            rows = pl.ds(pl.multiple_of(c * tk, tk), tk)
            va_ref[rows, :DF_VD] = v_ref[0, rows, :]
            va_ref[rows, DF_VD:] = jnp.ones((tk, DF_VD), va_ref.dtype)
            return carry
        lax.fori_loop(0, n // tk, fill, 0)

    q = q_ref[0]
    lane = lax.broadcasted_iota(jnp.int32, q.shape, 1)
    zero = jnp.zeros_like(q)
    qs = (jnp.where(lane < DF_HD, q, zero), jnp.where(lane < DF_HD, zero, q))
    acc_ref[...] = jnp.zeros_like(acc_ref)
    m_ref[...] = jnp.full_like(m_ref, NEG_INF)

    def scores(start, size, slot):
        kt = kt_ref[0, :, pl.ds(start, size)]
        for j in range(2):
            s = jnp.dot(qs[j], kt, preferred_element_type=F32)
            s_ref[slot, j, :, :size] = s
            smax_ref[slot, j] = jnp.max(s, axis=1, keepdims=True)

    def accumulate(start, size, slot):
        va = va_ref[pl.ds(start, size), :]
        ss = [s_ref[slot, j, :, :size] for j in range(2)]
        m_old = [m_ref[j] for j in range(2)]
        m_new = [jnp.maximum(m_old[j], smax_ref[slot, j]) for j in range(2)]
        ps = [jnp.exp2((ss[j] - m_new[j]).astype(BF16)) for j in range(2)]
        pv = [jnp.dot(ps[j], va, preferred_element_type=F32) for j in range(2)]
        for j in range(2):
            acc_ref[j] = jnp.exp2(m_old[j] - m_new[j]) * acc_ref[j] + pv[j]
            m_ref[j] = m_new[j]

    is_ctx_tile = qi * tq < n_ctx

    @pl.when(is_ctx_tile)
    def _():
        scores(0, n_ctx, 0)
        accumulate(0, n_ctx, 0)

    @pl.when(jnp.logical_not(is_ctx_tile))
    def _():
        n_chunks = n // tk
        unroll = ATT_UNROLL if (n_chunks - 1) % ATT_UNROLL == 0 else 2
        scores(0, tk, 0)

        def body(i, carry):
            c0 = pl.multiple_of(unroll * i * tk, tk)
            for u in range(unroll):
                scores(c0 + (u + 1) * tk, tk, (u + 1) % 2)
                accumulate(c0 + u * tk, tk, u % 2)
            return carry
        lax.fori_loop(0, (n_chunks - 1) // unroll, body, 0)
        accumulate((n_chunks - 1) * tk, tk, 0)

    lam = lam_ref[...]
    lam_full = (jnp.exp(jnp.sum(lam[0:1] * lam[1:2], axis=1, keepdims=True))
                - jnp.exp(jnp.sum(lam[2:3] * lam[3:4], axis=1, keepdims=True)) + lambda_init)
    o1 = acc_ref[0, :, :DF_VD] / acc_ref[0, :, DF_VD:]
    o2 = acc_ref[1, :, :DF_VD] / acc_ref[1, :, DF_VD:]
    o_ref[0] = o1 - lam_full * o2


def _diff_attention(q, k, v, lam, lambda_init, n_ctx):
    bsz, n, _ = q.shape
    tq, tk = ATT_Q_TILE, ATT_KV_TILE
    assert n_ctx % tq == 0 and n % tq == 0 and n % tk == 0 and (n // tk) % 2 == 1 and n_ctx <= tk
    kt = jnp.swapaxes(k, 1, 2)
    kern = functools.partial(_diff_kernel, tq=tq, tk=tk, n=n, n_ctx=n_ctx, lambda_init=lambda_init)
    return pl.pallas_call(
        kern,
        grid=(bsz, DF_HEADS, n // tq),
        in_specs=[
            pl.BlockSpec(lam.shape, lambda b, h, i: (0, 0)),
            pl.BlockSpec((1, tq, 2 * DF_HD), lambda b, h, i: (b, i, h)),
            pl.BlockSpec((1, 2 * DF_HD, n), lambda b, h, i: (b, h, 0)),
            pl.BlockSpec((1, n, DF_VD), lambda b, h, i: (b, 0, h)),
        ],
        out_specs=pl.BlockSpec((1, tq, DF_VD), lambda b, h, i: (b, i, h)),
        out_shape=jax.ShapeDtypeStruct((bsz, n, D_MODEL), F32),
        scratch_shapes=[pltpu.VMEM((2, tq, 2 * DF_VD), F32), pltpu.VMEM((2, tq, 1), F32),
                        pltpu.VMEM((2, 2, tq, tk), F32), pltpu.VMEM((2, 2, tq, 1), F32),
                        pltpu.VMEM((n, 2 * DF_VD), v.dtype)],
        compiler_params=_cparams(("arbitrary", "arbitrary", "arbitrary")),
    )(lam.astype(F32), q, kt, v)


def _swa_mask_table(tq, bw):
    t = jnp.arange(tq, dtype=jnp.int32)[:, None]
    s = jnp.arange(bw, dtype=jnp.int32)[None, :]

    def one(off, first_ok):
        ok = jnp.logical_and(s >= first_ok, jnp.abs(t - s + off) <= SW_WIN)
        return jnp.where(ok, 0.0, NEG_INF).astype(F32)
    return jnp.stack([one(SW_WIN, 0), one(SW_WIN, SW_WIN), one(2 * SW_WIN, 0), one(0, bw)])


def _swa_kernel(sink_ref, q_ref, kt_ref, va_ref, mask_ref, o_ref, *, tq, n, n_ctx):
    g = pl.program_id(1)
    i = pl.program_id(2)
    bw = tq + 2 * SW_WIN
    q = q_ref[0]
    qh = jnp.concatenate([q[:, u * SW_HD:(u + 1) * SW_HD] for u in range(SW_GROUP)], axis=0)
    start = pl.multiple_of(jnp.clip(i * tq - SW_WIN, 0, n - bw), SW_WIN)
    s_c = jnp.dot(qh, kt_ref[0, :, 0:n_ctx], preferred_element_type=F32)
    s_b = jnp.dot(qh, kt_ref[0, :, pl.ds(start, bw)], preferred_element_type=F32)
    case = jnp.where(i * tq < n_ctx, 3, jnp.where(i * tq - SW_WIN < n_ctx, 1,
                                                   jnp.where(i * tq - SW_WIN > n - bw, 2, 0)))
    s_b = s_b + jnp.tile(mask_ref[case], (SW_GROUP, 1))
    head = lax.broadcasted_iota(jnp.int32, (SW_GROUP * tq, 1), 0) // tq
    sink = jnp.zeros((SW_GROUP * tq, 1), F32)
    for u in range(SW_GROUP):
        sink = jnp.where(head == u, sink_ref[g * SW_GROUP + u] * LOG2E, sink)
    m = jnp.maximum(sink, jnp.maximum(jnp.max(s_c, axis=1, keepdims=True), jnp.max(s_b, axis=1, keepdims=True)))
    acc = (jnp.dot(jnp.exp2((s_c - m).astype(BF16)), va_ref[0, 0:n_ctx, :], preferred_element_type=F32)
           + jnp.dot(jnp.exp2((s_b - m).astype(BF16)), va_ref[0, pl.ds(start, bw), :],
                     preferred_element_type=F32))
    out = acc[:, :SW_HD] / (acc[:, SW_HD:] + jnp.exp2(sink - m))
    o_ref[0] = jnp.concatenate([out[u * tq:(u + 1) * tq] for u in range(SW_GROUP)], axis=1).astype(o_ref.dtype)


def _swa_attention(q, k, v, sinks, n_ctx):
    bsz, n, _ = q.shape
    tq = SWA_Q_TILE
    assert tq & (tq - 1) == 0 and n_ctx % tq == 0 and n % tq == 0 and tq % SW_WIN == 0
    bw = tq + 2 * SW_WIN
    assert n - bw >= n_ctx + tq
    gw = SW_GROUP * SW_HD
    kt = jnp.swapaxes(k, 1, 2)
    va = jnp.concatenate([v.reshape(bsz, n, SW_KV, SW_HD),
                          jnp.ones((bsz, n, SW_KV, SW_HD), v.dtype)], axis=-1).reshape(bsz, n, SW_KV * 2 * SW_HD)
    return pl.pallas_call(
        functools.partial(_swa_kernel, tq=tq, n=n, n_ctx=n_ctx),
        grid=(bsz, SW_KV, n // tq),
        in_specs=[
            pl.BlockSpec(memory_space=pltpu.SMEM),
            pl.BlockSpec((1, tq, gw), lambda b, g, i: (b, i, g)),
            pl.BlockSpec((1, SW_HD, n), lambda b, g, i: (b, g, 0)),
            pl.BlockSpec((1, n, 2 * SW_HD), lambda b, g, i: (b, 0, g)),
            _const_spec((4, tq, bw)),
        ],
        out_specs=pl.BlockSpec((1, tq, gw), lambda b, g, i: (b, i, g)),
        out_shape=jax.ShapeDtypeStruct((bsz, n, D_MODEL), BF16),
        compiler_params=_cparams(("arbitrary", "arbitrary", "arbitrary")),
    )(sinks.astype(F32), q, kt, va, _swa_mask_table(tq, bw))


def _head_norm(y, g, width):
    parts = []
    for h in range(y.shape[1] // width):
        yh = y[:, h * width:(h + 1) * width]
        parts.append(yh * lax.rsqrt(jnp.mean(yh * yh, axis=-1, keepdims=True) + EPS))
    return jnp.concatenate(parts, axis=1) * g


def _mixer_out_kernel(*refs, mode, n_ctx, tm, post_scale):
    it = iter(refs)
    x_ref, gate_ref, w_ref = next(it), next(it), next(it)
    if mode == "mlstm":
        hf_ref, hb_ref, og_ref, g_ref = next(it), next(it), next(it), next(it)
        y = _head_norm(hf_ref[0] + hb_ref[0], g_ref[...], ML_V)
        z = y * _sigmoid(og_ref[0].astype(F32))
    elif mode == "diff":
        a_ref, g_ref = next(it), next(it)
        z = _head_norm(a_ref[0], g_ref[...], DF_VD) * post_scale
    else:
        a_ref = next(it)
        z = a_ref[0]
    o_ref = next(it)
    y = jnp.dot(z.astype(BF16), w_ref[...], preferred_element_type=F32)
    gate = _row_select(gate_ref[0], pl.program_id(1) * tm, tm, n_ctx)
    o_ref[0] = x_ref[0] + gate * y


def _mixer_out(x, gate_pair, w_out, mode, n_ctx, extra, post_scale=1.0):
    bsz, n, d = x.shape
    tm = TOKEN_TILE
    tok = lambda b, t: (b, t, 0)
    in_specs = [pl.BlockSpec((1, tm, d), tok), pl.BlockSpec((1, 2, d), lambda b, t: (b, 0, 0)),
                _const_spec(w_out.shape)]
    args = [x, gate_pair, w_out]
    if mode == "mlstm":
        hf, hb, og, g = extra
        in_specs += [pl.BlockSpec((1, tm, d), tok), pl.BlockSpec((1, tm, d), tok), pl.BlockSpec((1, tm, d), tok),
                     pl.BlockSpec((1, d), lambda b, t: (0, 0))]
        args += [hf, hb, og, g.reshape(1, d)]
    elif mode == "diff":
        a, g = extra
        in_specs += [pl.BlockSpec((1, tm, d), tok), pl.BlockSpec((1, d), lambda b, t: (0, 0))]
        args += [a, g.reshape(1, d)]
    else:
        (a,) = extra
        in_specs += [pl.BlockSpec((1, tm, d), tok)]
        args += [a]
    kern = functools.partial(_mixer_out_kernel, mode=mode, n_ctx=n_ctx, tm=tm, post_scale=post_scale)
    return pl.pallas_call(
        kern,
        grid=(bsz, n // tm),
        in_specs=in_specs,
        out_specs=pl.BlockSpec((1, tm, d), tok),
        out_shape=jax.ShapeDtypeStruct((bsz, n, d), F32),
        compiler_params=_cparams(("arbitrary", "arbitrary")),
    )(*args)


def _dense_ffn_kernel(x_ref, a_ref, b_ref, gate_ref, wa_ref, wb_ref, wo_ref, o_ref, h_ref, g_ref,
                      *, n_ctx, tm, bounds):
    row0 = pl.program_id(1) * tm
    x = x_ref[0]
    h_ref[...] = _norm_mod(x, a_ref[0], b_ref[0], row0, n_ctx).astype(BF16)
    for c0, c1 in zip(bounds[:-1], bounds[1:]):
        a = jnp.dot(h_ref[...], wa_ref[:, c0:c1], preferred_element_type=F32)
        b = jnp.dot(h_ref[...], wb_ref[:, c0:c1], preferred_element_type=F32)
        g_ref[:, c0:c1] = (a * _sigmoid(a) * b).astype(BF16)
    y = jnp.dot(g_ref[...], wo_ref[...], preferred_element_type=F32)
    o_ref[0] = x + _row_select(gate_ref[0], row0, tm, n_ctx) * y


def _dense_ffn(x, a_pair, b_pair, gate_pair, wa, wb, wo, n_ctx):
    bsz, n, d = x.shape
    tm = TOKEN_TILE
    ff = wa.shape[1]
    half = (pl.cdiv(ff // MXU_WIDTH, 2) * MXU_WIDTH) if ff % MXU_WIDTH == 0 else ff
    bounds = (0, half, ff) if half < ff else (0, ff)
    tok = lambda b, t: (b, t, 0)
    pair = pl.BlockSpec((1, 2, d), lambda b, t: (b, 0, 0))
    kern = functools.partial(_dense_ffn_kernel, n_ctx=n_ctx, tm=tm, bounds=bounds)
    return pl.pallas_call(
        kern,
        grid=(bsz, n // tm),
        in_specs=[pl.BlockSpec((1, tm, d), tok), pair, pair, pair,
                  _const_spec(wa.shape), _const_spec(wb.shape), _const_spec(wo.shape)],
        out_specs=pl.BlockSpec((1, tm, d), tok),
        out_shape=jax.ShapeDtypeStruct((bsz, n, d), F32),
        scratch_shapes=[pltpu.VMEM((tm, d), BF16), pltpu.VMEM((tm, ff), BF16)],
        compiler_params=_cparams(("arbitrary", "arbitrary")),
    )(x, a_pair, b_pair, gate_pair, wa, wb, wo)


def _moe_route_kernel(x_ref, a_ref, b_ref, r_ref, h_ref, g_ref, i_ref, *, n_ctx, tm):
    row0 = pl.program_id(1) * tm
    h = _norm_mod(x_ref[0], a_ref[0], b_ref[0], row0, n_ctx)
    h_ref[0] = h.astype(BF16)
    logits = lax.dot_general(r_ref[...], h, (((1,), (1,)), ((), ())), preferred_element_type=F32,
                             precision=lax.Precision.HIGHEST)
    e_idx = lax.broadcasted_iota(jnp.int32, logits.shape, 0)
    v1 = jnp.max(logits, axis=0, keepdims=True)
    i1 = jnp.min(jnp.where(logits == v1, e_idx, N_EXPERTS), axis=0, keepdims=True)
    rest = jnp.where(e_idx == i1, -jnp.inf, logits)
    v2 = jnp.max(rest, axis=0, keepdims=True)
    i2 = jnp.min(jnp.where(rest == v2, e_idx, N_EXPERTS), axis=0, keepdims=True)
    p2 = jnp.exp(v2 - v1)
    den = 1.0 + p2
    g_ref[0] = jnp.concatenate([1.0 / den, p2 / den], axis=0)
    i_ref[0] = jnp.concatenate([i1, i2], axis=0)


def _moe_route(x, a_pair, b_pair, router_t, n_ctx):
    bsz, n, d = x.shape
    tm = TOKEN_TILE
    tok = lambda b, t: (b, t, 0)
    pair = pl.BlockSpec((1, 2, d), lambda b, t: (b, 0, 0))
    return pl.pallas_call(
        functools.partial(_moe_route_kernel, n_ctx=n_ctx, tm=tm),
        grid=(bsz, n // tm),
        in_specs=[pl.BlockSpec((1, tm, d), tok), pair, pair, _const_spec(router_t.shape)],
        out_specs=[pl.BlockSpec((1, tm, d), tok),
                   pl.BlockSpec((1, TOP_K, tm), lambda b, t: (b, 0, t)),
                   pl.BlockSpec((1, TOP_K, tm), lambda b, t: (b, 0, t))],
        out_shape=[jax.ShapeDtypeStruct((bsz, n, d), BF16),
                   jax.ShapeDtypeStruct((bsz, TOP_K, n), F32),
                   jax.ShapeDtypeStruct((bsz, TOP_K, n), jnp.int32)],
        compiler_params=_cparams(("arbitrary", "arbitrary")),
    )(x, a_pair, b_pair, router_t)


def _moe_expert_kernel(te_ref, nu_ref, x_ref, wi_ref, wo_ref, o_ref, g_ref, *, ff, chunk):
    i = pl.program_id(0)

    @pl.when(i < nu_ref[0])
    def _():
        x = x_ref[...]
        for c0 in range(0, ff, chunk):
            a = jnp.dot(x, wi_ref[0, :, c0:c0 + chunk], preferred_element_type=F32)
            b = jnp.dot(x, wi_ref[0, :, ff + c0:ff + c0 + chunk], preferred_element_type=F32)
            g_ref[:, c0:c0 + chunk] = (a * _sigmoid(a) * b).astype(BF16)
        o_ref[...] = jnp.dot(g_ref[...], wo_ref[0], preferred_element_type=F32).astype(o_ref.dtype)

    @pl.when(i >= nu_ref[0])
    def _():
        o_ref[...] = jnp.zeros_like(o_ref)


def _moe_experts(xs, w_in, w_out, tile_expert, n_used):
    p, d = xs.shape
    tm = MOE_TILE
    ff = w_out.shape[1]
    chunk = MOE_FF_CHUNK
    assert ff % chunk == 0 and chunk % LANES == 0
    nt = p // tm

    def teff(i, nu):
        return jnp.minimum(i, jnp.maximum(nu[0] - 1, 0))

    grid_spec = pltpu.PrefetchScalarGridSpec(
        num_scalar_prefetch=2,
        grid=(nt,),
        in_specs=[
            pl.BlockSpec((tm, d), lambda i, te, nu: (teff(i, nu), 0)),
            pl.BlockSpec((1, d, 2 * ff), lambda i, te, nu: (te[teff(i, nu)], 0, 0),
                         pipeline_mode=pl.Buffered(1)),
            pl.BlockSpec((1, ff, d), lambda i, te, nu: (te[teff(i, nu)], 0, 0),
                         pipeline_mode=pl.Buffered(1)),
        ],
        out_specs=pl.BlockSpec((tm, d), lambda i, te, nu: (i, 0)),
        scratch_shapes=[pltpu.VMEM((tm, ff), BF16)],
    )
    return pl.pallas_call(
        functools.partial(_moe_expert_kernel, ff=ff, chunk=chunk),
        grid_spec=grid_spec,
        out_shape=jax.ShapeDtypeStruct((p, d), BF16),
        compiler_params=_cparams(("arbitrary",)),
    )(tile_expert, n_used, xs, w_in, w_out)


def _moe_combine_kernel(*refs, n_ctx, tm, final):
    it = iter(refs)
    x_ref, gate_ref, y0_ref, y1_ref, pk_ref = next(it), next(it), next(it), next(it), next(it)
    fg_ref = next(it) if final else None
    o_ref = next(it)
    pk = pk_ref[0]
    y = pk[:, 0:1] * y0_ref[0].astype(F32) + pk[:, 1:2] * y1_ref[0].astype(F32)
    x = x_ref[0] + _row_select(gate_ref[0], pl.program_id(1) * tm, tm, n_ctx) * y
    if final:
        x = x * lax.rsqrt(jnp.mean(x * x, axis=-1, keepdims=True) + EPS) * fg_ref[...]
    o_ref[0] = x


def _moe_combine(x, gate_pair, y0, y1, slot_w, n_ctx, final_g=None):
    bsz, n, d = x.shape
    tm = TOKEN_TILE
    tok = lambda b, t: (b, t, 0)
    in_specs = [pl.BlockSpec((1, tm, d), tok), pl.BlockSpec((1, 2, d), lambda b, t: (b, 0, 0)),
                pl.BlockSpec((1, tm, d), tok), pl.BlockSpec((1, tm, d), tok),
                pl.BlockSpec((1, tm, TOP_K), tok)]
    args = [x, gate_pair, y0, y1, slot_w]
    if final_g is not None:
        in_specs.append(pl.BlockSpec((1, d), lambda b, t: (0, 0)))
        args.append(final_g.reshape(1, d))
    return pl.pallas_call(
        functools.partial(_moe_combine_kernel, n_ctx=n_ctx, tm=tm, final=final_g is not None),
        grid=(bsz, n // tm),
        in_specs=in_specs,
        out_specs=pl.BlockSpec((1, tm, d), tok),
        out_shape=jax.ShapeDtypeStruct((bsz, n, d), F32),
        compiler_params=_cparams(("arbitrary", "arbitrary")),
    )(*args)


def _moe_layer(x, a_pair, b_pair, gate_pair, router, w_in_all, w_out_all, layer, n_ctx, final_g):
    bsz, n, d = x.shape
    tm = MOE_TILE
    ntok = bsz * n
    h2, gates, idx = _moe_route(x, a_pair, b_pair, router.T, n_ctx)
    w_in_all, w_out_all, h2 = lax.optimization_barrier((w_in_all, w_out_all, h2))
    w_in = w_in_all[layer].astype(BF16)
    w_out = w_out_all[layer].astype(BF16)
    e_flat = jnp.moveaxis(idx, 1, 0).reshape(TOP_K * ntok)
    onehot = (e_flat[:, None] == jnp.arange(N_EXPERTS, dtype=jnp.int32)[None, :]).astype(jnp.int32)
    csum = jnp.cumsum(onehot, axis=0)
    counts = csum[-1]
    rank = jnp.sum(onehot * csum, axis=1) - 1
    padded = ((counts + tm - 1) // tm) * tm
    ends = jnp.cumsum(padded)
    starts = ends - padded
    dest = jnp.sum(onehot * starts[None, :], axis=1) + rank
    n_rows = TOP_K * ntok + N_EXPERTS * tm
    n_rows = ((n_rows + tm - 1) // tm) * tm
    tok_of = jnp.tile(jnp.arange(ntok, dtype=jnp.int32), TOP_K)
    src = (jnp.arange(n_rows, dtype=jnp.int32) % ntok).at[dest].set(tok_of)
    tile_start = jnp.arange(n_rows // tm, dtype=jnp.int32) * tm
    tile_expert = jnp.minimum(jnp.sum((tile_start[:, None] >= ends[None, :]).astype(jnp.int32), axis=1),
                              N_EXPERTS - 1).astype(jnp.int32)
    n_used = (ends[-1] // tm).astype(jnp.int32).reshape(1)
    rows = lambda a, i: a.at[i].get(mode="promise_in_bounds")
    xs = rows(h2.reshape(ntok, d), src)
    ys = _moe_experts(xs, w_in, w_out, tile_expert, n_used)
    y0 = rows(ys, dest[:ntok]).reshape(bsz, n, d)
    y1 = rows(ys, dest[ntok:]).reshape(bsz, n, d)
    return _moe_combine(x, gate_pair, y0, y1, jnp.swapaxes(gates, 1, 2), n_ctx, final_g)


def _rope_table(n_ctx, seq):
    pos = jnp.arange(seq)
    row = (pos // GRID_W).astype(F32)
    col = (pos % GRID_W).astype(F32)
    nf = DF_HD // 4
    inv = jnp.power(ROPE_BASE, -jnp.arange(nf, dtype=F32) / nf)
    ang = jnp.concatenate([row[:, None] * inv, col[:, None] * inv], axis=-1)
    cos, sin = jnp.cos(ang), jnp.sin(ang)
    cos = jnp.concatenate([jnp.ones((n_ctx, 2 * nf), F32), cos], axis=0)
    sin = jnp.concatenate([jnp.zeros((n_ctx, 2 * nf), F32), sin], axis=0)
    c128 = jnp.tile(jnp.concatenate([cos, cos], axis=1), (1, 2))
    s128 = jnp.tile(jnp.concatenate([-sin, sin], axis=1), (1, 2))
    return jnp.concatenate([c128, s128], axis=1)


def _pair(v, bsz):
    return jnp.stack([v[:bsz], jnp.broadcast_to(v[bsz], (bsz, v.shape[1]))], axis=1)


def kernel(x, c, ctx, c_ctx, ada_w, ada_b, norm1, norm2, ml_w_in, ml_gate_b, ml_hnorm, ml_w_out,
           df_w_in, df_lam, df_hnorm, df_w_out, sw_w_in, sw_sinks, sw_w_out,
           ffn_w_in, ffn_w_out, moe_router, moe_w_in, moe_w_out, final_norm):
    bsz, seq, d = x.shape
    n_ctx = ctx.shape[1]
    depth = ada_w.shape[0]
    assert d == D_MODEL and n_ctx % ATT_Q_TILE == 0 and (n_ctx + seq) % TOKEN_TILE == 0

    xs = jnp.concatenate([ctx, x], axis=1)
    cond_rows = SUBLANES * ((bsz + 1 + SUBLANES - 1) // SUBLANES)
    cond = jnp.zeros((cond_rows, d), F32).at[:bsz].set(c).at[bsz].set(c_ctx)
    mods = _ada_mod(cond, ada_w, ada_b)
    cs = _rope_table(n_ctx, seq)

    for i in range(depth):
        m = [mods[i, :, k * d:(k + 1) * d] for k in range(6)]
        a1 = _pair(norm1[i][None, :] * (1.0 + m[1]), bsz)
        b1 = _pair(m[0], bsz)
        g1 = _pair(m[2], bsz)
        a2 = _pair(norm2[i][None, :] * (1.0 + m[4]), bsz)
        b2 = _pair(m[3], bsz)
        g2 = _pair(m[5], bsz)
        kind, j = i % N_MIXERS, i // N_MIXERS
        if kind == 0:
            w = ml_w_in[j]
            nmain = 2 * ML_QKW + 2 * D_MODEL
            w_main = w[:, :nmain].astype(BF16)
            wgt = w[:, nmain:].T.astype(BF16)
            segs = ((0, ML_QKW, False, 1.0), (ML_QKW, 2 * ML_QKW, False, 1.0),
                    (2 * ML_QKW, 2 * ML_QKW + d, False, 1.0), (2 * ML_QKW + d, nmain, False, 1.0))
            q, k, v, og, grow = _project(xs, a1, b1, w_main, segs, (BF16, BF16, BF16, BF16), n_ctx, wgt=wgt)
            hf, hb = _mlstm_scan(q, k, v, grow, ml_gate_b[j], n_ctx)
            xs = _mixer_out(xs, g1, ml_w_out[j].astype(BF16), "mlstm", n_ctx, (hf, hb, og, ml_hnorm[j]))
        elif kind == 1:
            lambda_init = 0.8 - 0.6 * math.exp(-0.3 * i)
            segs = ((0, DF_QKW, True, DF_HD ** -0.5 * LOG2E), (DF_QKW, 2 * DF_QKW, True, 1.0),
                    (2 * DF_QKW, 2 * DF_QKW + d, False, 1.0))
            q, k, v = _project(xs, a1, b1, df_w_in[j].astype(BF16), segs, (BF16, BF16, BF16), n_ctx, cs=cs)
            att = _diff_attention(q, k, v, df_lam[j], lambda_init, n_ctx)
            xs = _mixer_out(xs, g1, df_w_out[j].astype(BF16), "diff", n_ctx, (att, df_hnorm[j]),
                            post_scale=1.0 - lambda_init)
        else:
            nq = SW_HEADS * SW_HD
            nkv = SW_KV * SW_HD
            segs = ((0, nq, True, SW_HD ** -0.5 * LOG2E), (nq, nq + nkv, True, 1.0), (nq + nkv, nq + 2 * nkv, False, 1.0))
            q, k, v = _project(xs, a1, b1, sw_w_in[j].astype(BF16), segs, (BF16, BF16, BF16), n_ctx, cs=cs)
            att = _swa_attention(q, k, v, sw_sinks[j], n_ctx)
            xs = _mixer_out(xs, g1, sw_w_out[j].astype(BF16), "plain", n_ctx, (att,))
        jf = i // 2
        last = i == depth - 1
        if i % 2 == 0:
            ff = ffn_w_out.shape[1]
            wi = ffn_w_in[jf].astype(BF16)
            xs = _dense_ffn(xs, a2, b2, g2, wi[:, :ff], wi[:, ff:], ffn_w_out[jf].astype(BF16), n_ctx)
            if last:
                raise NotImplementedError("final norm is fused into the MoE combine")
        else:
            xs = _moe_layer(xs, a2, b2, g2, moe_router[jf], moe_w_in, moe_w_out, jf, n_ctx,
                            final_norm if last else None)
    return xs[:, n_ctx:]
```

```python
import functools
import math

import jax
import jax.numpy as jnp
from jax import lax
from jax.experimental import pallas as pl
from jax.experimental.pallas import tpu as pltpu

F32 = jnp.float32
BF16 = jnp.bfloat16

D_MODEL = 1024
GRID_W = 64
EPS = 1e-6
NEG_INF = -1e30
ROPE_BASE = 10000.0
LOG2E = math.log2(math.e)
N_MIXERS = 3

ML_HEADS = 8
ML_V = D_MODEL // ML_HEADS
ML_QK = ML_V // 2
ML_QKW = ML_HEADS * ML_QK
ML_CHUNK = 128

DF_HD = 64
DF_HEADS = D_MODEL // (2 * DF_HD)
DF_VD = 2 * DF_HD
DF_QKW = 2 * DF_HEADS * DF_HD

SW_HD = 64
SW_HEADS = D_MODEL // SW_HD
SW_KV = 4
SW_GROUP = SW_HEADS // SW_KV
SW_WIN = 128

N_EXPERTS = 8
TOP_K = 2

LANES = 128
SUBLANES = 8
MXU_WIDTH = 256
V7X_VMEM_BYTES = 64 * 1024 * 1024
VMEM_LIMIT = V7X_VMEM_BYTES - 8 * 1024 * 1024

TOKEN_TILE = 640
ATT_Q_TILE = 256
ATT_KV_TILE = 1280
ATT_UNROLL = 12
SWA_Q_TILE = 256
MOE_TILE = 512
MOE_FF_CHUNK = 1792


def _cparams(semantics):
    return pltpu.CompilerParams(dimension_semantics=semantics, vmem_limit_bytes=VMEM_LIMIT)


def _const_spec(shape):
    nd = len(shape)
    return pl.BlockSpec(shape, lambda *_: (0,) * nd, pipeline_mode=pl.Buffered(1))


def _sigmoid(x):
    return 1.0 / (1.0 + jnp.exp(-x))


def _log_sigmoid(x):
    return jnp.minimum(x, 0.0) - jnp.log(1.0 + jnp.exp(-jnp.abs(x)))


def _norm_mod(x, a_rows, b_rows, row0, n_ctx):
    xn = x * lax.rsqrt(jnp.mean(x * x, axis=-1, keepdims=True) + EPS)
    is_ctx = (row0 + lax.broadcasted_iota(jnp.int32, (x.shape[0], 1), 0)) < n_ctx
    a = jnp.where(is_ctx, a_rows[1:2], a_rows[0:1])
    b = jnp.where(is_ctx, b_rows[1:2], b_rows[0:1])
    return xn * a + b


def _row_select(rows, row0, n_rows, n_ctx):
    is_ctx = (row0 + lax.broadcasted_iota(jnp.int32, (n_rows, 1), 0)) < n_ctx
    return jnp.where(is_ctx, rows[1:2], rows[0:1])


def _ada_kernel(c_ref, w_ref, b_ref, o_ref):
    c = c_ref[...]
    s = c * _sigmoid(c)
    o_ref[0] = jnp.dot(s, w_ref[0], preferred_element_type=F32,
                       precision=lax.Precision.HIGHEST) + b_ref[0]


def _ada_mod(cond, ada_w, ada_b):
    depth, d, six_d = ada_w.shape
    rows = cond.shape[0]
    col = D_MODEL
    return pl.pallas_call(
        _ada_kernel,
        grid=(depth, six_d // col),
        in_specs=[
            pl.BlockSpec((rows, d), lambda i, j: (0, 0)),
            pl.BlockSpec((1, d, col), lambda i, j: (i, 0, j)),
            pl.BlockSpec((1, 1, col), lambda i, j: (i, 0, j)),
        ],
        out_specs=pl.BlockSpec((1, rows, col), lambda i, j: (i, 0, j)),
        out_shape=jax.ShapeDtypeStruct((depth, rows, six_d), F32),
        compiler_params=_cparams(("arbitrary", "arbitrary")),
    )(cond, ada_w, ada_b.reshape(depth, 1, six_d))


def _rope(r, cos, sin):
    w = r.shape[1]
    lane = lax.broadcasted_iota(jnp.int32, r.shape, 1)
    swapped = jnp.where((lane & 32) == 0, pltpu.roll(r, w - 32, 1), pltpu.roll(r, 32, 1))
    reps = w // LANES
    return r * jnp.tile(cos, (1, reps)) + swapped * jnp.tile(sin, (1, reps))


def _proj_kernel(*refs, segs, n_ctx, tm, has_rope, has_grow):
    it = iter(refs)
    x_ref, a_ref, b_ref, w_ref = next(it), next(it), next(it), next(it)
    cs_ref = next(it) if has_rope else None
    wgt_ref = next(it) if has_grow else None
    outs = [next(it) for _ in segs]
    grow_ref = next(it) if has_grow else None
    h_ref = next(it)

    row0 = pl.program_id(1) * tm
    h_ref[...] = _norm_mod(x_ref[0], a_ref[0], b_ref[0], row0, n_ctx).astype(BF16)
    for (c0, c1, rope, scale), o_ref in zip(segs, outs):
        r = jnp.dot(h_ref[...], w_ref[:, c0:c1], preferred_element_type=F32)
        if rope:
            r = _rope(r, cs_ref[:, :LANES], cs_ref[:, LANES:])
        if scale != 1.0:
            r = r * scale
        o_ref[0] = r.astype(o_ref.dtype)
    if has_grow:
        grow_ref[0] = lax.dot_general(wgt_ref[...], h_ref[...], (((1,), (1,)), ((), ())),
                                      preferred_element_type=F32)


def _project(x, a_pair, b_pair, w, segs, out_dtypes, n_ctx, cs=None, wgt=None):
    bsz, n, d = x.shape
    tm = TOKEN_TILE
    assert n % tm == 0
    tok = lambda b, t: (b, t, 0)
    in_specs = [
        pl.BlockSpec((1, tm, d), tok),
        pl.BlockSpec((1, 2, d), lambda b, t: (b, 0, 0)),
        pl.BlockSpec((1, 2, d), lambda b, t: (b, 0, 0)),
        _const_spec(w.shape),
    ]
    args = [x, a_pair, b_pair, w]
    if cs is not None:
        in_specs.append(pl.BlockSpec((tm, 2 * LANES), lambda b, t: (t, 0)))
        args.append(cs)
    if wgt is not None:
        in_specs.append(_const_spec(wgt.shape))
        args.append(wgt)
    out_specs = [pl.BlockSpec((1, tm, c1 - c0), tok) for (c0, c1, _, _) in segs]
    out_shape = [jax.ShapeDtypeStruct((bsz, n, c1 - c0), dt) for (c0, c1, _, _), dt in zip(segs, out_dtypes)]
    if wgt is not None:
        out_specs.append(pl.BlockSpec((1, wgt.shape[0], tm), lambda b, t: (b, 0, t)))
        out_shape.append(jax.ShapeDtypeStruct((bsz, wgt.shape[0], n), F32))
    kern = functools.partial(_proj_kernel, segs=segs, n_ctx=n_ctx, tm=tm,
                             has_rope=cs is not None, has_grow=wgt is not None)
    return pl.pallas_call(
        kern,
        grid=(bsz, n // tm),
        in_specs=in_specs,
        out_specs=out_specs,
        out_shape=out_shape,
        scratch_shapes=[pltpu.VMEM((tm, d), BF16)],
        compiler_params=_cparams(("arbitrary", "arbitrary")),
    )(*args)


def _split3(x):
    hi = x.astype(BF16)
    r1 = x - hi.astype(F32)
    mid = r1.astype(BF16)
    lo = (r1 - mid.astype(F32)).astype(BF16)
    return hi, mid, lo


_TN = (((0,), (0,)), ((), ()))
_NT = (((1,), (1,)), ((), ()))


def _expand_heads(x, width):
    nh = x.shape[0]
    x3 = jnp.concatenate(_split3(x), axis=0)
    r = lax.broadcasted_iota(jnp.int32, (3 * nh, nh * width), 0)
    c = lax.broadcasted_iota(jnp.int32, (3 * nh, nh * width), 1)
    sel = jnp.where((r & (nh - 1)) == (c >> int(math.log2(width))), 1.0, 0.0).astype(BF16)
    return lax.dot_general(x3, sel, _TN, preferred_element_type=F32)


ML_STAT_ROWS = 6 * ML_HEADS


def _mlstm_gate_kernel(gr_ref, br_ref, st_ref, *, t, chunks):
    nh = ML_HEADS
    d = pl.program_id(1)
    fwd = d == 0
    sgn = 1 - 2 * d
    row = lax.broadcasted_iota(jnp.int32, (t, t), 0)
    col = lax.broadcasted_iota(jnp.int32, (t, t), 1)
    tri_t = jnp.where((row - col) * sgn <= 0, 1.0, 0.0).astype(BF16)
    lane = lax.broadcasted_iota(jnp.int32, (nh, t), 1)
    scan_pos = jnp.where(fwd, lane, t - 1 - lane)
    for c in range(chunks):
        cols = slice(c * t, (c + 1) * t)
        gr = gr_ref[0, :, cols] + br_ref[...]
        gr = jnp.where(fwd, gr[:2 * nh], gr[2 * nh:])
        ig = gr[:nh]
        fg = _log_sigmoid(gr[nh:])
        bs = jnp.dot(jnp.concatenate(_split3(fg), axis=0), tri_t, preferred_element_type=F32)
        b = bs[:nh] + bs[nh:2 * nh] + bs[2 * nh:]
        b_end = jnp.sum(fg, axis=1, keepdims=True)
        src = ig - b
        cmax = src
        shift = 1
        while shift < t:
            moved = jnp.where(fwd, pltpu.roll(cmax, shift, 1), pltpu.roll(cmax, t - shift, 1))
            cmax = jnp.maximum(cmax, jnp.where(scan_pos >= shift, moved, -jnp.inf))
            shift *= 2
        w_end = b_end + src
        m_loc = jnp.max(w_end, axis=1, keepdims=True)
        e_end = jnp.exp(w_end - m_loc)
        st_ref[0, 0, :, cols] = jnp.concatenate(
            [src, cmax, b, e_end, jnp.broadcast_to(b_end, (nh, t)), jnp.broadcast_to(m_loc, (nh, t))], axis=0)


def _mlstm_gate_stats(grow, gate_b):
    bsz, rows, n = grow.shape
    t = ML_CHUNK
    assert t & (t - 1) == 0
    width = TOKEN_TILE * 2 if n % (TOKEN_TILE * 2) == 0 else t
    bias_row = jnp.broadcast_to(gate_b.reshape(rows, 1).astype(F32), (rows, t))
    return pl.pallas_call(
        functools.partial(_mlstm_gate_kernel, t=t, chunks=width // t),
        grid=(bsz, 2, n // width),
        in_specs=[pl.BlockSpec((1, rows, width), lambda b, d, i: (b, 0, i)),
                  pl.BlockSpec((rows, t), lambda b, d, i: (0, 0))],
        out_specs=pl.BlockSpec((1, 1, ML_STAT_ROWS, width), lambda b, d, i: (b, d, 0, i)),
        out_shape=jax.ShapeDtypeStruct((bsz, 2, ML_STAT_ROWS, n), F32),
        compiler_params=_cparams(("arbitrary", "arbitrary", "arbitrary")),
    )(grow, bias_row)


def _mlstm_kernel(qf_ref, kf_ref, vf_ref, stf_ref, qb_ref, kb_ref, vb_ref, stb_ref, of_ref, ob_ref,
                  ct_ref, m_ref, *, t):
    @pl.when(pl.program_id(1) == 0)
    def _():
        ct_ref[...] = jnp.zeros_like(ct_ref)
        m_ref[...] = jnp.zeros_like(m_ref)

    _mlstm_chunk(qf_ref, kf_ref, vf_ref, stf_ref, of_ref, ct_ref.at[0], m_ref.at[0], t=t, forward=True)
    _mlstm_chunk(qb_ref, kb_ref, vb_ref, stb_ref, ob_ref, ct_ref.at[1], m_ref.at[1], t=t, forward=False)


def _mlstm_chunk(q_ref, k_ref, v_ref, st_ref, o_ref, ct_ref, m_ref, *, t, forward):
    nh = ML_HEADS
    row = lax.broadcasted_iota(jnp.int32, (t, t), 0)
    col = lax.broadcasted_iota(jnp.int32, (t, t), 1)
    mask = (col <= row) if forward else (col >= row)
    st = st_ref[0, 0]
    src, cmax, b, e_end = (st[j * nh:(j + 1) * nh] for j in range(4))
    b_end = st[4 * nh:5 * nh, 0:1]
    m_loc = st[5 * nh:6 * nh, 0:1]

    m_in = m_ref[:, 0:1]
    big_m = jnp.maximum(m_in, cmax)
    m_new = jnp.maximum(b_end + m_in, m_loc)
    a_dec = jnp.exp(b_end + m_in - m_new)
    s_dec = jnp.exp(m_loc - m_new)

    q = q_ref[0]
    k = k_ref[0] * (ML_QK ** -0.5)
    v = v_ref[0]
    q_in = (q.astype(F32) * _expand_heads(jnp.exp(m_in - big_m), ML_QK)).astype(BF16)
    k_end = (k.astype(F32) * _expand_heads(e_end, ML_QK)).astype(BF16)
    floor = _expand_heads(jnp.exp(-(b + big_m)), ML_V)

    ones3 = jnp.ones((3 * nh, t), BF16)
    dec_l = jnp.concatenate(list(_split3(-big_m)) + [ones3], axis=0)
    dec_r = jnp.concatenate([ones3] + list(_split3(src)), axis=0)
    head_of_row = lax.broadcasted_iota(jnp.int32, (6 * nh, t), 0) & (nh - 1)
    ones = jnp.ones((t, ML_V), BF16)
    nums, dens = [], []
    for h in range(nh):
        qk = slice(h * ML_QK, (h + 1) * ML_QK)
        va = jnp.concatenate([v[:, h * ML_V:(h + 1) * ML_V], ones], axis=1)
        s = lax.dot_general(q[:, qk], k[:, qk], _NT, preferred_element_type=F32)
        w_log = lax.dot_general(dec_l, jnp.where(head_of_row == h, dec_r, jnp.zeros_like(dec_r)), _TN,
                                preferred_element_type=F32)
        w = (jnp.where(mask, jnp.exp(w_log), 0.0) * s).astype(BF16)
        ct = ct_ref[h]
        r = (jnp.dot(w, va, preferred_element_type=F32)
             + jnp.dot(q_in[:, qk], ct.astype(BF16), preferred_element_type=F32))
        nums.append(r[:, :ML_V])
        dens.append(r[:, ML_V:])
        c_loc = lax.dot_general(k_end[:, qk], va, _TN, preferred_element_type=F32)
        ct_ref[h] = a_dec[h:h + 1] * ct + s_dec[h:h + 1] * c_loc
    num = jnp.concatenate(nums, axis=1)
    den = jnp.concatenate(dens, axis=1)
    o_ref[0] = (num / jnp.maximum(jnp.abs(den), floor)).astype(o_ref.dtype)
    m_ref[...] = jnp.broadcast_to(m_new, m_ref.shape)


def _mlstm_scan(q, k, v, grow, gate_b, n_ctx):
    bsz, n, _ = q.shape
    t = ML_CHUNK
    assert t & (t - 1) == 0
    nc = n // t
    ncc = n_ctx // t

    def back(i):
        return jnp.where(i < ncc, ncc - 1 - i, nc - 1 - (i - ncc))

    stats = _mlstm_gate_stats(grow, gate_b)
    fwd = lambda b, i: (b, i, 0)
    bwd = lambda b, i: (b, back(i), 0)
    specs = lambda tok, d: [
        pl.BlockSpec((1, t, ML_QKW), tok), pl.BlockSpec((1, t, ML_QKW), tok), pl.BlockSpec((1, t, D_MODEL), tok),
        pl.BlockSpec((1, 1, ML_STAT_ROWS, t), lambda b, i: (b, d, 0, tok(b, i)[1]))]
    out = jax.ShapeDtypeStruct((bsz, n, D_MODEL), BF16)
    return pl.pallas_call(
        functools.partial(_mlstm_kernel, t=t),
        grid=(bsz, nc),
        in_specs=specs(fwd, 0) + specs(bwd, 1),
        out_specs=[pl.BlockSpec((1, t, D_MODEL), fwd), pl.BlockSpec((1, t, D_MODEL), bwd)],
        out_shape=[out, out],
        scratch_shapes=[pltpu.VMEM((2, ML_HEADS, ML_QK, 2 * ML_V), F32), pltpu.VMEM((2, ML_HEADS, LANES), F32)],
        compiler_params=_cparams(("arbitrary", "arbitrary")),
    )(q, k, v, stats, q, k, v, stats)


def _diff_kernel(lam_ref, q_ref, kt_ref, v_ref, o_ref, acc_ref, m_ref, s_ref, smax_ref, va_ref, *, tq, tk, n,
                 n_ctx, lambda_init):
    qi = pl.program_id(2)

    @pl.when(qi == 0)
    def _():
        def fill(c, carry):
            rows = pl.ds(pl.multiple_of(c * tk, tk), tk)
            va_ref[rows, :DF_VD] = v_ref[0, rows, :]
            va_ref[rows, DF_VD:] = jnp.ones((tk, DF_VD), va_ref.dtype)
            return carry
        lax.fori_loop(0, n // tk, fill, 0)

    q = q_ref[0]
    lane = lax.broadcasted_iota(jnp.int32, q.shape, 1)
    zero = jnp.zeros_like(q)
    qs = (jnp.where(lane < DF_HD, q, zero), jnp.where(lane < DF_HD, zero, q))
    acc_ref[...] = jnp.zeros_like(acc_ref)
    m_ref[...] = jnp.full_like(m_ref, NEG_INF)

    def scores(start, size, slot):
        kt = kt_ref[0, :, pl.ds(start, size)]
        for j in range(2):
            s = jnp.dot(qs[j], kt, preferred_element_type=F32)
            s_ref[slot, j, :, :size] = s
            smax_ref[slot, j] = jnp.max(s, axis=1, keepdims=True)

    def accumulate(start, size, slot):
        va = va_ref[pl.ds(start, size), :]
        ss = [s_ref[slot, j, :, :size] for j in range(2)]
        m_old = [m_ref[j] for j in range(2)]
        m_new = [jnp.maximum(m_old[j], smax_ref[slot, j]) for j in range(2)]
        ps = [jnp.exp2((ss[j] - m_new[j]).astype(BF16)) for j in range(2)]
        pv = [jnp.dot(ps[j], va, preferred_element_type=F32) for j in range(2)]
        for j in range(2):
            acc_ref[j] = jnp.exp2(m_old[j] - m_new[j]) * acc_ref[j] + pv[j]
            m_ref[j] = m_new[j]

    is_ctx_tile = qi * tq < n_ctx

    @pl.when(is_ctx_tile)
    def _():
        scores(0, n_ctx, 0)
        accumulate(0, n_ctx, 0)

    @pl.when(jnp.logical_not(is_ctx_tile))
    def _():
        n_chunks = n // tk
        unroll = ATT_UNROLL if (n_chunks - 1) % ATT_UNROLL == 0 else 2
        scores(0, tk, 0)

        def body(i, carry):
            c0 = pl.multiple_of(unroll * i * tk, tk)
            for u in range(unroll):
                scores(c0 + (u + 1) * tk, tk, (u + 1) % 2)
                accumulate(c0 + u * tk, tk, u % 2)
            return carry
        lax.fori_loop(0, (n_chunks - 1) // unroll, body, 0)
        accumulate((n_chunks - 1) * tk, tk, 0)

    lam = lam_ref[...]
    lam_full = (jnp.exp(jnp.sum(lam[0:1] * lam[1:2], axis=1, keepdims=True))
                - jnp.exp(jnp.sum(lam[2:3] * lam[3:4], axis=1, keepdims=True)) + lambda_init)
    o1 = acc_ref[0, :, :DF_VD] / acc_ref[0, :, DF_VD:]
    o2 = acc_ref[1, :, :DF_VD] / acc_ref[1, :, DF_VD:]
    o_ref[0] = (o1 - lam_full * o2).astype(o_ref.dtype)


def _diff_attention(q, k, v, lam, lambda_init, n_ctx):
    bsz, n, _ = q.shape
    tq, tk = ATT_Q_TILE, ATT_KV_TILE
    assert n_ctx % tq == 0 and n % tq == 0 and n % tk == 0 and (n // tk) % 2 == 1 and n_ctx <= tk
    kt = jnp.swapaxes(k, 1, 2)
    kern = functools.partial(_diff_kernel, tq=tq, tk=tk, n=n, n_ctx=n_ctx, lambda_init=lambda_init)
    return pl.pallas_call(
        kern,
        grid=(bsz, DF_HEADS, n // tq),
        in_specs=[
            pl.BlockSpec(lam.shape, lambda b, h, i: (0, 0)),
            pl.BlockSpec((1, tq, 2 * DF_HD), lambda b, h, i: (b, i, h)),
            pl.BlockSpec((1, 2 * DF_HD, n), lambda b, h, i: (b, h, 0)),
            pl.BlockSpec((1, n, DF_VD), lambda b, h, i: (b, 0, h)),
        ],
        out_specs=pl.BlockSpec((1, tq, DF_VD), lambda b, h, i: (b, i, h)),
        out_shape=jax.ShapeDtypeStruct((bsz, n, D_MODEL), BF16),
        scratch_shapes=[pltpu.VMEM((2, tq, 2 * DF_VD), F32), pltpu.VMEM((2, tq, 1), F32),
                        pltpu.VMEM((2, 2, tq, tk), F32), pltpu.VMEM((2, 2, tq, 1), F32),
                        pltpu.VMEM((n, 2 * DF_VD), v.dtype)],
        compiler_params=_cparams(("arbitrary", "arbitrary", "arbitrary")),
    )(lam.astype(F32), q, kt, v)


def _swa_mask_table(tq, bw):
    t = jnp.arange(tq, dtype=jnp.int32)[:, None]
    s = jnp.arange(bw, dtype=jnp.int32)[None, :]

    def one(off, first_ok):
        ok = jnp.logical_and(s >= first_ok, jnp.abs(t - s + off) <= SW_WIN)
        return jnp.where(ok, 0.0, NEG_INF).astype(F32)
    return jnp.stack([one(SW_WIN, 0), one(SW_WIN, SW_WIN), one(2 * SW_WIN, 0), one(0, bw)])


def _swa_kernel(sink_ref, q_ref, kt_ref, va_ref, mask_ref, o_ref, *, tq, n, n_ctx):
    g = pl.program_id(1)
    i = pl.program_id(2)
    bw = tq + 2 * SW_WIN
    q = q_ref[0]
    qh = jnp.concatenate([q[:, u * SW_HD:(u + 1) * SW_HD] for u in range(SW_GROUP)], axis=0)
    start = pl.multiple_of(jnp.clip(i * tq - SW_WIN, 0, n - bw), SW_WIN)
    s_c = jnp.dot(qh, kt_ref[0, :, 0:n_ctx], preferred_element_type=F32)
    s_b = jnp.dot(qh, kt_ref[0, :, pl.ds(start, bw)], preferred_element_type=F32)
    case = jnp.where(i * tq < n_ctx, 3, jnp.where(i * tq - SW_WIN < n_ctx, 1,
                                                   jnp.where(i * tq - SW_WIN > n - bw, 2, 0)))
    s_b = s_b + jnp.tile(mask_ref[case], (SW_GROUP, 1))
    head = lax.broadcasted_iota(jnp.int32, (SW_GROUP * tq, 1), 0) // tq
    sink = jnp.zeros((SW_GROUP * tq, 1), F32)
    for u in range(SW_GROUP):
        sink = jnp.where(head == u, sink_ref[g * SW_GROUP + u] * LOG2E, sink)
    m = jnp.maximum(sink, jnp.maximum(jnp.max(s_c, axis=1, keepdims=True), jnp.max(s_b, axis=1, keepdims=True)))
    acc = (jnp.dot(jnp.exp2((s_c - m).astype(BF16)), va_ref[0, 0:n_ctx, :], preferred_element_type=F32)
           + jnp.dot(jnp.exp2((s_b - m).astype(BF16)), va_ref[0, pl.ds(start, bw), :],
                     preferred_element_type=F32))
    out = acc[:, :SW_HD] / (acc[:, SW_HD:] + jnp.exp2(sink - m))
    o_ref[0] = jnp.concatenate([out[u * tq:(u + 1) * tq] for u in range(SW_GROUP)], axis=1).astype(o_ref.dtype)


def _swa_attention(q, k, v, sinks, n_ctx):
    bsz, n, _ = q.shape
    tq = SWA_Q_TILE
    assert tq & (tq - 1) == 0 and n_ctx % tq == 0 and n % tq == 0 and tq % SW_WIN == 0
    bw = tq + 2 * SW_WIN
    assert n - bw >= n_ctx + tq
    gw = SW_GROUP * SW_HD
    kt = jnp.swapaxes(k, 1, 2)
    va = jnp.concatenate([v.reshape(bsz, n, SW_KV, SW_HD),
                          jnp.ones((bsz, n, SW_KV, SW_HD), v.dtype)], axis=-1).reshape(bsz, n, SW_KV * 2 * SW_HD)
    return pl.pallas_call(
        functools.partial(_swa_kernel, tq=tq, n=n, n_ctx=n_ctx),
        grid=(bsz, SW_KV, n // tq),
        in_specs=[
            pl.BlockSpec(memory_space=pltpu.SMEM),
            pl.BlockSpec((1, tq, gw), lambda b, g, i: (b, i, g)),
            pl.BlockSpec((1, SW_HD, n), lambda b, g, i: (b, g, 0)),
            pl.BlockSpec((1, n, 2 * SW_HD), lambda b, g, i: (b, 0, g)),
            _const_spec((4, tq, bw)),
        ],
        out_specs=pl.BlockSpec((1, tq, gw), lambda b, g, i: (b, i, g)),
        out_shape=jax.ShapeDtypeStruct((bsz, n, D_MODEL), BF16),
        compiler_params=_cparams(("arbitrary", "arbitrary", "arbitrary")),
    )(sinks.astype(F32), q, kt, va, _swa_mask_table(tq, bw))


def _head_norm(y, g, width):
    parts = []
    for h in range(y.shape[1] // width):
        yh = y[:, h * width:(h + 1) * width]
        parts.append(yh * lax.rsqrt(jnp.mean(yh * yh, axis=-1, keepdims=True) + EPS))
    return jnp.concatenate(parts, axis=1) * g


def _mixer_update(it, mode, n_ctx, tm, post_scale):
    x_ref, gate_ref, w_ref = next(it), next(it), next(it)
    if mode == "mlstm":
        hf_ref, hb_ref, og_ref, g_ref = next(it), next(it), next(it), next(it)
        y = _head_norm(hf_ref[0].astype(F32) + hb_ref[0].astype(F32), g_ref[...], ML_V)
        z = y * _sigmoid(og_ref[0].astype(F32))
    elif mode == "diff":
        a_ref, g_ref = next(it), next(it)
        z = _head_norm(a_ref[0].astype(F32), g_ref[...], DF_VD) * post_scale
    else:
        a_ref = next(it)
        z = a_ref[0]
    y = jnp.dot(z.astype(BF16), w_ref[...], preferred_element_type=F32)
    gate = _row_select(gate_ref[0], pl.program_id(1) * tm, tm, n_ctx)
    return x_ref[0] + gate * y


def _mixer_operands(x, mix, tm):
    mode, gate_pair, w_out, extra, _ = mix
    d = x.shape[2]
    tok = lambda b, t: (b, t, 0)
    in_specs = [pl.BlockSpec((1, tm, d), tok), pl.BlockSpec((1, 2, d), lambda b, t: (b, 0, 0)),
                _const_spec(w_out.shape)]
    args = [x, gate_pair, w_out]
    if mode == "mlstm":
        hf, hb, og, g = extra
        in_specs += [pl.BlockSpec((1, tm, d), tok), pl.BlockSpec((1, tm, d), tok), pl.BlockSpec((1, tm, d), tok),
                     pl.BlockSpec((1, d), lambda b, t: (0, 0))]
        args += [hf, hb, og, g.reshape(1, d)]
    elif mode == "diff":
        a, g = extra
        in_specs += [pl.BlockSpec((1, tm, d), tok), pl.BlockSpec((1, d), lambda b, t: (0, 0))]
        args += [a, g.reshape(1, d)]
    else:
        (a,) = extra
        in_specs += [pl.BlockSpec((1, tm, d), tok)]
        args += [a]
    return in_specs, args


def _dense_ffn_kernel(*refs, mix_mode, post_scale, n_ctx, tm, bounds):
    it = iter(refs)
    x = _mixer_update(it, mix_mode, n_ctx, tm, post_scale)
    a_ref, b_ref, gate_ref, wa_ref, wb_ref, wo_ref, o_ref, h_ref, g_ref = (next(it) for _ in range(9))
    row0 = pl.program_id(1) * tm
    h_ref[...] = _norm_mod(x, a_ref[0], b_ref[0], row0, n_ctx).astype(BF16)
    for c0, c1 in zip(bounds[:-1], bounds[1:]):
        a = jnp.dot(h_ref[...], wa_ref[:, c0:c1], preferred_element_type=F32)
        b = jnp.dot(h_ref[...], wb_ref[:, c0:c1], preferred_element_type=F32)
        g_ref[:, c0:c1] = (a * _sigmoid(a) * b).astype(BF16)
    y = jnp.dot(g_ref[...], wo_ref[...], preferred_element_type=F32)
    o_ref[0] = x + _row_select(gate_ref[0], row0, tm, n_ctx) * y


def _dense_ffn(x, mix, a_pair, b_pair, gate_pair, wa, wb, wo, n_ctx):
    bsz, n, d = x.shape
    tm = TOKEN_TILE
    mix_specs, mix_args = _mixer_operands(x, mix, tm)
    ff = wa.shape[1]
    half = (pl.cdiv(ff // MXU_WIDTH, 2) * MXU_WIDTH) if ff % MXU_WIDTH == 0 else ff
    bounds = (0, half, ff) if half < ff else (0, ff)
    tok = lambda b, t: (b, t, 0)
    pair = pl.BlockSpec((1, 2, d), lambda b, t: (b, 0, 0))
    kern = functools.partial(_dense_ffn_kernel, mix_mode=mix[0], post_scale=mix[4], n_ctx=n_ctx, tm=tm,
                             bounds=bounds)
    return pl.pallas_call(
        kern,
        grid=(bsz, n // tm),
        in_specs=mix_specs + [pair, pair, pair,
                              _const_spec(wa.shape), _const_spec(wb.shape), _const_spec(wo.shape)],
        out_specs=pl.BlockSpec((1, tm, d), tok),
        out_shape=jax.ShapeDtypeStruct((bsz, n, d), F32),
        scratch_shapes=[pltpu.VMEM((tm, d), BF16), pltpu.VMEM((tm, ff), BF16)],
        compiler_params=_cparams(("arbitrary", "arbitrary")),
    )(*mix_args, a_pair, b_pair, gate_pair, wa, wb, wo)


def _moe_route_kernel(*refs, mix_mode, post_scale, n_ctx, tm):
    it = iter(refs)
    x = _mixer_update(it, mix_mode, n_ctx, tm, post_scale)
    a_ref, b_ref, r_ref, x_ref, h_ref, g_ref, i_ref = (next(it) for _ in range(7))
    row0 = pl.program_id(1) * tm
    x_ref[0] = x
    h = _norm_mod(x, a_ref[0], b_ref[0], row0, n_ctx)
    h_ref[0] = h.astype(BF16)
    logits = lax.dot_general(r_ref[...], h, (((1,), (1,)), ((), ())), preferred_element_type=F32,
                             precision=lax.Precision.HIGHEST)
    e_idx = lax.broadcasted_iota(jnp.int32, logits.shape, 0)
    v1 = jnp.max(logits, axis=0, keepdims=True)
    i1 = jnp.min(jnp.where(logits == v1, e_idx, N_EXPERTS), axis=0, keepdims=True)
    rest = jnp.where(e_idx == i1, -jnp.inf, logits)
    v2 = jnp.max(rest, axis=0, keepdims=True)
    i2 = jnp.min(jnp.where(rest == v2, e_idx, N_EXPERTS), axis=0, keepdims=True)
    p2 = jnp.exp(v2 - v1)
    den = 1.0 + p2
    g_ref[0] = jnp.concatenate([1.0 / den, p2 / den], axis=0)
    i_ref[0] = jnp.concatenate([i1, i2], axis=0)


def _moe_route(x, mix, a_pair, b_pair, router_t, n_ctx):
    bsz, n, d = x.shape
    tm = TOKEN_TILE
    tok = lambda b, t: (b, t, 0)
    pair = pl.BlockSpec((1, 2, d), lambda b, t: (b, 0, 0))
    mix_specs, mix_args = _mixer_operands(x, mix, tm)
    return pl.pallas_call(
        functools.partial(_moe_route_kernel, mix_mode=mix[0], post_scale=mix[4], n_ctx=n_ctx, tm=tm),
        grid=(bsz, n // tm),
        in_specs=mix_specs + [pair, pair, _const_spec(router_t.shape)],
        out_specs=[pl.BlockSpec((1, tm, d), tok),
                   pl.BlockSpec((1, tm, d), tok),
                   pl.BlockSpec((1, TOP_K, tm), lambda b, t: (b, 0, t)),
                   pl.BlockSpec((1, TOP_K, tm), lambda b, t: (b, 0, t))],
        out_shape=[jax.ShapeDtypeStruct((bsz, n, d), F32),
                   jax.ShapeDtypeStruct((bsz, n, d), BF16),
                   jax.ShapeDtypeStruct((bsz, TOP_K, n), F32),
                   jax.ShapeDtypeStruct((bsz, TOP_K, n), jnp.int32)],
        compiler_params=_cparams(("arbitrary", "arbitrary")),
    )(*mix_args, a_pair, b_pair, router_t)


def _moe_expert_kernel(te_ref, nu_ref, x_ref, wi_ref, wo_ref, o_ref, g_ref, *, ff, chunk):
    i = pl.program_id(0)

    @pl.when(i < nu_ref[0])
    def _():
        x = x_ref[...]
        for c0 in range(0, ff, chunk):
            a = jnp.dot(x, wi_ref[0, :, c0:c0 + chunk], preferred_element_type=F32)
            b = jnp.dot(x, wi_ref[0, :, ff + c0:ff + c0 + chunk], preferred_element_type=F32)
            g_ref[:, c0:c0 + chunk] = (a * _sigmoid(a) * b).astype(BF16)
        o_ref[...] = jnp.dot(g_ref[...], wo_ref[0], preferred_element_type=F32).astype(o_ref.dtype)

    @pl.when(i >= nu_ref[0])
    def _():
        o_ref[...] = jnp.zeros_like(o_ref)


def _moe_experts(xs, w_in, w_out, tile_expert, n_used):
    p, d = xs.shape
    tm = MOE_TILE
    ff = w_out.shape[1]
    chunk = MOE_FF_CHUNK
    assert ff % chunk == 0 and chunk % LANES == 0
    nt = p // tm

    def teff(i, nu):
        return jnp.minimum(i, jnp.maximum(nu[0] - 1, 0))

    grid_spec = pltpu.PrefetchScalarGridSpec(
        num_scalar_prefetch=2,
        grid=(nt,),
        in_specs=[
            pl.BlockSpec((tm, d), lambda i, te, nu: (teff(i, nu), 0)),
            pl.BlockSpec((1, d, 2 * ff), lambda i, te, nu: (te[teff(i, nu)], 0, 0),
                         pipeline_mode=pl.Buffered(1)),
            pl.BlockSpec((1, ff, d), lambda i, te, nu: (te[teff(i, nu)], 0, 0),
                         pipeline_mode=pl.Buffered(1)),
        ],
        out_specs=pl.BlockSpec((tm, d), lambda i, te, nu: (i, 0)),
        scratch_shapes=[pltpu.VMEM((tm, ff), BF16)],
    )
    return pl.pallas_call(
        functools.partial(_moe_expert_kernel, ff=ff, chunk=chunk),
        grid_spec=grid_spec,
        out_shape=jax.ShapeDtypeStruct((p, d), BF16),
        compiler_params=_cparams(("arbitrary",)),
    )(tile_expert, n_used, xs, w_in, w_out)


def _moe_combine_kernel(*refs, n_ctx, tm, final):
    it = iter(refs)
    x_ref, gate_ref, y0_ref, y1_ref, pk_ref = next(it), next(it), next(it), next(it), next(it)
    fg_ref = next(it) if final else None
    o_ref = next(it)
    pk = pk_ref[0]
    y = pk[:, 0:1] * y0_ref[0].astype(F32) + pk[:, 1:2] * y1_ref[0].astype(F32)
    x = x_ref[0] + _row_select(gate_ref[0], pl.program_id(1) * tm, tm, n_ctx) * y
    if final:
        x = x * lax.rsqrt(jnp.mean(x * x, axis=-1, keepdims=True) + EPS) * fg_ref[...]
    o_ref[0] = x


def _moe_combine(x, gate_pair, y0, y1, slot_w, n_ctx, final_g=None):
    bsz, n, d = x.shape
    tm = TOKEN_TILE
    tok = lambda b, t: (b, t, 0)
    in_specs = [pl.BlockSpec((1, tm, d), tok), pl.BlockSpec((1, 2, d), lambda b, t: (b, 0, 0)),
                pl.BlockSpec((1, tm, d), tok), pl.BlockSpec((1, tm, d), tok),
                pl.BlockSpec((1, tm, TOP_K), tok)]
    args = [x, gate_pair, y0, y1, slot_w]
    if final_g is not None:
        in_specs.append(pl.BlockSpec((1, d), lambda b, t: (0, 0)))
        args.append(final_g.reshape(1, d))
    return pl.pallas_call(
        functools.partial(_moe_combine_kernel, n_ctx=n_ctx, tm=tm, final=final_g is not None),
        grid=(bsz, n // tm),
        in_specs=in_specs,
        out_specs=pl.BlockSpec((1, tm, d), tok),
        out_shape=jax.ShapeDtypeStruct((bsz, n, d), F32),
        compiler_params=_cparams(("arbitrary", "arbitrary")),
    )(*args)


def _moe_layer(x, mix, a_pair, b_pair, gate_pair, router, w_in_all, w_out_all, layer, n_ctx, final_g):
    bsz, n, d = x.shape
    tm = MOE_TILE
    ntok = bsz * n
    x, h2, gates, idx = _moe_route(x, mix, a_pair, b_pair, router.T, n_ctx)
    w_in_all, w_out_all, h2 = lax.optimization_barrier((w_in_all, w_out_all, h2))
    w_in = w_in_all[layer].astype(BF16)
    w_out = w_out_all[layer].astype(BF16)
    e_flat = jnp.moveaxis(idx, 1, 0).reshape(TOP_K * ntok)
    onehot = (e_flat[:, None] == jnp.arange(N_EXPERTS, dtype=jnp.int32)[None, :]).astype(jnp.int32)
    csum = jnp.cumsum(onehot, axis=0)
    counts = csum[-1]
    rank = jnp.sum(onehot * csum, axis=1) - 1
    padded = ((counts + tm - 1) // tm) * tm
    ends = jnp.cumsum(padded)
    starts = ends - padded
    dest = jnp.sum(onehot * starts[None, :], axis=1) + rank
    n_rows = TOP_K * ntok + N_EXPERTS * tm
    n_rows = ((n_rows + tm - 1) // tm) * tm
    tile_start = jnp.arange(n_rows // tm, dtype=jnp.int32) * tm
    tile_expert = jnp.minimum(jnp.sum((tile_start[:, None] >= ends[None, :]).astype(jnp.int32), axis=1),
                              N_EXPERTS - 1).astype(jnp.int32)
    n_used = (ends[-1] // tm).astype(jnp.int32).reshape(1)
    order = jnp.argsort(e_flat, stable=True).astype(jnp.int32)
    local = (tile_start - starts[tile_expert])[:, None] + jnp.arange(tm, dtype=jnp.int32)[None, :]
    pos = (jnp.cumsum(counts) - counts)[tile_expert][:, None] + local
    filler = jnp.arange(n_rows, dtype=jnp.int32).reshape(-1, tm) % (TOP_K * ntok)
    pos = jnp.where(local < counts[tile_expert][:, None], pos, filler).reshape(n_rows)
    src = order.at[pos].get(mode="promise_in_bounds") % ntok
    rows = lambda a, i: a.at[i].get(mode="promise_in_bounds")
    xs = rows(h2.reshape(ntok, d), src)
    ys = _moe_experts(xs, w_in, w_out, tile_expert, n_used)
    y0 = rows(ys, dest[:ntok]).reshape(bsz, n, d)
    y1 = rows(ys, dest[ntok:]).reshape(bsz, n, d)
    return _moe_combine(x, gate_pair, y0, y1, jnp.swapaxes(gates, 1, 2), n_ctx, final_g)


def _rope_table(n_ctx, seq):
    pos = jnp.arange(seq)
    row = (pos // GRID_W).astype(F32)
    col = (pos % GRID_W).astype(F32)
    nf = DF_HD // 4
    inv = jnp.power(ROPE_BASE, -jnp.arange(nf, dtype=F32) / nf)
    ang = jnp.concatenate([row[:, None] * inv, col[:, None] * inv], axis=-1)
    cos, sin = jnp.cos(ang), jnp.sin(ang)
    cos = jnp.concatenate([jnp.ones((n_ctx, 2 * nf), F32), cos], axis=0)
    sin = jnp.concatenate([jnp.zeros((n_ctx, 2 * nf), F32), sin], axis=0)
    c128 = jnp.tile(jnp.concatenate([cos, cos], axis=1), (1, 2))
    s128 = jnp.tile(jnp.concatenate([-sin, sin], axis=1), (1, 2))
    return jnp.concatenate([c128, s128], axis=1)


def _pair(v, bsz):
    return jnp.stack([v[:bsz], jnp.broadcast_to(v[bsz], (bsz, v.shape[1]))], axis=1)


def kernel(x, c, ctx, c_ctx, ada_w, ada_b, norm1, norm2, ml_w_in, ml_gate_b, ml_hnorm, ml_w_out,
           df_w_in, df_lam, df_hnorm, df_w_out, sw_w_in, sw_sinks, sw_w_out,
           ffn_w_in, ffn_w_out, moe_router, moe_w_in, moe_w_out, final_norm):
    bsz, seq, d = x.shape
    n_ctx = ctx.shape[1]
    depth = ada_w.shape[0]
    assert d == D_MODEL and n_ctx % ATT_Q_TILE == 0 and (n_ctx + seq) % TOKEN_TILE == 0

    xs = jnp.concatenate([ctx, x], axis=1)
    cond_rows = SUBLANES * ((bsz + 1 + SUBLANES - 1) // SUBLANES)
    cond = jnp.zeros((cond_rows, d), F32).at[:bsz].set(c).at[bsz].set(c_ctx)
    mods = _ada_mod(cond, ada_w, ada_b)
    cs = _rope_table(n_ctx, seq)

    for i in range(depth):
        m = [mods[i, :, k * d:(k + 1) * d] for k in range(6)]
        a1 = _pair(norm1[i][None, :] * (1.0 + m[1]), bsz)
        b1 = _pair(m[0], bsz)
        g1 = _pair(m[2], bsz)
        a2 = _pair(norm2[i][None, :] * (1.0 + m[4]), bsz)
        b2 = _pair(m[3], bsz)
        g2 = _pair(m[5], bsz)
        kind, j = i % N_MIXERS, i // N_MIXERS
        if kind == 0:
            w = ml_w_in[j]
            nmain = 2 * ML_QKW + 2 * D_MODEL
            w_main = w[:, :nmain].astype(BF16)
            wgt = w[:, nmain:].T.astype(BF16)
            segs = ((0, ML_QKW, False, 1.0), (ML_QKW, 2 * ML_QKW, False, 1.0),
                    (2 * ML_QKW, 2 * ML_QKW + d, False, 1.0), (2 * ML_QKW + d, nmain, False, 1.0))
            q, k, v, og, grow = _project(xs, a1, b1, w_main, segs, (BF16, BF16, BF16, BF16), n_ctx, wgt=wgt)
            hf, hb = _mlstm_scan(q, k, v, grow, ml_gate_b[j], n_ctx)
            mix = ("mlstm", g1, ml_w_out[j].astype(BF16), (hf, hb, og, ml_hnorm[j]), 1.0)
        elif kind == 1:
            lambda_init = 0.8 - 0.6 * math.exp(-0.3 * i)
            segs = ((0, DF_QKW, True, DF_HD ** -0.5 * LOG2E), (DF_QKW, 2 * DF_QKW, True, 1.0),
                    (2 * DF_QKW, 2 * DF_QKW + d, False, 1.0))
            q, k, v = _project(xs, a1, b1, df_w_in[j].astype(BF16), segs, (BF16, BF16, BF16), n_ctx, cs=cs)
            att = _diff_attention(q, k, v, df_lam[j], lambda_init, n_ctx)
            mix = ("diff", g1, df_w_out[j].astype(BF16), (att, df_hnorm[j]), 1.0 - lambda_init)
        else:
            nq = SW_HEADS * SW_HD
            nkv = SW_KV * SW_HD
            segs = ((0, nq, True, SW_HD ** -0.5 * LOG2E), (nq, nq + nkv, True, 1.0), (nq + nkv, nq + 2 * nkv, False, 1.0))
            q, k, v = _project(xs, a1, b1, sw_w_in[j].astype(BF16), segs, (BF16, BF16, BF16), n_ctx, cs=cs)
            att = _swa_attention(q, k, v, sw_sinks[j], n_ctx)
            mix = ("plain", g1, sw_w_out[j].astype(BF16), (att,), 1.0)
        jf = i // 2
        last = i == depth - 1
        if i % 2 == 0:
            ff = ffn_w_out.shape[1]
            wi = ffn_w_in[jf].astype(BF16)
            xs = _dense_ffn(xs, mix, a2, b2, g2, wi[:, :ff], wi[:, ff:], ffn_w_out[jf].astype(BF16), n_ctx)
            if last:
                raise NotImplementedError("final norm is fused into the MoE combine")
        else:
            xs = _moe_layer(xs, mix, a2, b2, g2, moe_router[jf], moe_w_in, moe_w_out, jf, n_ctx,
                            final_norm if last else None)
    return xs[:, n_ctx:]
```

```python
import functools
import math

import jax
import jax.numpy as jnp
from jax import lax
from jax.experimental import pallas as pl
from jax.experimental.pallas import tpu as pltpu

F32 = jnp.float32
BF16 = jnp.bfloat16

D_MODEL = 1024
GRID_W = 64
EPS = 1e-6
NEG_INF = -1e30
ROPE_BASE = 10000.0
LOG2E = math.log2(math.e)
N_MIXERS = 3

ML_HEADS = 8
ML_V = D_MODEL // ML_HEADS
ML_QK = ML_V // 2
ML_QKW = ML_HEADS * ML_QK
ML_CHUNK = 256

DF_HD = 64
DF_HEADS = D_MODEL // (2 * DF_HD)
DF_VD = 2 * DF_HD
DF_QKW = 2 * DF_HEADS * DF_HD

SW_HD = 64
SW_HEADS = D_MODEL // SW_HD
SW_KV = 4
SW_GROUP = SW_HEADS // SW_KV
SW_WIN = 128

N_EXPERTS = 8
TOP_K = 2

LANES = 128
SUBLANES = 8
MXU_WIDTH = 256
V7X_VMEM_BYTES = 64 * 1024 * 1024
VMEM_LIMIT = V7X_VMEM_BYTES - 8 * 1024 * 1024

TOKEN_TILE = 640
ATT_Q_TILE = 256
ATT_KV_TILES = (3328, 1280, 256)
ATT_UNROLL = 4
SWA_Q_TILE = 256
MOE_TILE = 512
MOE_FF_CHUNK = 1792


def _cparams(semantics):
    return pltpu.CompilerParams(dimension_semantics=semantics, vmem_limit_bytes=VMEM_LIMIT)


def _const_spec(shape):
    nd = len(shape)
    return pl.BlockSpec(shape, lambda *_: (0,) * nd, pipeline_mode=pl.Buffered(1))


def _sigmoid(x):
    return 1.0 / (1.0 + jnp.exp(-x))


def _log_sigmoid(x):
    return jnp.minimum(x, 0.0) - jnp.log(1.0 + jnp.exp(-jnp.abs(x)))


def _norm_mod(x, a_rows, b_rows, row0, n_ctx):
    xn = x * lax.rsqrt(jnp.mean(x * x, axis=-1, keepdims=True) + EPS)
    is_ctx = (row0 + lax.broadcasted_iota(jnp.int32, (x.shape[0], 1), 0)) < n_ctx
    a = jnp.where(is_ctx, a_rows[1:2], a_rows[0:1])
    b = jnp.where(is_ctx, b_rows[1:2], b_rows[0:1])
    return xn * a + b


def _row_select(rows, row0, n_rows, n_ctx):
    is_ctx = (row0 + lax.broadcasted_iota(jnp.int32, (n_rows, 1), 0)) < n_ctx
    return jnp.where(is_ctx, rows[1:2], rows[0:1])


def _ada_kernel(c_ref, w_ref, b_ref, o_ref):
    c = c_ref[...]
    s = c * _sigmoid(c)
    o_ref[0] = jnp.dot(s, w_ref[0], preferred_element_type=F32,
                       precision=lax.Precision.HIGHEST) + b_ref[0]


def _ada_mod(cond, ada_w, ada_b):
    depth, d, six_d = ada_w.shape
    rows = cond.shape[0]
    col = D_MODEL
    return pl.pallas_call(
        _ada_kernel,
        grid=(depth, six_d // col),
        in_specs=[
            pl.BlockSpec((rows, d), lambda i, j: (0, 0)),
            pl.BlockSpec((1, d, col), lambda i, j: (i, 0, j)),
            pl.BlockSpec((1, 1, col), lambda i, j: (i, 0, j)),
        ],
        out_specs=pl.BlockSpec((1, rows, col), lambda i, j: (i, 0, j)),
        out_shape=jax.ShapeDtypeStruct((depth, rows, six_d), F32),
        compiler_params=_cparams(("arbitrary", "arbitrary")),
    )(cond, ada_w, ada_b.reshape(depth, 1, six_d))


def _rope(r, cos, sin):
    w = r.shape[1]
    lane = lax.broadcasted_iota(jnp.int32, r.shape, 1)
    swapped = jnp.where((lane & 32) == 0, pltpu.roll(r, w - 32, 1), pltpu.roll(r, 32, 1))
    reps = w // LANES
    return r * jnp.tile(cos, (1, reps)) + swapped * jnp.tile(sin, (1, reps))


def _proj_kernel(*refs, segs, n_ctx, tm, has_rope, has_grow):
    it = iter(refs)
    x_ref, a_ref, b_ref, w_ref = next(it), next(it), next(it), next(it)
    cs_ref = next(it) if has_rope else None
    wgt_ref = next(it) if has_grow else None
    outs = [next(it) for _ in segs]
    grow_ref = next(it) if has_grow else None
    h_ref = next(it)

    row0 = pl.program_id(1) * tm
    h_ref[...] = _norm_mod(x_ref[0], a_ref[0], b_ref[0], row0, n_ctx).astype(BF16)
    for (c0, c1, rope, scale), o_ref in zip(segs, outs):
        r = jnp.dot(h_ref[...], w_ref[:, c0:c1], preferred_element_type=F32)
        if rope:
            r = _rope(r, cs_ref[:, :LANES], cs_ref[:, LANES:])
        if scale != 1.0:
            r = r * scale
        o_ref[0] = r.astype(o_ref.dtype)
    if has_grow:
        grow_ref[0] = lax.dot_general(wgt_ref[...], h_ref[...], (((1,), (1,)), ((), ())),
                                      preferred_element_type=F32)


def _project(x, a_pair, b_pair, w, segs, out_dtypes, n_ctx, cs=None, wgt=None):
    bsz, n, d = x.shape
    tm = TOKEN_TILE
    assert n % tm == 0
    tok = lambda b, t: (b, t, 0)
    in_specs = [
        pl.BlockSpec((1, tm, d), tok),
        pl.BlockSpec((1, 2, d), lambda b, t: (b, 0, 0)),
        pl.BlockSpec((1, 2, d), lambda b, t: (b, 0, 0)),
        _const_spec(w.shape),
    ]
    args = [x, a_pair, b_pair, w]
    if cs is not None:
        in_specs.append(pl.BlockSpec((tm, 2 * LANES), lambda b, t: (t, 0)))
        args.append(cs)
    if wgt is not None:
        in_specs.append(_const_spec(wgt.shape))
        args.append(wgt)
    out_specs = [pl.BlockSpec((1, tm, c1 - c0), tok) for (c0, c1, _, _) in segs]
    out_shape = [jax.ShapeDtypeStruct((bsz, n, c1 - c0), dt) for (c0, c1, _, _), dt in zip(segs, out_dtypes)]
    if wgt is not None:
        out_specs.append(pl.BlockSpec((1, wgt.shape[0], tm), lambda b, t: (b, 0, t)))
        out_shape.append(jax.ShapeDtypeStruct((bsz, wgt.shape[0], n), F32))
    kern = functools.partial(_proj_kernel, segs=segs, n_ctx=n_ctx, tm=tm,
                             has_rope=cs is not None, has_grow=wgt is not None)
    return pl.pallas_call(
        kern,
        grid=(bsz, n // tm),
        in_specs=in_specs,
        out_specs=out_specs,
        out_shape=out_shape,
        scratch_shapes=[pltpu.VMEM((tm, d), BF16)],
        compiler_params=_cparams(("arbitrary", "arbitrary")),
    )(*args)


def _split3(x):
    hi = x.astype(BF16)
    r1 = x - hi.astype(F32)
    mid = r1.astype(BF16)
    lo = (r1 - mid.astype(F32)).astype(BF16)
    return hi, mid, lo


_TN = (((0,), (0,)), ((), ()))
_NT = (((1,), (1,)), ((), ()))


def _expand_heads(x, width):
    nh = x.shape[0]
    x3 = jnp.concatenate(_split3(x), axis=0)
    r = lax.broadcasted_iota(jnp.int32, (3 * nh, nh * width), 0)
    c = lax.broadcasted_iota(jnp.int32, (3 * nh, nh * width), 1)
    sel = jnp.where((r & (nh - 1)) == (c >> int(math.log2(width))), 1.0, 0.0).astype(BF16)
    return lax.dot_general(x3, sel, _TN, preferred_element_type=F32)


ML_STAT_ROWS = 6 * ML_HEADS


def _mlstm_gate_kernel(gr_ref, br_ref, st_ref, *, t, chunks):
    nh = ML_HEADS
    d = pl.program_id(1)
    fwd = d == 0
    sgn = 1 - 2 * d
    row = lax.broadcasted_iota(jnp.int32, (t, t), 0)
    col = lax.broadcasted_iota(jnp.int32, (t, t), 1)
    tri_t = jnp.where((row - col) * sgn <= 0, 1.0, 0.0).astype(BF16)
    lane = lax.broadcasted_iota(jnp.int32, (nh, t), 1)
    scan_pos = jnp.where(fwd, lane, t - 1 - lane)
    for c in range(chunks):
        cols = slice(c * t, (c + 1) * t)
        gr = gr_ref[0, :, cols] + br_ref[...]
        gr = jnp.where(fwd, gr[:2 * nh], gr[2 * nh:])
        ig = gr[:nh]
        fg = _log_sigmoid(gr[nh:])
        bs = jnp.dot(jnp.concatenate(_split3(fg), axis=0), tri_t, preferred_element_type=F32)
        b = bs[:nh] + bs[nh:2 * nh] + bs[2 * nh:]
        b_end = jnp.sum(fg, axis=1, keepdims=True)
        src = ig - b
        cmax = src
        shift = 1
        while shift < t:
            moved = jnp.where(fwd, pltpu.roll(cmax, shift, 1), pltpu.roll(cmax, t - shift, 1))
            cmax = jnp.maximum(cmax, jnp.where(scan_pos >= shift, moved, -jnp.inf))
            shift *= 2
        w_end = b_end + src
        m_loc = jnp.max(w_end, axis=1, keepdims=True)
        e_end = jnp.exp(w_end - m_loc)
        st_ref[0, 0, :, cols] = jnp.concatenate(
            [src, cmax, b, e_end, jnp.broadcast_to(b_end, (nh, t)), jnp.broadcast_to(m_loc, (nh, t))], axis=0)


def _mlstm_gate_stats(grow, gate_b):
    bsz, rows, n = grow.shape
    t = ML_CHUNK
    assert t & (t - 1) == 0
    width = TOKEN_TILE * 2 if n % (TOKEN_TILE * 2) == 0 else t
    bias_row = jnp.broadcast_to(gate_b.reshape(rows, 1).astype(F32), (rows, t))
    return pl.pallas_call(
        functools.partial(_mlstm_gate_kernel, t=t, chunks=width // t),
        grid=(bsz, 2, n // width),
        in_specs=[pl.BlockSpec((1, rows, width), lambda b, d, i: (b, 0, i)),
                  pl.BlockSpec((rows, t), lambda b, d, i: (0, 0))],
        out_specs=pl.BlockSpec((1, 1, ML_STAT_ROWS, width), lambda b, d, i: (b, d, 0, i)),
        out_shape=jax.ShapeDtypeStruct((bsz, 2, ML_STAT_ROWS, n), F32),
        compiler_params=_cparams(("arbitrary", "arbitrary", "arbitrary")),
    )(grow, bias_row)


def _mlstm_kernel(qf_ref, kf_ref, vf_ref, stf_ref, qb_ref, kb_ref, vb_ref, stb_ref, of_ref, ob_ref,
                  ct_ref, m_ref, *, t):
    @pl.when(pl.program_id(1) == 0)
    def _():
        ct_ref[...] = jnp.zeros_like(ct_ref)
        m_ref[...] = jnp.zeros_like(m_ref)

    _mlstm_chunk(qf_ref, kf_ref, vf_ref, stf_ref, of_ref, ct_ref.at[0], m_ref.at[0], t=t, forward=True)
    _mlstm_chunk(qb_ref, kb_ref, vb_ref, stb_ref, ob_ref, ct_ref.at[1], m_ref.at[1], t=t, forward=False)


def _mlstm_chunk(q_ref, k_ref, v_ref, st_ref, o_ref, ct_ref, m_ref, *, t, forward):
    nh = ML_HEADS
    row = lax.broadcasted_iota(jnp.int32, (t, t), 0)
    col = lax.broadcasted_iota(jnp.int32, (t, t), 1)
    mask = (col <= row) if forward else (col >= row)
    st = st_ref[0, 0]
    src, cmax, b, e_end = (st[j * nh:(j + 1) * nh] for j in range(4))
    b_end = st[4 * nh:5 * nh, 0:1]
    m_loc = st[5 * nh:6 * nh, 0:1]

    m_in = m_ref[:, 0:1]
    big_m = jnp.maximum(m_in, cmax)
    m_new = jnp.maximum(b_end + m_in, m_loc)
    a_dec = jnp.exp(b_end + m_in - m_new)
    s_dec = jnp.exp(m_loc - m_new)

    q = q_ref[0]
    k = k_ref[0] * (ML_QK ** -0.5)
    v = v_ref[0]
    q_in = (q.astype(F32) * _expand_heads(jnp.exp(m_in - big_m), ML_QK)).astype(BF16)
    k_end = (k.astype(F32) * _expand_heads(e_end, ML_QK)).astype(BF16)
    floor = _expand_heads(jnp.exp(-(b + big_m)), ML_V)

    ones3 = jnp.ones((3 * nh, t), BF16)
    dec_l = jnp.concatenate(list(_split3(-big_m)) + [ones3], axis=0)
    dec_r = jnp.concatenate([ones3] + list(_split3(src)), axis=0)
    head_of_row = lax.broadcasted_iota(jnp.int32, (6 * nh, t), 0) & (nh - 1)
    ones = jnp.ones((t, ML_V), BF16)
    nums, dens = [], []
    for h in range(nh):
        qk = slice(h * ML_QK, (h + 1) * ML_QK)
        va = jnp.concatenate([v[:, h * ML_V:(h + 1) * ML_V], ones], axis=1)
        s = lax.dot_general(q[:, qk], k[:, qk], _NT, preferred_element_type=F32)
        w_log = lax.dot_general(dec_l, jnp.where(head_of_row == h, dec_r, jnp.zeros_like(dec_r)), _TN,
                                preferred_element_type=F32)
        w = (jnp.where(mask, jnp.exp(w_log), 0.0) * s).astype(BF16)
        ct = ct_ref[h]
        r = (jnp.dot(w, va, preferred_element_type=F32)
             + jnp.dot(q_in[:, qk], ct.astype(BF16), preferred_element_type=F32))
        nums.append(r[:, :ML_V])
        dens.append(r[:, ML_V:])
        c_loc = lax.dot_general(k_end[:, qk], va, _TN, preferred_element_type=F32)
        ct_ref[h] = a_dec[h:h + 1] * ct + s_dec[h:h + 1] * c_loc
    num = jnp.concatenate(nums, axis=1)
    den = jnp.concatenate(dens, axis=1)
    o_ref[0] = (num / jnp.maximum(jnp.abs(den), floor)).astype(o_ref.dtype)
    m_ref[...] = jnp.broadcast_to(m_new, m_ref.shape)


def _mlstm_scan(q, k, v, grow, gate_b, n_ctx):
    bsz, n, _ = q.shape
    t = ML_CHUNK
    assert t & (t - 1) == 0
    nc = n // t
    ncc = n_ctx // t

    def back(i):
        return jnp.where(i < ncc, ncc - 1 - i, nc - 1 - (i - ncc))

    stats = _mlstm_gate_stats(grow, gate_b)
    fwd = lambda b, i: (b, i, 0)
    bwd = lambda b, i: (b, back(i), 0)
    specs = lambda tok, d: [
        pl.BlockSpec((1, t, ML_QKW), tok), pl.BlockSpec((1, t, ML_QKW), tok), pl.BlockSpec((1, t, D_MODEL), tok),
        pl.BlockSpec((1, 1, ML_STAT_ROWS, t), lambda b, i: (b, d, 0, tok(b, i)[1]))]
    out = jax.ShapeDtypeStruct((bsz, n, D_MODEL), BF16)
    return pl.pallas_call(
        functools.partial(_mlstm_kernel, t=t),
        grid=(bsz, nc),
        in_specs=specs(fwd, 0) + specs(bwd, 1),
        out_specs=[pl.BlockSpec((1, t, D_MODEL), fwd), pl.BlockSpec((1, t, D_MODEL), bwd)],
        out_shape=[out, out],
        scratch_shapes=[pltpu.VMEM((2, ML_HEADS, ML_QK, 2 * ML_V), F32), pltpu.VMEM((2, ML_HEADS, LANES), F32)],
        compiler_params=_cparams(("arbitrary", "arbitrary")),
    )(q, k, v, stats, q, k, v, stats)


def _diff_kernel(lam_ref, q_ref, kt_ref, v_ref, o_ref, acc_ref, m_ref, s_ref, smax_ref, va_ref, *, tq, tk, n,
                 n_ctx, lambda_init):
    qi = pl.program_id(2)

    @pl.when(qi == 0)
    def _():
        def fill(c, carry):
            rows = pl.ds(pl.multiple_of(c * tk, tk), tk)
            va_ref[rows, :DF_VD] = v_ref[0, rows, :]
            va_ref[rows, DF_VD:] = jnp.ones((tk, DF_VD), va_ref.dtype)
            return carry
        lax.fori_loop(0, n // tk, fill, 0)

    q = q_ref[0]
    lane = lax.broadcasted_iota(jnp.int32, q.shape, 1)
    zero = jnp.zeros_like(q)
    qs = (jnp.where(lane < DF_HD, q, zero), jnp.where(lane < DF_HD, zero, q))
    acc_ref[...] = jnp.zeros_like(acc_ref)
    m_ref[...] = jnp.full_like(m_ref, NEG_INF)

    def scores(start, size, slot):
        kt = kt_ref[0, :, pl.ds(start, size)]
        for j in range(2):
            s = jnp.dot(qs[j], kt, preferred_element_type=F32)
            s_ref[slot, j, :, :size] = s
            smax_ref[slot, j] = jnp.max(s, axis=1, keepdims=True)

    def accumulate(start, size, slot):
        va = va_ref[pl.ds(start, size), :]
        ss = [s_ref[slot, j, :, :size] for j in range(2)]
        m_old = [m_ref[j] for j in range(2)]
        m_new = [jnp.maximum(m_old[j], smax_ref[slot, j]) for j in range(2)]
        ps = [jnp.exp2((ss[j] - m_new[j]).astype(BF16)) for j in range(2)]
        pv = [jnp.dot(ps[j], va, preferred_element_type=F32) for j in range(2)]
        for j in range(2):
            acc_ref[j] = jnp.exp2(m_old[j] - m_new[j]) * acc_ref[j] + pv[j]
            m_ref[j] = m_new[j]

    is_ctx_tile = qi * tq < n_ctx

    @pl.when(is_ctx_tile)
    def _():
        scores(0, n_ctx, 0)
        accumulate(0, n_ctx, 0)

    @pl.when(jnp.logical_not(is_ctx_tile))
    def _():
        n_chunks = n // tk
        unroll = ATT_UNROLL if (n_chunks - 1) % ATT_UNROLL == 0 else 2
        scores(0, tk, 0)

        def body(i, carry):
            c0 = pl.multiple_of(unroll * i * tk, tk)
            for u in range(unroll):
                scores(c0 + (u + 1) * tk, tk, (u + 1) % 2)
                accumulate(c0 + u * tk, tk, u % 2)
            return carry
        lax.fori_loop(0, (n_chunks - 1) // unroll, body, 0)
        accumulate((n_chunks - 1) * tk, tk, 0)

    lam = lam_ref[...]
    lam_full = (jnp.exp(jnp.sum(lam[0:1] * lam[1:2], axis=1, keepdims=True))
                - jnp.exp(jnp.sum(lam[2:3] * lam[3:4], axis=1, keepdims=True)) + lambda_init)
    o1 = acc_ref[0, :, :DF_VD] / acc_ref[0, :, DF_VD:]
    o2 = acc_ref[1, :, :DF_VD] / acc_ref[1, :, DF_VD:]
    o_ref[0] = (o1 - lam_full * o2).astype(o_ref.dtype)


def _diff_attention(q, k, v, lam, lambda_init, n_ctx):
    bsz, n, _ = q.shape
    tq = ATT_Q_TILE
    tk = next(t for t in ATT_KV_TILES if n % t == 0 and (n // t) % 2 == 1)
    assert n_ctx % tq == 0 and n % tq == 0 and n_ctx <= tk
    kt = jnp.swapaxes(k, 1, 2)
    kern = functools.partial(_diff_kernel, tq=tq, tk=tk, n=n, n_ctx=n_ctx, lambda_init=lambda_init)
    return pl.pallas_call(
        kern,
        grid=(bsz, DF_HEADS, n // tq),
        in_specs=[
            pl.BlockSpec(lam.shape, lambda b, h, i: (0, 0)),
            pl.BlockSpec((1, tq, 2 * DF_HD), lambda b, h, i: (b, i, h)),
            pl.BlockSpec((1, 2 * DF_HD, n), lambda b, h, i: (b, h, 0)),
            pl.BlockSpec((1, n, DF_VD), lambda b, h, i: (b, 0, h)),
        ],
        out_specs=pl.BlockSpec((1, tq, DF_VD), lambda b, h, i: (b, i, h)),
        out_shape=jax.ShapeDtypeStruct((bsz, n, D_MODEL), BF16),
        scratch_shapes=[pltpu.VMEM((2, tq, 2 * DF_VD), F32), pltpu.VMEM((2, tq, 1), F32),
                        pltpu.VMEM((2, 2, tq, tk), F32), pltpu.VMEM((2, 2, tq, 1), F32),
                        pltpu.VMEM((n, 2 * DF_VD), v.dtype)],
        compiler_params=_cparams(("arbitrary", "arbitrary", "arbitrary")),
    )(lam.astype(F32), q, kt, v)


def _swa_mask_table(tq, bw):
    t = jnp.arange(tq, dtype=jnp.int32)[:, None]
    s = jnp.arange(bw, dtype=jnp.int32)[None, :]

    def one(off, first_ok):
        ok = jnp.logical_and(s >= first_ok, jnp.abs(t - s + off) <= SW_WIN)
        return jnp.where(ok, 0.0, NEG_INF).astype(F32)
    return jnp.stack([one(SW_WIN, 0), one(SW_WIN, SW_WIN), one(2 * SW_WIN, 0), one(0, bw)])


def _swa_kernel(sink_ref, q_ref, kt_ref, va_ref, mask_ref, o_ref, *, tq, n, n_ctx):
    g = pl.program_id(1)
    i = pl.program_id(2)
    bw = tq + 2 * SW_WIN
    q = q_ref[0]
    qh = jnp.concatenate([q[:, u * SW_HD:(u + 1) * SW_HD] for u in range(SW_GROUP)], axis=0)
    start = pl.multiple_of(jnp.clip(i * tq - SW_WIN, 0, n - bw), SW_WIN)
    s_c = jnp.dot(qh, kt_ref[0, :, 0:n_ctx], preferred_element_type=F32)
    s_b = jnp.dot(qh, kt_ref[0, :, pl.ds(start, bw)], preferred_element_type=F32)
    case = jnp.where(i * tq < n_ctx, 3, jnp.where(i * tq - SW_WIN < n_ctx, 1,
                                                   jnp.where(i * tq - SW_WIN > n - bw, 2, 0)))
    s_b = s_b + jnp.tile(mask_ref[case], (SW_GROUP, 1))
    head = lax.broadcasted_iota(jnp.int32, (SW_GROUP * tq, 1), 0) // tq
    sink = jnp.zeros((SW_GROUP * tq, 1), F32)
    for u in range(SW_GROUP):
        sink = jnp.where(head == u, sink_ref[g * SW_GROUP + u] * LOG2E, sink)
    m = jnp.maximum(sink, jnp.maximum(jnp.max(s_c, axis=1, keepdims=True), jnp.max(s_b, axis=1, keepdims=True)))
    acc = (jnp.dot(jnp.exp2((s_c - m).astype(BF16)), va_ref[0, 0:n_ctx, :], preferred_element_type=F32)
           + jnp.dot(jnp.exp2((s_b - m).astype(BF16)), va_ref[0, pl.ds(start, bw), :],
                     preferred_element_type=F32))
    out = acc[:, :SW_HD] / (acc[:, SW_HD:] + jnp.exp2(sink - m))
    o_ref[0] = jnp.concatenate([out[u * tq:(u + 1) * tq] for u in range(SW_GROUP)], axis=1).astype(o_ref.dtype)


def _swa_attention(q, k, v, sinks, n_ctx):
    bsz, n, _ = q.shape
    tq = SWA_Q_TILE
    assert tq & (tq - 1) == 0 and n_ctx % tq == 0 and n % tq == 0 and tq % SW_WIN == 0
    bw = tq + 2 * SW_WIN
    assert n - bw >= n_ctx + tq
    gw = SW_GROUP * SW_HD
    kt = jnp.swapaxes(k, 1, 2)
    va = jnp.concatenate([v.reshape(bsz, n, SW_KV, SW_HD),
                          jnp.ones((bsz, n, SW_KV, SW_HD), v.dtype)], axis=-1).reshape(bsz, n, SW_KV * 2 * SW_HD)
    return pl.pallas_call(
        functools.partial(_swa_kernel, tq=tq, n=n, n_ctx=n_ctx),
        grid=(bsz, SW_KV, n // tq),
        in_specs=[
            pl.BlockSpec(memory_space=pltpu.SMEM),
            pl.BlockSpec((1, tq, gw), lambda b, g, i: (b, i, g)),
            pl.BlockSpec((1, SW_HD, n), lambda b, g, i: (b, g, 0)),
            pl.BlockSpec((1, n, 2 * SW_HD), lambda b, g, i: (b, 0, g)),
            _const_spec((4, tq, bw)),
        ],
        out_specs=pl.BlockSpec((1, tq, gw), lambda b, g, i: (b, i, g)),
        out_shape=jax.ShapeDtypeStruct((bsz, n, D_MODEL), BF16),
        compiler_params=_cparams(("arbitrary", "arbitrary", "arbitrary")),
    )(sinks.astype(F32), q, kt, va, _swa_mask_table(tq, bw))


def _head_norm(y, g, width):
    parts = []
    for h in range(y.shape[1] // width):
        yh = y[:, h * width:(h + 1) * width]
        parts.append(yh * lax.rsqrt(jnp.mean(yh * yh, axis=-1, keepdims=True) + EPS))
    return jnp.concatenate(parts, axis=1) * g


def _mixer_update(it, mode, n_ctx, tm, post_scale):
    x_ref, gate_ref, w_ref = next(it), next(it), next(it)
    if mode == "mlstm":
        hf_ref, hb_ref, og_ref, g_ref = next(it), next(it), next(it), next(it)
        y = _head_norm(hf_ref[0].astype(F32) + hb_ref[0].astype(F32), g_ref[...], ML_V)
        z = y * _sigmoid(og_ref[0].astype(F32))
    elif mode == "diff":
        a_ref, g_ref = next(it), next(it)
        z = _head_norm(a_ref[0].astype(F32), g_ref[...], DF_VD) * post_scale
    else:
        a_ref = next(it)
        z = a_ref[0]
    y = jnp.dot(z.astype(BF16), w_ref[...], preferred_element_type=F32)
    gate = _row_select(gate_ref[0], pl.program_id(1) * tm, tm, n_ctx)
    return x_ref[0] + gate * y


def _mixer_operands(x, mix, tm):
    mode, gate_pair, w_out, extra, _ = mix
    d = x.shape[2]
    tok = lambda b, t: (b, t, 0)
    in_specs = [pl.BlockSpec((1, tm, d), tok), pl.BlockSpec((1, 2, d), lambda b, t: (b, 0, 0)),
                _const_spec(w_out.shape)]
    args = [x, gate_pair, w_out]
    if mode == "mlstm":
        hf, hb, og, g = extra
        in_specs += [pl.BlockSpec((1, tm, d), tok), pl.BlockSpec((1, tm, d), tok), pl.BlockSpec((1, tm, d), tok),
                     pl.BlockSpec((1, d), lambda b, t: (0, 0))]
        args += [hf, hb, og, g.reshape(1, d)]
    elif mode == "diff":
        a, g = extra
        in_specs += [pl.BlockSpec((1, tm, d), tok), pl.BlockSpec((1, d), lambda b, t: (0, 0))]
        args += [a, g.reshape(1, d)]
    else:
        (a,) = extra
        in_specs += [pl.BlockSpec((1, tm, d), tok)]
        args += [a]
    return in_specs, args


def _dense_ffn_kernel(*refs, mix_mode, post_scale, n_ctx, tm, bounds):
    it = iter(refs)
    x = _mixer_update(it, mix_mode, n_ctx, tm, post_scale)
    a_ref, b_ref, gate_ref, wa_ref, wb_ref, wo_ref, o_ref, h_ref, g_ref = (next(it) for _ in range(9))
    row0 = pl.program_id(1) * tm
    h_ref[...] = _norm_mod(x, a_ref[0], b_ref[0], row0, n_ctx).astype(BF16)
    for c0, c1 in zip(bounds[:-1], bounds[1:]):
        a = jnp.dot(h_ref[...], wa_ref[:, c0:c1], preferred_element_type=F32)
        b = jnp.dot(h_ref[...], wb_ref[:, c0:c1], preferred_element_type=F32)
        g_ref[:, c0:c1] = (a * _sigmoid(a) * b).astype(BF16)
    y = jnp.dot(g_ref[...], wo_ref[...], preferred_element_type=F32)
    o_ref[0] = x + _row_select(gate_ref[0], row0, tm, n_ctx) * y


def _dense_ffn(x, mix, a_pair, b_pair, gate_pair, wa, wb, wo, n_ctx):
    bsz, n, d = x.shape
    tm = TOKEN_TILE
    mix_specs, mix_args = _mixer_operands(x, mix, tm)
    ff = wa.shape[1]
    half = (pl.cdiv(ff // MXU_WIDTH, 2) * MXU_WIDTH) if ff % MXU_WIDTH == 0 else ff
    bounds = (0, half, ff) if half < ff else (0, ff)
    tok = lambda b, t: (b, t, 0)
    pair = pl.BlockSpec((1, 2, d), lambda b, t: (b, 0, 0))
    kern = functools.partial(_dense_ffn_kernel, mix_mode=mix[0], post_scale=mix[4], n_ctx=n_ctx, tm=tm,
                             bounds=bounds)
    return pl.pallas_call(
        kern,
        grid=(bsz, n // tm),
        in_specs=mix_specs + [pair, pair, pair,
                              _const_spec(wa.shape), _const_spec(wb.shape), _const_spec(wo.shape)],
        out_specs=pl.BlockSpec((1, tm, d), tok),
        out_shape=jax.ShapeDtypeStruct((bsz, n, d), F32),
        scratch_shapes=[pltpu.VMEM((tm, d), BF16), pltpu.VMEM((tm, ff), BF16)],
        compiler_params=_cparams(("arbitrary", "arbitrary")),
    )(*mix_args, a_pair, b_pair, gate_pair, wa, wb, wo)


def _moe_route_kernel(*refs, mix_mode, post_scale, n_ctx, tm):
    it = iter(refs)
    x = _mixer_update(it, mix_mode, n_ctx, tm, post_scale)
    a_ref, b_ref, r_ref, x_ref, h_ref, g_ref, i_ref = (next(it) for _ in range(7))
    row0 = pl.program_id(1) * tm
    x_ref[0] = x
    h = _norm_mod(x, a_ref[0], b_ref[0], row0, n_ctx)
    h_ref[0] = h.astype(BF16)
    logits = lax.dot_general(r_ref[...], h, (((1,), (1,)), ((), ())), preferred_element_type=F32,
                             precision=lax.Precision.HIGHEST)
    e_idx = lax.broadcasted_iota(jnp.int32, logits.shape, 0)
    v1 = jnp.max(logits, axis=0, keepdims=True)
    i1 = jnp.min(jnp.where(logits == v1, e_idx, N_EXPERTS), axis=0, keepdims=True)
    rest = jnp.where(e_idx == i1, -jnp.inf, logits)
    v2 = jnp.max(rest, axis=0, keepdims=True)
    i2 = jnp.min(jnp.where(rest == v2, e_idx, N_EXPERTS), axis=0, keepdims=True)
    p2 = jnp.exp(v2 - v1)
    den = 1.0 + p2
    g_ref[0] = jnp.concatenate([1.0 / den, p2 / den], axis=0)
    i_ref[0] = jnp.concatenate([i1, i2], axis=0)


def _moe_route(x, mix, a_pair, b_pair, router_t, n_ctx):
    bsz, n, d = x.shape
    tm = TOKEN_TILE
    tok = lambda b, t: (b, t, 0)
    pair = pl.BlockSpec((1, 2, d), lambda b, t: (b, 0, 0))
    mix_specs, mix_args = _mixer_operands(x, mix, tm)
    return pl.pallas_call(
        functools.partial(_moe_route_kernel, mix_mode=mix[0], post_scale=mix[4], n_ctx=n_ctx, tm=tm),
        grid=(bsz, n // tm),
        in_specs=mix_specs + [pair, pair, _const_spec(router_t.shape)],
        out_specs=[pl.BlockSpec((1, tm, d), tok),
                   pl.BlockSpec((1, tm, d), tok),
                   pl.BlockSpec((1, TOP_K, tm), lambda b, t: (b, 0, t)),
                   pl.BlockSpec((1, TOP_K, tm), lambda b, t: (b, 0, t))],
        out_shape=[jax.ShapeDtypeStruct((bsz, n, d), F32),
                   jax.ShapeDtypeStruct((bsz, n, d), BF16),
                   jax.ShapeDtypeStruct((bsz, TOP_K, n), F32),
                   jax.ShapeDtypeStruct((bsz, TOP_K, n), jnp.int32)],
        compiler_params=_cparams(("arbitrary", "arbitrary")),
    )(*mix_args, a_pair, b_pair, router_t)


def _moe_expert_kernel(te_ref, nu_ref, x_ref, wi_ref, wo_ref, o_ref, g_ref, *, ff, chunk):
    i = pl.program_id(0)

    @pl.when(i < nu_ref[0])
    def _():
        x = x_ref[...]
        for c0 in range(0, ff, chunk):
            a = jnp.dot(x, wi_ref[0, :, c0:c0 + chunk], preferred_element_type=F32)
            b = jnp.dot(x, wi_ref[0, :, ff + c0:ff + c0 + chunk], preferred_element_type=F32)
            g_ref[:, c0:c0 + chunk] = (a * _sigmoid(a) * b).astype(BF16)
        o_ref[...] = jnp.dot(g_ref[...], wo_ref[0], preferred_element_type=F32).astype(o_ref.dtype)

    @pl.when(i >= nu_ref[0])
    def _():
        o_ref[...] = jnp.zeros_like(o_ref)


def _moe_experts(xs, w_in, w_out, tile_expert, n_used):
    p, d = xs.shape
    tm = MOE_TILE
    ff = w_out.shape[1]
    chunk = MOE_FF_CHUNK
    assert ff % chunk == 0 and chunk % LANES == 0
    nt = p // tm

    def teff(i, nu):
        return jnp.minimum(i, jnp.maximum(nu[0] - 1, 0))

    grid_spec = pltpu.PrefetchScalarGridSpec(
        num_scalar_prefetch=2,
        grid=(nt,),
        in_specs=[
            pl.BlockSpec((tm, d), lambda i, te, nu: (teff(i, nu), 0)),
            pl.BlockSpec((1, d, 2 * ff), lambda i, te, nu: (te[teff(i, nu)], 0, 0),
                         pipeline_mode=pl.Buffered(1)),
            pl.BlockSpec((1, ff, d), lambda i, te, nu: (te[teff(i, nu)], 0, 0),
                         pipeline_mode=pl.Buffered(1)),
        ],
        out_specs=pl.BlockSpec((tm, d), lambda i, te, nu: (i, 0)),
        scratch_shapes=[pltpu.VMEM((tm, ff), BF16)],
    )
    return pl.pallas_call(
        functools.partial(_moe_expert_kernel, ff=ff, chunk=chunk),
        grid_spec=grid_spec,
        out_shape=jax.ShapeDtypeStruct((p, d), BF16),
        compiler_params=_cparams(("arbitrary",)),
    )(tile_expert, n_used, xs, w_in, w_out)


def _moe_combine_kernel(*refs, n_ctx, tm, skip, final):
    it = iter(refs)
    x_ref, gate_ref, y0_ref, y1_ref, pk_ref = next(it), next(it), next(it), next(it), next(it)
    fg_ref = next(it) if final else None
    o_ref = next(it)
    pk = pk_ref[0]
    y = pk[:, 0:1] * y0_ref[0].astype(F32) + pk[:, 1:2] * y1_ref[0].astype(F32)
    x = x_ref[0] + _row_select(gate_ref[0], (pl.program_id(1) + skip) * tm, tm, n_ctx) * y
    if final:
        x = x * lax.rsqrt(jnp.mean(x * x, axis=-1, keepdims=True) + EPS) * fg_ref[...]
    o_ref[0] = x


def _moe_combine(x, gate_pair, y0, y1, slot_w, n_ctx, final_g=None):
    bsz, n, d = x.shape
    final = final_g is not None
    tm = n_ctx if final else TOKEN_TILE
    skip = n_ctx // tm if final else 0
    assert n % tm == 0
    tok = lambda b, t: (b, t + skip, 0)
    in_specs = [pl.BlockSpec((1, tm, d), tok), pl.BlockSpec((1, 2, d), lambda b, t: (b, 0, 0)),
                pl.BlockSpec((1, tm, d), tok), pl.BlockSpec((1, tm, d), tok),
                pl.BlockSpec((1, tm, TOP_K), tok)]
    args = [x, gate_pair, y0, y1, slot_w]
    if final:
        in_specs.append(pl.BlockSpec((1, d), lambda b, t: (0, 0)))
        args.append(final_g.reshape(1, d))
    return pl.pallas_call(
        functools.partial(_moe_combine_kernel, n_ctx=n_ctx, tm=tm, skip=skip, final=final),
        grid=(bsz, n // tm - skip),
        in_specs=in_specs,
        out_specs=pl.BlockSpec((1, tm, d), lambda b, t: (b, t, 0)),
        out_shape=jax.ShapeDtypeStruct((bsz, n - skip * tm, d), F32),
        compiler_params=_cparams(("arbitrary", "arbitrary")),
    )(*args)


def _moe_layer(x, mix, a_pair, b_pair, gate_pair, router, w_in_all, w_out_all, layer, n_ctx, final_g):
    bsz, n, d = x.shape
    tm = MOE_TILE
    ntok = bsz * n
    x, h2, gates, idx = _moe_route(x, mix, a_pair, b_pair, router.T, n_ctx)
    w_in_all, w_out_all, h2 = lax.optimization_barrier((w_in_all, w_out_all, h2))
    w_in = w_in_all[layer].astype(BF16)
    w_out = w_out_all[layer].astype(BF16)
    e_flat = jnp.moveaxis(idx, 1, 0).reshape(TOP_K * ntok)
    onehot = (e_flat[:, None] == jnp.arange(N_EXPERTS, dtype=jnp.int32)[None, :]).astype(jnp.int32)
    csum = jnp.cumsum(onehot, axis=0)
    counts = csum[-1]
    rank = jnp.sum(onehot * csum, axis=1) - 1
    padded = ((counts + tm - 1) // tm) * tm
    ends = jnp.cumsum(padded)
    starts = ends - padded
    dest = jnp.sum(onehot * starts[None, :], axis=1) + rank
    n_rows = TOP_K * ntok + N_EXPERTS * tm
    n_rows = ((n_rows + tm - 1) // tm) * tm
    tile_start = jnp.arange(n_rows // tm, dtype=jnp.int32) * tm
    tile_expert = jnp.minimum(jnp.sum((tile_start[:, None] >= ends[None, :]).astype(jnp.int32), axis=1),
                              N_EXPERTS - 1).astype(jnp.int32)
    n_used = (ends[-1] // tm).astype(jnp.int32).reshape(1)
    order = jnp.argsort(e_flat, stable=True).astype(jnp.int32)
    local = (tile_start - starts[tile_expert])[:, None] + jnp.arange(tm, dtype=jnp.int32)[None, :]
    pos = (jnp.cumsum(counts) - counts)[tile_expert][:, None] + local
    filler = jnp.arange(n_rows, dtype=jnp.int32).reshape(-1, tm) % (TOP_K * ntok)
    pos = jnp.where(local < counts[tile_expert][:, None], pos, filler).reshape(n_rows)
    src = order.at[pos].get(mode="promise_in_bounds") % ntok
    rows = lambda a, i: a.at[i].get(mode="promise_in_bounds")
    xs = rows(h2.reshape(ntok, d), src)
    ys = _moe_experts(xs, w_in, w_out, tile_expert, n_used)
    y0 = rows(ys, dest[:ntok]).reshape(bsz, n, d)
    y1 = rows(ys, dest[ntok:]).reshape(bsz, n, d)
    return _moe_combine(x, gate_pair, y0, y1, jnp.swapaxes(gates, 1, 2), n_ctx, final_g)


def _rope_table(n_ctx, seq):
    pos = jnp.arange(seq)
    row = (pos // GRID_W).astype(F32)
    col = (pos % GRID_W).astype(F32)
    nf = DF_HD // 4
    inv = jnp.power(ROPE_BASE, -jnp.arange(nf, dtype=F32) / nf)
    ang = jnp.concatenate([row[:, None] * inv, col[:, None] * inv], axis=-1)
    cos, sin = jnp.cos(ang), jnp.sin(ang)
    cos = jnp.concatenate([jnp.ones((n_ctx, 2 * nf), F32), cos], axis=0)
    sin = jnp.concatenate([jnp.zeros((n_ctx, 2 * nf), F32), sin], axis=0)
    c128 = jnp.tile(jnp.concatenate([cos, cos], axis=1), (1, 2))
    s128 = jnp.tile(jnp.concatenate([-sin, sin], axis=1), (1, 2))
    return jnp.concatenate([c128, s128], axis=1)


def _pair(v, bsz):
    return jnp.stack([v[:bsz], jnp.broadcast_to(v[bsz], (bsz, v.shape[1]))], axis=1)


def kernel(x, c, ctx, c_ctx, ada_w, ada_b, norm1, norm2, ml_w_in, ml_gate_b, ml_hnorm, ml_w_out,
           df_w_in, df_lam, df_hnorm, df_w_out, sw_w_in, sw_sinks, sw_w_out,
           ffn_w_in, ffn_w_out, moe_router, moe_w_in, moe_w_out, final_norm):
    bsz, seq, d = x.shape
    n_ctx = ctx.shape[1]
    depth = ada_w.shape[0]
    assert d == D_MODEL and n_ctx % ATT_Q_TILE == 0 and (n_ctx + seq) % TOKEN_TILE == 0

    xs = jnp.concatenate([ctx, x], axis=1)
    cond_rows = SUBLANES * ((bsz + 1 + SUBLANES - 1) // SUBLANES)
    cond = jnp.zeros((cond_rows, d), F32).at[:bsz].set(c).at[bsz].set(c_ctx)
    mods = _ada_mod(cond, ada_w, ada_b)
    cs = _rope_table(n_ctx, seq)

    for i in range(depth):
        m = [mods[i, :, k * d:(k + 1) * d] for k in range(6)]
        a1 = _pair(norm1[i][None, :] * (1.0 + m[1]), bsz)
        b1 = _pair(m[0], bsz)
        g1 = _pair(m[2], bsz)
        a2 = _pair(norm2[i][None, :] * (1.0 + m[4]), bsz)
        b2 = _pair(m[3], bsz)
        g2 = _pair(m[5], bsz)
        kind, j = i % N_MIXERS, i // N_MIXERS
        if kind == 0:
            w = ml_w_in[j]
            nmain = 2 * ML_QKW + 2 * D_MODEL
            w_main = w[:, :nmain].astype(BF16)
            wgt = w[:, nmain:].T.astype(BF16)
            segs = ((0, ML_QKW, False, 1.0), (ML_QKW, 2 * ML_QKW, False, 1.0),
                    (2 * ML_QKW, 2 * ML_QKW + d, False, 1.0), (2 * ML_QKW + d, nmain, False, 1.0))
            q, k, v, og, grow = _project(xs, a1, b1, w_main, segs, (BF16, BF16, BF16, BF16), n_ctx, wgt=wgt)
            hf, hb = _mlstm_scan(q, k, v, grow, ml_gate_b[j], n_ctx)
            mix = ("mlstm", g1, ml_w_out[j].astype(BF16), (hf, hb, og, ml_hnorm[j]), 1.0)
        elif kind == 1:
            lambda_init = 0.8 - 0.6 * math.exp(-0.3 * i)
            segs = ((0, DF_QKW, True, DF_HD ** -0.5 * LOG2E), (DF_QKW, 2 * DF_QKW, True, 1.0),
                    (2 * DF_QKW, 2 * DF_QKW + d, False, 1.0))
            q, k, v = _project(xs, a1, b1, df_w_in[j].astype(BF16), segs, (BF16, BF16, BF16), n_ctx, cs=cs)
            att = _diff_attention(q, k, v, df_lam[j], lambda_init, n_ctx)
            mix = ("diff", g1, df_w_out[j].astype(BF16), (att, df_hnorm[j]), 1.0 - lambda_init)
        else:
            nq = SW_HEADS * SW_HD
            nkv = SW_KV * SW_HD
            segs = ((0, nq, True, SW_HD ** -0.5 * LOG2E), (nq, nq + nkv, True, 1.0), (nq + nkv, nq + 2 * nkv, False, 1.0))
            q, k, v = _project(xs, a1, b1, sw_w_in[j].astype(BF16), segs, (BF16, BF16, BF16), n_ctx, cs=cs)
            att = _swa_attention(q, k, v, sw_sinks[j], n_ctx)
            mix = ("plain", g1, sw_w_out[j].astype(BF16), (att,), 1.0)
        jf = i // 2
        last = i == depth - 1
        if i % 2 == 0:
            ff = ffn_w_out.shape[1]
            wi = ffn_w_in[jf].astype(BF16)
            xs = _dense_ffn(xs, mix, a2, b2, g2, wi[:, :ff], wi[:, ff:], ffn_w_out[jf].astype(BF16), n_ctx)
            if last:
                raise NotImplementedError("final norm is fused into the MoE combine")
        else:
            xs = _moe_layer(xs, mix, a2, b2, g2, moe_router[jf], moe_w_in, moe_w_out, jf, n_ctx,
                            final_norm if last else None)
    return xs
```

```python
import functools
import math

import jax
import jax.numpy as jnp
from jax import lax
from jax.experimental import pallas as pl
from jax.experimental.pallas import tpu as pltpu

F32 = jnp.float32
BF16 = jnp.bfloat16

D_MODEL = 1024
GRID_W = 64
EPS = 1e-6
NEG_INF = -1e30
ROPE_BASE = 10000.0
LOG2E = math.log2(math.e)
N_MIXERS = 3

ML_HEADS = 8
ML_V = D_MODEL // ML_HEADS
ML_QK = ML_V // 2
ML_QKW = ML_HEADS * ML_QK
ML_CHUNK = 256

DF_HD = 64
DF_HEADS = D_MODEL // (2 * DF_HD)
DF_VD = 2 * DF_HD
DF_QKW = 2 * DF_HEADS * DF_HD

SW_HD = 64
SW_HEADS = D_MODEL // SW_HD
SW_KV = 4
SW_GROUP = SW_HEADS // SW_KV
SW_WIN = 128

N_EXPERTS = 8
TOP_K = 2

LANES = 128
SUBLANES = 8
MXU_WIDTH = 256
V7X_VMEM_BYTES = 64 * 1024 * 1024
VMEM_LIMIT = V7X_VMEM_BYTES - 8 * 1024 * 1024

TOKEN_TILE = 640
ATT_Q_TILE = 256
ATT_KV_TILES = (3328, 1280, 256)
ATT_UNROLL = 4
SWA_Q_TILE = 256
MOE_TILE = 512
MOE_FF_CHUNK = 1792


def _cparams(semantics):
    return pltpu.CompilerParams(dimension_semantics=semantics, vmem_limit_bytes=VMEM_LIMIT)


def _const_spec(shape):
    nd = len(shape)
    return pl.BlockSpec(shape, lambda *_: (0,) * nd, pipeline_mode=pl.Buffered(1))


def _sigmoid(x):
    return 1.0 / (1.0 + jnp.exp(-x))


def _log_sigmoid(x):
    return jnp.minimum(x, 0.0) - jnp.log(1.0 + jnp.exp(-jnp.abs(x)))


def _norm_mod(x, a_rows, b_rows, row0, n_ctx):
    xn = x * lax.rsqrt(jnp.mean(x * x, axis=-1, keepdims=True) + EPS)
    is_ctx = (row0 + lax.broadcasted_iota(jnp.int32, (x.shape[0], 1), 0)) < n_ctx
    a = jnp.where(is_ctx, a_rows[1:2], a_rows[0:1])
    b = jnp.where(is_ctx, b_rows[1:2], b_rows[0:1])
    return xn * a + b


def _row_select(rows, row0, n_rows, n_ctx):
    is_ctx = (row0 + lax.broadcasted_iota(jnp.int32, (n_rows, 1), 0)) < n_ctx
    return jnp.where(is_ctx, rows[1:2], rows[0:1])


def _ada_kernel(c_ref, w_ref, b_ref, o_ref):
    c = c_ref[...]
    s = c * _sigmoid(c)
    o_ref[0] = jnp.dot(s, w_ref[0], preferred_element_type=F32,
                       precision=lax.Precision.HIGHEST) + b_ref[0]


def _ada_mod(cond, ada_w, ada_b):
    depth, d, six_d = ada_w.shape
    rows = cond.shape[0]
    col = D_MODEL
    return pl.pallas_call(
        _ada_kernel,
        grid=(depth, six_d // col),
        in_specs=[
            pl.BlockSpec((rows, d), lambda i, j: (0, 0)),
            pl.BlockSpec((1, d, col), lambda i, j: (i, 0, j)),
            pl.BlockSpec((1, 1, col), lambda i, j: (i, 0, j)),
        ],
        out_specs=pl.BlockSpec((1, rows, col), lambda i, j: (i, 0, j)),
        out_shape=jax.ShapeDtypeStruct((depth, rows, six_d), F32),
        compiler_params=_cparams(("arbitrary", "arbitrary")),
    )(cond, ada_w, ada_b.reshape(depth, 1, six_d))


def _rope(r, cos, sin):
    w = r.shape[1]
    lane = lax.broadcasted_iota(jnp.int32, r.shape, 1)
    swapped = jnp.where((lane & 32) == 0, pltpu.roll(r, w - 32, 1), pltpu.roll(r, 32, 1))
    reps = w // LANES
    return r * jnp.tile(cos, (1, reps)) + swapped * jnp.tile(sin, (1, reps))


def _proj_kernel(*refs, segs, n_ctx, tm, has_rope, has_grow):
    it = iter(refs)
    x_ref, a_ref, b_ref, w_ref = next(it), next(it), next(it), next(it)
    cs_ref = next(it) if has_rope else None
    wgt_ref = next(it) if has_grow else None
    outs = [next(it) for _ in segs]
    grow_ref = next(it) if has_grow else None
    h_ref = next(it)

    row0 = pl.program_id(1) * tm
    h_ref[...] = _norm_mod(x_ref[0], a_ref[0], b_ref[0], row0, n_ctx).astype(BF16)
    for (c0, c1, rope, scale, ones_w), o_ref in zip(segs, outs):
        r = jnp.dot(h_ref[...], w_ref[:, c0:c1], preferred_element_type=F32)
        if rope:
            r = _rope(r, cs_ref[:, :LANES], cs_ref[:, LANES:])
        if scale != 1.0:
            r = r * scale
        r = r.astype(o_ref.dtype)
        if ones_w:
            ones = jnp.ones((tm, ones_w), o_ref.dtype)
            r = jnp.concatenate([piece for g in range((c1 - c0) // ones_w)
                                 for piece in (r[:, g * ones_w:(g + 1) * ones_w], ones)], axis=1)
        o_ref[0] = r
    if has_grow:
        grow_ref[0] = lax.dot_general(wgt_ref[...], h_ref[...], (((1,), (1,)), ((), ())),
                                      preferred_element_type=F32)


def _project(x, a_pair, b_pair, w, segs, out_dtypes, n_ctx, cs=None, wgt=None):
    bsz, n, d = x.shape
    tm = TOKEN_TILE
    assert n % tm == 0
    tok = lambda b, t: (b, t, 0)
    in_specs = [
        pl.BlockSpec((1, tm, d), tok),
        pl.BlockSpec((1, 2, d), lambda b, t: (b, 0, 0)),
        pl.BlockSpec((1, 2, d), lambda b, t: (b, 0, 0)),
        _const_spec(w.shape),
    ]
    args = [x, a_pair, b_pair, w]
    if cs is not None:
        in_specs.append(pl.BlockSpec((tm, 2 * LANES), lambda b, t: (t, 0)))
        args.append(cs)
    if wgt is not None:
        in_specs.append(_const_spec(wgt.shape))
        args.append(wgt)
    widths = [(c1 - c0) * (2 if ones_w else 1) for (c0, c1, _, _, ones_w) in segs]
    out_specs = [pl.BlockSpec((1, tm, w_), tok) for w_ in widths]
    out_shape = [jax.ShapeDtypeStruct((bsz, n, w_), dt) for w_, dt in zip(widths, out_dtypes)]
    if wgt is not None:
        out_specs.append(pl.BlockSpec((1, wgt.shape[0], tm), lambda b, t: (b, 0, t)))
        out_shape.append(jax.ShapeDtypeStruct((bsz, wgt.shape[0], n), F32))
    kern = functools.partial(_proj_kernel, segs=segs, n_ctx=n_ctx, tm=tm,
                             has_rope=cs is not None, has_grow=wgt is not None)
    return pl.pallas_call(
        kern,
        grid=(bsz, n // tm),
        in_specs=in_specs,
        out_specs=out_specs,
        out_shape=out_shape,
        scratch_shapes=[pltpu.VMEM((tm, d), BF16)],
        compiler_params=_cparams(("arbitrary", "arbitrary")),
    )(*args)


def _split3(x):
    hi = x.astype(BF16)
    r1 = x - hi.astype(F32)
    mid = r1.astype(BF16)
    lo = (r1 - mid.astype(F32)).astype(BF16)
    return hi, mid, lo


_TN = (((0,), (0,)), ((), ()))
_NT = (((1,), (1,)), ((), ()))


def _expand_heads(x, width):
    nh = x.shape[0]
    x3 = jnp.concatenate(_split3(x), axis=0)
    r = lax.broadcasted_iota(jnp.int32, (3 * nh, nh * width), 0)
    c = lax.broadcasted_iota(jnp.int32, (3 * nh, nh * width), 1)
    sel = jnp.where((r & (nh - 1)) == (c >> int(math.log2(width))), 1.0, 0.0).astype(BF16)
    return lax.dot_general(x3, sel, _TN, preferred_element_type=F32)


ML_STAT_ROWS = 6 * ML_HEADS


def _mlstm_gate_kernel(gr_ref, br_ref, st_ref, *, t, chunks):
    nh = ML_HEADS
    d = pl.program_id(1)
    fwd = d == 0
    sgn = 1 - 2 * d
    row = lax.broadcasted_iota(jnp.int32, (t, t), 0)
    col = lax.broadcasted_iota(jnp.int32, (t, t), 1)
    tri_t = jnp.where((row - col) * sgn <= 0, 1.0, 0.0).astype(BF16)
    lane = lax.broadcasted_iota(jnp.int32, (nh, t), 1)
    scan_pos = jnp.where(fwd, lane, t - 1 - lane)
    for c in range(chunks):
        cols = slice(c * t, (c + 1) * t)
        gr = gr_ref[0, :, cols] + br_ref[...]
        gr = jnp.where(fwd, gr[:2 * nh], gr[2 * nh:])
        ig = gr[:nh]
        fg = _log_sigmoid(gr[nh:])
        bs = jnp.dot(jnp.concatenate(_split3(fg), axis=0), tri_t, preferred_element_type=F32)
        b = bs[:nh] + bs[nh:2 * nh] + bs[2 * nh:]
        b_end = jnp.sum(fg, axis=1, keepdims=True)
        src = ig - b
        cmax = src
        shift = 1
        while shift < t:
            moved = jnp.where(fwd, pltpu.roll(cmax, shift, 1), pltpu.roll(cmax, t - shift, 1))
            cmax = jnp.maximum(cmax, jnp.where(scan_pos >= shift, moved, -jnp.inf))
            shift *= 2
        w_end = b_end + src
        m_loc = jnp.max(w_end, axis=1, keepdims=True)
        e_end = jnp.exp(w_end - m_loc)
        st_ref[0, 0, :, cols] = jnp.concatenate(
            [src, cmax, b, e_end, jnp.broadcast_to(b_end, (nh, t)), jnp.broadcast_to(m_loc, (nh, t))], axis=0)


def _mlstm_gate_stats(grow, gate_b):
    bsz, rows, n = grow.shape
    t = ML_CHUNK
    assert t & (t - 1) == 0
    width = TOKEN_TILE * 2 if n % (TOKEN_TILE * 2) == 0 else t
    bias_row = jnp.broadcast_to(gate_b.reshape(rows, 1).astype(F32), (rows, t))
    return pl.pallas_call(
        functools.partial(_mlstm_gate_kernel, t=t, chunks=width // t),
        grid=(bsz, 2, n // width),
        in_specs=[pl.BlockSpec((1, rows, width), lambda b, d, i: (b, 0, i)),
                  pl.BlockSpec((rows, t), lambda b, d, i: (0, 0))],
        out_specs=pl.BlockSpec((1, 1, ML_STAT_ROWS, width), lambda b, d, i: (b, d, 0, i)),
        out_shape=jax.ShapeDtypeStruct((bsz, 2, ML_STAT_ROWS, n), F32),
        compiler_params=_cparams(("arbitrary", "arbitrary", "arbitrary")),
    )(grow, bias_row)


def _mlstm_kernel(qf_ref, kf_ref, vf_ref, stf_ref, qb_ref, kb_ref, vb_ref, stb_ref, of_ref, ob_ref,
                  ct_ref, m_ref, *, t):
    @pl.when(pl.program_id(1) == 0)
    def _():
        ct_ref[...] = jnp.zeros_like(ct_ref)
        m_ref[...] = jnp.zeros_like(m_ref)

    _mlstm_chunk(qf_ref, kf_ref, vf_ref, stf_ref, of_ref, ct_ref.at[0], m_ref.at[0], t=t, forward=True)
    _mlstm_chunk(qb_ref, kb_ref, vb_ref, stb_ref, ob_ref, ct_ref.at[1], m_ref.at[1], t=t, forward=False)


def _mlstm_chunk(q_ref, k_ref, v_ref, st_ref, o_ref, ct_ref, m_ref, *, t, forward):
    nh = ML_HEADS
    row = lax.broadcasted_iota(jnp.int32, (t, t), 0)
    col = lax.broadcasted_iota(jnp.int32, (t, t), 1)
    mask = (col <= row) if forward else (col >= row)
    st = st_ref[0, 0]
    src, cmax, b, e_end = (st[j * nh:(j + 1) * nh] for j in range(4))
    b_end = st[4 * nh:5 * nh, 0:1]
    m_loc = st[5 * nh:6 * nh, 0:1]

    m_in = m_ref[:, 0:1]
    big_m = jnp.maximum(m_in, cmax)
    m_new = jnp.maximum(b_end + m_in, m_loc)
    a_dec = jnp.exp(b_end + m_in - m_new)
    s_dec = jnp.exp(m_loc - m_new)

    q = q_ref[0]
    k = k_ref[0] * (ML_QK ** -0.5)
    v = v_ref[0]
    q_in = (q.astype(F32) * _expand_heads(jnp.exp(m_in - big_m), ML_QK)).astype(BF16)
    k_end = (k.astype(F32) * _expand_heads(e_end, ML_QK)).astype(BF16)
    floor = _expand_heads(jnp.exp(-(b + big_m)), ML_V)

    ones3 = jnp.ones((3 * nh, t), BF16)
    dec_l = jnp.concatenate(list(_split3(-big_m)) + [ones3], axis=0)
    dec_r = jnp.concatenate([ones3] + list(_split3(src)), axis=0)
    head_of_row = lax.broadcasted_iota(jnp.int32, (6 * nh, t), 0) & (nh - 1)
    ones = jnp.ones((t, ML_V), BF16)
    nums, dens = [], []
    for h in range(nh):
        qk = slice(h * ML_QK, (h + 1) * ML_QK)
        va = jnp.concatenate([v[:, h * ML_V:(h + 1) * ML_V], ones], axis=1)
        s = lax.dot_general(q[:, qk], k[:, qk], _NT, preferred_element_type=F32)
        w_log = lax.dot_general(dec_l, jnp.where(head_of_row == h, dec_r, jnp.zeros_like(dec_r)), _TN,
                                preferred_element_type=F32)
        w = (jnp.where(mask, jnp.exp(w_log), 0.0) * s).astype(BF16)
        ct = ct_ref[h]
        r = (jnp.dot(w, va, preferred_element_type=F32)
             + jnp.dot(q_in[:, qk], ct.astype(BF16), preferred_element_type=F32))
        nums.append(r[:, :ML_V])
        dens.append(r[:, ML_V:])
        c_loc = lax.dot_general(k_end[:, qk], va, _TN, preferred_element_type=F32)
        ct_ref[h] = a_dec[h:h + 1] * ct + s_dec[h:h + 1] * c_loc
    num = jnp.concatenate(nums, axis=1)
    den = jnp.concatenate(dens, axis=1)
    o_ref[0] = (num / jnp.maximum(jnp.abs(den), floor)).astype(o_ref.dtype)
    m_ref[...] = jnp.broadcast_to(m_new, m_ref.shape)


def _mlstm_scan(q, k, v, grow, gate_b, n_ctx):
    bsz, n, _ = q.shape
    t = ML_CHUNK
    assert t & (t - 1) == 0
    nc = n // t
    ncc = n_ctx // t

    def back(i):
        return jnp.where(i < ncc, ncc - 1 - i, nc - 1 - (i - ncc))

    stats = _mlstm_gate_stats(grow, gate_b)
    fwd = lambda b, i: (b, i, 0)
    bwd = lambda b, i: (b, back(i), 0)
    specs = lambda tok, d: [
        pl.BlockSpec((1, t, ML_QKW), tok), pl.BlockSpec((1, t, ML_QKW), tok), pl.BlockSpec((1, t, D_MODEL), tok),
        pl.BlockSpec((1, 1, ML_STAT_ROWS, t), lambda b, i: (b, d, 0, tok(b, i)[1]))]
    out = jax.ShapeDtypeStruct((bsz, n, D_MODEL), BF16)
    return pl.pallas_call(
        functools.partial(_mlstm_kernel, t=t),
        grid=(bsz, nc),
        in_specs=specs(fwd, 0) + specs(bwd, 1),
        out_specs=[pl.BlockSpec((1, t, D_MODEL), fwd), pl.BlockSpec((1, t, D_MODEL), bwd)],
        out_shape=[out, out],
        scratch_shapes=[pltpu.VMEM((2, ML_HEADS, ML_QK, 2 * ML_V), F32), pltpu.VMEM((2, ML_HEADS, LANES), F32)],
        compiler_params=_cparams(("arbitrary", "arbitrary")),
    )(q, k, v, stats, q, k, v, stats)


def _diff_kernel(lam_ref, q_ref, kt_ref, va_ref, o_ref, acc_ref, m_ref, s_ref, smax_ref, *, tq, tk, n, n_ctx,
                 lambda_init):
    qi = pl.program_id(2)
    q = q_ref[0]
    lane = lax.broadcasted_iota(jnp.int32, q.shape, 1)
    zero = jnp.zeros_like(q)
    qs = (jnp.where(lane < DF_HD, q, zero), jnp.where(lane < DF_HD, zero, q))
    acc_ref[...] = jnp.zeros_like(acc_ref)
    m_ref[...] = jnp.full_like(m_ref, NEG_INF)

    def scores(start, size, slot):
        kt = kt_ref[0, :, pl.ds(start, size)]
        for j in range(2):
            s = jnp.dot(qs[j], kt, preferred_element_type=F32)
            s_ref[slot, j, :, :size] = s
            smax_ref[slot, j] = jnp.max(s, axis=1, keepdims=True)

    def accumulate(start, size, slot):
        va = va_ref[0, pl.ds(start, size), :]
        ss = [s_ref[slot, j, :, :size] for j in range(2)]
        m_old = [m_ref[j] for j in range(2)]
        m_new = [jnp.maximum(m_old[j], smax_ref[slot, j]) for j in range(2)]
        ps = [jnp.exp2((ss[j] - m_new[j]).astype(BF16)) for j in range(2)]
        pv = [jnp.dot(ps[j], va, preferred_element_type=F32) for j in range(2)]
        for j in range(2):
            acc_ref[j] = jnp.exp2(m_old[j] - m_new[j]) * acc_ref[j] + pv[j]
            m_ref[j] = m_new[j]

    is_ctx_tile = qi * tq < n_ctx

    @pl.when(is_ctx_tile)
    def _():
        scores(0, n_ctx, 0)
        accumulate(0, n_ctx, 0)

    @pl.when(jnp.logical_not(is_ctx_tile))
    def _():
        n_chunks = n // tk
        unroll = ATT_UNROLL if (n_chunks - 1) % ATT_UNROLL == 0 else 2
        scores(0, tk, 0)

        def body(i, carry):
            c0 = pl.multiple_of(unroll * i * tk, tk)
            for u in range(unroll):
                scores(c0 + (u + 1) * tk, tk, (u + 1) % 2)
                accumulate(c0 + u * tk, tk, u % 2)
            return carry
        lax.fori_loop(0, (n_chunks - 1) // unroll, body, 0)
        accumulate((n_chunks - 1) * tk, tk, 0)

    lam = lam_ref[...]
    lam_full = (jnp.exp(jnp.sum(lam[0:1] * lam[1:2], axis=1, keepdims=True))
                - jnp.exp(jnp.sum(lam[2:3] * lam[3:4], axis=1, keepdims=True)) + lambda_init)
    o1 = acc_ref[0, :, :DF_VD] / acc_ref[0, :, DF_VD:]
    o2 = acc_ref[1, :, :DF_VD] / acc_ref[1, :, DF_VD:]
    o_ref[0] = (o1 - lam_full * o2).astype(o_ref.dtype)


def _diff_attention(q, k, va, lam, lambda_init, n_ctx):
    bsz, n, _ = q.shape
    tq = ATT_Q_TILE
    tk = next(t for t in ATT_KV_TILES if n % t == 0 and (n // t) % 2 == 1)
    assert n_ctx % tq == 0 and n % tq == 0 and n_ctx <= tk
    kt = jnp.swapaxes(k, 1, 2)
    kern = functools.partial(_diff_kernel, tq=tq, tk=tk, n=n, n_ctx=n_ctx, lambda_init=lambda_init)
    once = pl.Buffered(1)
    return pl.pallas_call(
        kern,
        grid=(bsz, DF_HEADS, n // tq),
        in_specs=[
            pl.BlockSpec(lam.shape, lambda b, h, i: (0, 0)),
            pl.BlockSpec((1, tq, 2 * DF_HD), lambda b, h, i: (b, i, h)),
            pl.BlockSpec((1, 2 * DF_HD, n), lambda b, h, i: (b, h, 0), pipeline_mode=once),
            pl.BlockSpec((1, n, 2 * DF_VD), lambda b, h, i: (b, 0, h), pipeline_mode=once),
        ],
        out_specs=pl.BlockSpec((1, tq, DF_VD), lambda b, h, i: (b, i, h)),
        out_shape=jax.ShapeDtypeStruct((bsz, n, D_MODEL), BF16),
        scratch_shapes=[pltpu.VMEM((2, tq, 2 * DF_VD), F32), pltpu.VMEM((2, tq, 1), F32),
                        pltpu.VMEM((2, 2, tq, tk), F32), pltpu.VMEM((2, 2, tq, 1), F32)],
        compiler_params=_cparams(("arbitrary", "arbitrary", "arbitrary")),
    )(lam.astype(F32), q, kt, va)


def _swa_mask_table(tq, bw):
    t = jnp.arange(tq, dtype=jnp.int32)[:, None]
    s = jnp.arange(bw, dtype=jnp.int32)[None, :]

    def one(off, first_ok):
        ok = jnp.logical_and(s >= first_ok, jnp.abs(t - s + off) <= SW_WIN)
        return jnp.where(ok, 0.0, NEG_INF).astype(F32)
    return jnp.stack([one(SW_WIN, 0), one(SW_WIN, SW_WIN), one(2 * SW_WIN, 0), one(0, bw)])


def _swa_kernel(sink_ref, q_ref, kt_ref, va_ref, mask_ref, o_ref, *, tq, n, n_ctx):
    g = pl.program_id(1)
    i = pl.program_id(2)
    bw = tq + 2 * SW_WIN
    q = q_ref[0]
    qh = jnp.concatenate([q[:, u * SW_HD:(u + 1) * SW_HD] for u in range(SW_GROUP)], axis=0)
    start = pl.multiple_of(jnp.clip(i * tq - SW_WIN, 0, n - bw), SW_WIN)
    s_c = jnp.dot(qh, kt_ref[0, :, 0:n_ctx], preferred_element_type=F32)
    s_b = jnp.dot(qh, kt_ref[0, :, pl.ds(start, bw)], preferred_element_type=F32)
    case = jnp.where(i * tq < n_ctx, 3, jnp.where(i * tq - SW_WIN < n_ctx, 1,
                                                   jnp.where(i * tq - SW_WIN > n - bw, 2, 0)))
    s_b = s_b + jnp.tile(mask_ref[case], (SW_GROUP, 1))
    head = lax.broadcasted_iota(jnp.int32, (SW_GROUP * tq, 1), 0) // tq
    sink = jnp.zeros((SW_GROUP * tq, 1), F32)
    for u in range(SW_GROUP):
        sink = jnp.where(head == u, sink_ref[g * SW_GROUP + u] * LOG2E, sink)
    m = jnp.maximum(sink, jnp.maximum(jnp.max(s_c, axis=1, keepdims=True), jnp.max(s_b, axis=1, keepdims=True)))
    acc = (jnp.dot(jnp.exp2((s_c - m).astype(BF16)), va_ref[0, 0:n_ctx, :], preferred_element_type=F32)
           + jnp.dot(jnp.exp2((s_b - m).astype(BF16)), va_ref[0, pl.ds(start, bw), :],
                     preferred_element_type=F32))
    out = acc[:, :SW_HD] / (acc[:, SW_HD:] + jnp.exp2(sink - m))
    o_ref[0] = jnp.concatenate([out[u * tq:(u + 1) * tq] for u in range(SW_GROUP)], axis=1).astype(o_ref.dtype)


def _swa_attention(q, k, va, sinks, n_ctx):
    bsz, n, _ = q.shape
    tq = SWA_Q_TILE
    assert tq & (tq - 1) == 0 and n_ctx % tq == 0 and n % tq == 0 and tq % SW_WIN == 0
    bw = tq + 2 * SW_WIN
    assert n - bw >= n_ctx + tq
    gw = SW_GROUP * SW_HD
    kt = jnp.swapaxes(k, 1, 2)
    return pl.pallas_call(
        functools.partial(_swa_kernel, tq=tq, n=n, n_ctx=n_ctx),
        grid=(bsz, SW_KV, n // tq),
        in_specs=[
            pl.BlockSpec(memory_space=pltpu.SMEM),
            pl.BlockSpec((1, tq, gw), lambda b, g, i: (b, i, g)),
            pl.BlockSpec((1, SW_HD, n), lambda b, g, i: (b, g, 0)),
            pl.BlockSpec((1, n, 2 * SW_HD), lambda b, g, i: (b, 0, g)),
            _const_spec((4, tq, bw)),
        ],
        out_specs=pl.BlockSpec((1, tq, gw), lambda b, g, i: (b, i, g)),
        out_shape=jax.ShapeDtypeStruct((bsz, n, D_MODEL), BF16),
        compiler_params=_cparams(("arbitrary", "arbitrary", "arbitrary")),
    )(sinks.astype(F32), q, kt, va, _swa_mask_table(tq, bw))


def _head_norm(y, g, width):
    parts = []
    for h in range(y.shape[1] // width):
        yh = y[:, h * width:(h + 1) * width]
        parts.append(yh * lax.rsqrt(jnp.mean(yh * yh, axis=-1, keepdims=True) + EPS))
    return jnp.concatenate(parts, axis=1) * g


def _mixer_update(it, mode, n_ctx, tm, post_scale):
    x_ref, gate_ref, w_ref = next(it), next(it), next(it)
    if mode == "mlstm":
        hf_ref, hb_ref, og_ref, g_ref = next(it), next(it), next(it), next(it)
        y = _head_norm(hf_ref[0].astype(F32) + hb_ref[0].astype(F32), g_ref[...], ML_V)
        z = y * _sigmoid(og_ref[0].astype(F32))
    elif mode == "diff":
        a_ref, g_ref = next(it), next(it)
        z = _head_norm(a_ref[0].astype(F32), g_ref[...], DF_VD) * post_scale
    else:
        a_ref = next(it)
        z = a_ref[0]
    y = jnp.dot(z.astype(BF16), w_ref[...], preferred_element_type=F32)
    gate = _row_select(gate_ref[0], pl.program_id(1) * tm, tm, n_ctx)
    return x_ref[0] + gate * y


def _mixer_operands(x, mix, tm):
    mode, gate_pair, w_out, extra, _ = mix
    d = x.shape[2]
    tok = lambda b, t: (b, t, 0)
    in_specs = [pl.BlockSpec((1, tm, d), tok), pl.BlockSpec((1, 2, d), lambda b, t: (b, 0, 0)),
                _const_spec(w_out.shape)]
    args = [x, gate_pair, w_out]
    if mode == "mlstm":
        hf, hb, og, g = extra
        in_specs += [pl.BlockSpec((1, tm, d), tok), pl.BlockSpec((1, tm, d), tok), pl.BlockSpec((1, tm, d), tok),
                     pl.BlockSpec((1, d), lambda b, t: (0, 0))]
        args += [hf, hb, og, g.reshape(1, d)]
    elif mode == "diff":
        a, g = extra
        in_specs += [pl.BlockSpec((1, tm, d), tok), pl.BlockSpec((1, d), lambda b, t: (0, 0))]
        args += [a, g.reshape(1, d)]
    else:
        (a,) = extra
        in_specs += [pl.BlockSpec((1, tm, d), tok)]
        args += [a]
    return in_specs, args


def _dense_ffn_kernel(*refs, mix_mode, post_scale, n_ctx, tm, bounds):
    it = iter(refs)
    x = _mixer_update(it, mix_mode, n_ctx, tm, post_scale)
    a_ref, b_ref, gate_ref, wa_ref, wb_ref, wo_ref, o_ref, h_ref, g_ref = (next(it) for _ in range(9))
    row0 = pl.program_id(1) * tm
    h_ref[...] = _norm_mod(x, a_ref[0], b_ref[0], row0, n_ctx).astype(BF16)
    for c0, c1 in zip(bounds[:-1], bounds[1:]):
        a = jnp.dot(h_ref[...], wa_ref[:, c0:c1], preferred_element_type=F32)
        b = jnp.dot(h_ref[...], wb_ref[:, c0:c1], preferred_element_type=F32)
        g_ref[:, c0:c1] = (a * _sigmoid(a) * b).astype(BF16)
    y = jnp.dot(g_ref[...], wo_ref[...], preferred_element_type=F32)
    o_ref[0] = x + _row_select(gate_ref[0], row0, tm, n_ctx) * y


def _dense_ffn(x, mix, a_pair, b_pair, gate_pair, wa, wb, wo, n_ctx):
    bsz, n, d = x.shape
    tm = TOKEN_TILE
    mix_specs, mix_args = _mixer_operands(x, mix, tm)
    ff = wa.shape[1]
    half = (pl.cdiv(ff // MXU_WIDTH, 2) * MXU_WIDTH) if ff % MXU_WIDTH == 0 else ff
    bounds = (0, half, ff) if half < ff else (0, ff)
    tok = lambda b, t: (b, t, 0)
    pair = pl.BlockSpec((1, 2, d), lambda b, t: (b, 0, 0))
    kern = functools.partial(_dense_ffn_kernel, mix_mode=mix[0], post_scale=mix[4], n_ctx=n_ctx, tm=tm,
                             bounds=bounds)
    return pl.pallas_call(
        kern,
        grid=(bsz, n // tm),
        in_specs=mix_specs + [pair, pair, pair,
                              _const_spec(wa.shape), _const_spec(wb.shape), _const_spec(wo.shape)],
        out_specs=pl.BlockSpec((1, tm, d), tok),
        out_shape=jax.ShapeDtypeStruct((bsz, n, d), F32),
        scratch_shapes=[pltpu.VMEM((tm, d), BF16), pltpu.VMEM((tm, ff), BF16)],
        compiler_params=_cparams(("arbitrary", "arbitrary")),
    )(*mix_args, a_pair, b_pair, gate_pair, wa, wb, wo)


def _moe_route_kernel(*refs, mix_mode, post_scale, n_ctx, tm):
    it = iter(refs)
    x = _mixer_update(it, mix_mode, n_ctx, tm, post_scale)
    a_ref, b_ref, r_ref, x_ref, h_ref, g_ref, i_ref = (next(it) for _ in range(7))
    row0 = pl.program_id(1) * tm
    x_ref[0] = x
    h = _norm_mod(x, a_ref[0], b_ref[0], row0, n_ctx)
    h_ref[0] = h.astype(BF16)
    logits = lax.dot_general(r_ref[...], h, (((1,), (1,)), ((), ())), preferred_element_type=F32,
                             precision=lax.Precision.HIGHEST)
    e_idx = lax.broadcasted_iota(jnp.int32, logits.shape, 0)
    v1 = jnp.max(logits, axis=0, keepdims=True)
    i1 = jnp.min(jnp.where(logits == v1, e_idx, N_EXPERTS), axis=0, keepdims=True)
    rest = jnp.where(e_idx == i1, -jnp.inf, logits)
    v2 = jnp.max(rest, axis=0, keepdims=True)
    i2 = jnp.min(jnp.where(rest == v2, e_idx, N_EXPERTS), axis=0, keepdims=True)
    p2 = jnp.exp(v2 - v1)
    den = 1.0 + p2
    g_ref[0] = jnp.concatenate([1.0 / den, p2 / den], axis=0)
    i_ref[0] = jnp.concatenate([i1, i2], axis=0)


def _moe_route(x, mix, a_pair, b_pair, router_t, n_ctx):
    bsz, n, d = x.shape
    tm = TOKEN_TILE
    tok = lambda b, t: (b, t, 0)
    pair = pl.BlockSpec((1, 2, d), lambda b, t: (b, 0, 0))
    mix_specs, mix_args = _mixer_operands(x, mix, tm)
    return pl.pallas_call(
        functools.partial(_moe_route_kernel, mix_mode=mix[0], post_scale=mix[4], n_ctx=n_ctx, tm=tm),
        grid=(bsz, n // tm),
        in_specs=mix_specs + [pair, pair, _const_spec(router_t.shape)],
        out_specs=[pl.BlockSpec((1, tm, d), tok),
                   pl.BlockSpec((1, tm, d), tok),
                   pl.BlockSpec((1, TOP_K, tm), lambda b, t: (b, 0, t)),
                   pl.BlockSpec((1, TOP_K, tm), lambda b, t: (b, 0, t))],
        out_shape=[jax.ShapeDtypeStruct((bsz, n, d), F32),
                   jax.ShapeDtypeStruct((bsz, n, d), BF16),
                   jax.ShapeDtypeStruct((bsz, TOP_K, n), F32),
                   jax.ShapeDtypeStruct((bsz, TOP_K, n), jnp.int32)],
        compiler_params=_cparams(("arbitrary", "arbitrary")),
    )(*mix_args, a_pair, b_pair, router_t)


def _moe_expert_kernel(te_ref, nu_ref, x_ref, wi_ref, wo_ref, o_ref, g_ref, *, ff, chunk):
    i = pl.program_id(0)

    @pl.when(i < nu_ref[0])
    def _():
        x = x_ref[...]
        for c0 in range(0, ff, chunk):
            a = jnp.dot(x, wi_ref[0, :, c0:c0 + chunk], preferred_element_type=F32)
            b = jnp.dot(x, wi_ref[0, :, ff + c0:ff + c0 + chunk], preferred_element_type=F32)
            g_ref[:, c0:c0 + chunk] = (a * _sigmoid(a) * b).astype(BF16)
        o_ref[...] = jnp.dot(g_ref[...], wo_ref[0], preferred_element_type=F32).astype(o_ref.dtype)

    @pl.when(i >= nu_ref[0])
    def _():
        o_ref[...] = jnp.zeros_like(o_ref)


def _moe_experts(xs, w_in, w_out, tile_expert, n_used):
    p, d = xs.shape
    tm = MOE_TILE
    ff = w_out.shape[1]
    chunk = MOE_FF_CHUNK
    assert ff % chunk == 0 and chunk % LANES == 0
    nt = p // tm

    def teff(i, nu):
        return jnp.minimum(i, jnp.maximum(nu[0] - 1, 0))

    grid_spec = pltpu.PrefetchScalarGridSpec(
        num_scalar_prefetch=2,
        grid=(nt,),
        in_specs=[
            pl.BlockSpec((tm, d), lambda i, te, nu: (teff(i, nu), 0)),
            pl.BlockSpec((1, d, 2 * ff), lambda i, te, nu: (te[teff(i, nu)], 0, 0),
                         pipeline_mode=pl.Buffered(1)),
            pl.BlockSpec((1, ff, d), lambda i, te, nu: (te[teff(i, nu)], 0, 0),
                         pipeline_mode=pl.Buffered(1)),
        ],
        out_specs=pl.BlockSpec((tm, d), lambda i, te, nu: (i, 0)),
        scratch_shapes=[pltpu.VMEM((tm, ff), BF16)],
    )
    return pl.pallas_call(
        functools.partial(_moe_expert_kernel, ff=ff, chunk=chunk),
        grid_spec=grid_spec,
        out_shape=jax.ShapeDtypeStruct((p, d), BF16),
        compiler_params=_cparams(("arbitrary",)),
    )(tile_expert, n_used, xs, w_in, w_out)


def _moe_combine_kernel(*refs, n_ctx, tm, skip, final):
    it = iter(refs)
    x_ref, gate_ref, y0_ref, y1_ref, pk_ref = next(it), next(it), next(it), next(it), next(it)
    fg_ref = next(it) if final else None
    o_ref = next(it)
    pk = pk_ref[0]
    y = pk[:, 0:1] * y0_ref[0, 0].astype(F32) + pk[:, 1:2] * y1_ref[0, 0].astype(F32)
    x = x_ref[0] + _row_select(gate_ref[0], (pl.program_id(1) + skip) * tm, tm, n_ctx) * y
    if final:
        x = x * lax.rsqrt(jnp.mean(x * x, axis=-1, keepdims=True) + EPS) * fg_ref[...]
    o_ref[0] = x


def _moe_combine(x, gate_pair, y, slot_w, n_ctx, final_g=None):
    bsz, n, d = x.shape
    final = final_g is not None
    tm = n_ctx if final else TOKEN_TILE
    skip = n_ctx // tm if final else 0
    assert n % tm == 0
    tok = lambda b, t: (b, t + skip, 0)
    slot = lambda k: pl.BlockSpec((1, 1, tm, d), lambda b, t: (k, b, t + skip, 0))
    in_specs = [pl.BlockSpec((1, tm, d), tok), pl.BlockSpec((1, 2, d), lambda b, t: (b, 0, 0)),
                slot(0), slot(1), pl.BlockSpec((1, tm, TOP_K), tok)]
    args = [x, gate_pair, y, y, slot_w]
    if final:
        in_specs.append(pl.BlockSpec((1, d), lambda b, t: (0, 0)))
        args.append(final_g.reshape(1, d))
    return pl.pallas_call(
        functools.partial(_moe_combine_kernel, n_ctx=n_ctx, tm=tm, skip=skip, final=final),
        grid=(bsz, n // tm - skip),
        in_specs=in_specs,
        out_specs=pl.BlockSpec((1, tm, d), lambda b, t: (b, t, 0)),
        out_shape=jax.ShapeDtypeStruct((bsz, n - skip * tm, d), F32),
        compiler_params=_cparams(("arbitrary", "arbitrary")),
    )(*args)


def _moe_layer(x, mix, a_pair, b_pair, gate_pair, router, w_in_all, w_out_all, layer, n_ctx, final_g):
    bsz, n, d = x.shape
    tm = MOE_TILE
    ntok = bsz * n
    x, h2, gates, idx = _moe_route(x, mix, a_pair, b_pair, router.T, n_ctx)
    w_in_all, w_out_all, h2 = lax.optimization_barrier((w_in_all, w_out_all, h2))
    w_in = w_in_all[layer].astype(BF16)
    w_out = w_out_all[layer].astype(BF16)
    e_flat = jnp.moveaxis(idx, 1, 0).reshape(TOP_K * ntok)
    onehot = (e_flat[:, None] == jnp.arange(N_EXPERTS, dtype=jnp.int32)[None, :]).astype(jnp.int32)
    csum = jnp.cumsum(onehot, axis=0)
    counts = csum[-1]
    rank = jnp.sum(onehot * csum, axis=1) - 1
    padded = ((counts + tm - 1) // tm) * tm
    ends = jnp.cumsum(padded)
    starts = ends - padded
    dest = jnp.sum(onehot * starts[None, :], axis=1) + rank
    n_rows = TOP_K * ntok + N_EXPERTS * tm
    n_rows = ((n_rows + tm - 1) // tm) * tm
    tile_start = jnp.arange(n_rows // tm, dtype=jnp.int32) * tm
    tile_expert = jnp.minimum(jnp.sum((tile_start[:, None] >= ends[None, :]).astype(jnp.int32), axis=1),
                              N_EXPERTS - 1).astype(jnp.int32)
    n_used = (ends[-1] // tm).astype(jnp.int32).reshape(1)
    order = jnp.argsort(e_flat, stable=True).astype(jnp.int32)
    local = (tile_start - starts[tile_expert])[:, None] + jnp.arange(tm, dtype=jnp.int32)[None, :]
    pos = (jnp.cumsum(counts) - counts)[tile_expert][:, None] + local
    filler = jnp.arange(n_rows, dtype=jnp.int32).reshape(-1, tm) % (TOP_K * ntok)
    pos = jnp.where(local < counts[tile_expert][:, None], pos, filler).reshape(n_rows)
    src = order.at[pos].get(mode="promise_in_bounds") % ntok
    rows = lambda a, i: a.at[i].get(mode="promise_in_bounds")
    xs = rows(h2.reshape(ntok, d), src)
    ys = _moe_experts(xs, w_in, w_out, tile_expert, n_used)
    y = rows(ys, dest).reshape(TOP_K, bsz, n, d)
    return _moe_combine(x, gate_pair, y, jnp.swapaxes(gates, 1, 2), n_ctx, final_g)


def _rope_table(n_ctx, seq):
    pos = jnp.arange(seq)
    row = (pos // GRID_W).astype(F32)
    col = (pos % GRID_W).astype(F32)
    nf = DF_HD // 4
    inv = jnp.power(ROPE_BASE, -jnp.arange(nf, dtype=F32) / nf)
    ang = jnp.concatenate([row[:, None] * inv, col[:, None] * inv], axis=-1)
    cos, sin = jnp.cos(ang), jnp.sin(ang)
    cos = jnp.concatenate([jnp.ones((n_ctx, 2 * nf), F32), cos], axis=0)
    sin = jnp.concatenate([jnp.zeros((n_ctx, 2 * nf), F32), sin], axis=0)
    c128 = jnp.tile(jnp.concatenate([cos, cos], axis=1), (1, 2))
    s128 = jnp.tile(jnp.concatenate([-sin, sin], axis=1), (1, 2))
    return jnp.concatenate([c128, s128], axis=1)


def _pair(v, bsz):
    return jnp.stack([v[:bsz], jnp.broadcast_to(v[bsz], (bsz, v.shape[1]))], axis=1)


def kernel(x, c, ctx, c_ctx, ada_w, ada_b, norm1, norm2, ml_w_in, ml_gate_b, ml_hnorm, ml_w_out,
           df_w_in, df_lam, df_hnorm, df_w_out, sw_w_in, sw_sinks, sw_w_out,
           ffn_w_in, ffn_w_out, moe_router, moe_w_in, moe_w_out, final_norm):
    bsz, seq, d = x.shape
    n_ctx = ctx.shape[1]
    depth = ada_w.shape[0]
    assert d == D_MODEL and n_ctx % ATT_Q_TILE == 0 and (n_ctx + seq) % TOKEN_TILE == 0

    xs = jnp.concatenate([ctx, x], axis=1)
    cond_rows = SUBLANES * ((bsz + 1 + SUBLANES - 1) // SUBLANES)
    cond = jnp.zeros((cond_rows, d), F32).at[:bsz].set(c).at[bsz].set(c_ctx)
    mods = _ada_mod(cond, ada_w, ada_b)
    cs = _rope_table(n_ctx, seq)

    for i in range(depth):
        m = [mods[i, :, k * d:(k + 1) * d] for k in range(6)]
        a1 = _pair(norm1[i][None, :] * (1.0 + m[1]), bsz)
        b1 = _pair(m[0], bsz)
        g1 = _pair(m[2], bsz)
        a2 = _pair(norm2[i][None, :] * (1.0 + m[4]), bsz)
        b2 = _pair(m[3], bsz)
        g2 = _pair(m[5], bsz)
        kind, j = i % N_MIXERS, i // N_MIXERS
        if kind == 0:
            w = ml_w_in[j]
            nmain = 2 * ML_QKW + 2 * D_MODEL
            w_main = w[:, :nmain].astype(BF16)
            wgt = w[:, nmain:].T.astype(BF16)
            segs = ((0, ML_QKW, False, 1.0, 0), (ML_QKW, 2 * ML_QKW, False, 1.0, 0),
                    (2 * ML_QKW, 2 * ML_QKW + d, False, 1.0, 0), (2 * ML_QKW + d, nmain, False, 1.0, 0))
            q, k, v, og, grow = _project(xs, a1, b1, w_main, segs, (BF16, BF16, BF16, BF16), n_ctx, wgt=wgt)
            hf, hb = _mlstm_scan(q, k, v, grow, ml_gate_b[j], n_ctx)
            mix = ("mlstm", g1, ml_w_out[j].astype(BF16), (hf, hb, og, ml_hnorm[j]), 1.0)
        elif kind == 1:
            lambda_init = 0.8 - 0.6 * math.exp(-0.3 * i)
            segs = ((0, DF_QKW, True, DF_HD ** -0.5 * LOG2E, 0), (DF_QKW, 2 * DF_QKW, True, 1.0, 0),
                    (2 * DF_QKW, 2 * DF_QKW + d, False, 1.0, DF_VD))
            q, k, v = _project(xs, a1, b1, df_w_in[j].astype(BF16), segs, (BF16, BF16, BF16), n_ctx, cs=cs)
            att = _diff_attention(q, k, v, df_lam[j], lambda_init, n_ctx)
            mix = ("diff", g1, df_w_out[j].astype(BF16), (att, df_hnorm[j]), 1.0 - lambda_init)
        else:
            nq = SW_HEADS * SW_HD
            nkv = SW_KV * SW_HD
            segs = ((0, nq, True, SW_HD ** -0.5 * LOG2E, 0), (nq, nq + nkv, True, 1.0, 0),
                    (nq + nkv, nq + 2 * nkv, False, 1.0, SW_HD))
            q, k, v = _project(xs, a1, b1, sw_w_in[j].astype(BF16), segs, (BF16, BF16, BF16), n_ctx, cs=cs)
            att = _swa_attention(q, k, v, sw_sinks[j], n_ctx)
            mix = ("plain", g1, sw_w_out[j].astype(BF16), (att,), 1.0)
        jf = i // 2
        last = i == depth - 1
        if i % 2 == 0:
            ff = ffn_w_out.shape[1]
            wi = ffn_w_in[jf].astype(BF16)
            xs = _dense_ffn(xs, mix, a2, b2, g2, wi[:, :ff], wi[:, ff:], ffn_w_out[jf].astype(BF16), n_ctx)
            if last:
                raise NotImplementedError("final norm is fused into the MoE combine")
        else:
            xs = _moe_layer(xs, mix, a2, b2, g2, moe_router[jf], moe_w_in, moe_w_out, jf, n_ctx,
                            final_norm if last else None)
    return xs
```

```python
import functools
import math

import jax
import jax.numpy as jnp
from jax import lax
from jax.experimental import pallas as pl
from jax.experimental.pallas import tpu as pltpu

F32 = jnp.float32
BF16 = jnp.bfloat16

D_MODEL = 1024
GRID_W = 64
EPS = 1e-6
NEG_INF = -1e30
ROPE_BASE = 10000.0
LOG2E = math.log2(math.e)
N_MIXERS = 3

ML_HEADS = 8
ML_V = D_MODEL // ML_HEADS
ML_QK = ML_V // 2
ML_QKW = ML_HEADS * ML_QK
ML_CHUNK = 256

DF_HD = 64
DF_HEADS = D_MODEL // (2 * DF_HD)
DF_VD = 2 * DF_HD
DF_QKW = 2 * DF_HEADS * DF_HD

SW_HD = 64
SW_HEADS = D_MODEL // SW_HD
SW_KV = 4
SW_GROUP = SW_HEADS // SW_KV
SW_WIN = 128

N_EXPERTS = 8
TOP_K = 2

LANES = 128
SUBLANES = 8
MXU_WIDTH = 256
V7X_VMEM_BYTES = 64 * 1024 * 1024
VMEM_LIMIT = V7X_VMEM_BYTES - 8 * 1024 * 1024

TOKEN_TILE = 640
ATT_Q_TILE = 256
ATT_KV_TILES = (3328, 1280, 256)
SWA_Q_TILE = 256
MOE_TILE = 512
MOE_FF_CHUNK = 1792


def _cparams(semantics):
    return pltpu.CompilerParams(dimension_semantics=semantics, vmem_limit_bytes=VMEM_LIMIT)


def _const_spec(shape):
    nd = len(shape)
    return pl.BlockSpec(shape, lambda *_: (0,) * nd, pipeline_mode=pl.Buffered(1))


def _sigmoid(x):
    return 1.0 / (1.0 + jnp.exp(-x))


def _log_sigmoid(x):
    return jnp.minimum(x, 0.0) - jnp.log(1.0 + jnp.exp(-jnp.abs(x)))


def _norm_mod(x, a_rows, b_rows, row0, n_ctx):
    xn = x * lax.rsqrt(jnp.mean(x * x, axis=-1, keepdims=True) + EPS)
    is_ctx = (row0 + lax.broadcasted_iota(jnp.int32, (x.shape[0], 1), 0)) < n_ctx
    a = jnp.where(is_ctx, a_rows[1:2], a_rows[0:1])
    b = jnp.where(is_ctx, b_rows[1:2], b_rows[0:1])
    return xn * a + b


def _row_select(rows, row0, n_rows, n_ctx):
    is_ctx = (row0 + lax.broadcasted_iota(jnp.int32, (n_rows, 1), 0)) < n_ctx
    return jnp.where(is_ctx, rows[1:2], rows[0:1])


def _ada_kernel(c_ref, w_ref, b_ref, o_ref):
    c = c_ref[...]
    s = c * _sigmoid(c)
    o_ref[0] = jnp.dot(s, w_ref[0], preferred_element_type=F32,
                       precision=lax.Precision.HIGHEST) + b_ref[0]


def _ada_mod(cond, ada_w, ada_b):
    depth, d, six_d = ada_w.shape
    rows = cond.shape[0]
    col = D_MODEL
    return pl.pallas_call(
        _ada_kernel,
        grid=(depth, six_d // col),
        in_specs=[
            pl.BlockSpec((rows, d), lambda i, j: (0, 0)),
            pl.BlockSpec((1, d, col), lambda i, j: (i, 0, j)),
            pl.BlockSpec((1, 1, col), lambda i, j: (i, 0, j)),
        ],
        out_specs=pl.BlockSpec((1, rows, col), lambda i, j: (i, 0, j)),
        out_shape=jax.ShapeDtypeStruct((depth, rows, six_d), F32),
        compiler_params=_cparams(("arbitrary", "arbitrary")),
    )(cond, ada_w, ada_b.reshape(depth, 1, six_d))


def _rope(r, cos, sin):
    w = r.shape[1]
    lane = lax.broadcasted_iota(jnp.int32, r.shape, 1)
    swapped = jnp.where((lane & 32) == 0, pltpu.roll(r, w - 32, 1), pltpu.roll(r, 32, 1))
    reps = w // LANES
    return r * jnp.tile(cos, (1, reps)) + swapped * jnp.tile(sin, (1, reps))


def _proj_kernel(*refs, segs, n_ctx, tm, has_rope, has_grow):
    it = iter(refs)
    x_ref, a_ref, b_ref, w_ref = next(it), next(it), next(it), next(it)
    cs_ref = next(it) if has_rope else None
    wgt_ref = next(it) if has_grow else None
    outs = [next(it) for _ in segs]
    grow_ref = next(it) if has_grow else None
    h_ref = next(it)

    row0 = pl.program_id(1) * tm
    h_ref[...] = _norm_mod(x_ref[0], a_ref[0], b_ref[0], row0, n_ctx).astype(BF16)
    for (c0, c1, rope, scale, ones_w), o_ref in zip(segs, outs):
        r = jnp.dot(h_ref[...], w_ref[:, c0:c1], preferred_element_type=F32)
        if rope:
            r = _rope(r, cs_ref[:, :LANES], cs_ref[:, LANES:])
        if scale != 1.0:
            r = r * scale
        r = r.astype(o_ref.dtype)
        if ones_w:
            ones = jnp.ones((tm, ones_w), o_ref.dtype)
            r = jnp.concatenate([piece for g in range((c1 - c0) // ones_w)
                                 for piece in (r[:, g * ones_w:(g + 1) * ones_w], ones)], axis=1)
        o_ref[0] = r
    if has_grow:
        grow_ref[0] = lax.dot_general(wgt_ref[...], h_ref[...], (((1,), (1,)), ((), ())),
                                      preferred_element_type=F32)


def _project(x, a_pair, b_pair, w, segs, out_dtypes, n_ctx, cs=None, wgt=None):
    bsz, n, d = x.shape
    tm = TOKEN_TILE
    assert n % tm == 0
    tok = lambda b, t: (b, t, 0)
    in_specs = [
        pl.BlockSpec((1, tm, d), tok),
        pl.BlockSpec((1, 2, d), lambda b, t: (b, 0, 0)),
        pl.BlockSpec((1, 2, d), lambda b, t: (b, 0, 0)),
        _const_spec(w.shape),
    ]
    args = [x, a_pair, b_pair, w]
    if cs is not None:
        in_specs.append(pl.BlockSpec((tm, 2 * LANES), lambda b, t: (t, 0)))
        args.append(cs)
    if wgt is not None:
        in_specs.append(_const_spec(wgt.shape))
        args.append(wgt)
    widths = [(c1 - c0) * (2 if ones_w else 1) for (c0, c1, _, _, ones_w) in segs]
    out_specs = [pl.BlockSpec((1, tm, w_), tok) for w_ in widths]
    out_shape = [jax.ShapeDtypeStruct((bsz, n, w_), dt) for w_, dt in zip(widths, out_dtypes)]
    if wgt is not None:
        out_specs.append(pl.BlockSpec((1, wgt.shape[0], tm), lambda b, t: (b, 0, t)))
        out_shape.append(jax.ShapeDtypeStruct((bsz, wgt.shape[0], n), F32))
    kern = functools.partial(_proj_kernel, segs=segs, n_ctx=n_ctx, tm=tm,
                             has_rope=cs is not None, has_grow=wgt is not None)
    return pl.pallas_call(
        kern,
        grid=(bsz, n // tm),
        in_specs=in_specs,
        out_specs=out_specs,
        out_shape=out_shape,
        scratch_shapes=[pltpu.VMEM((tm, d), BF16)],
        compiler_params=_cparams(("arbitrary", "arbitrary")),
    )(*args)


def _split3(x):
    hi = x.astype(BF16)
    r1 = x - hi.astype(F32)
    mid = r1.astype(BF16)
    lo = (r1 - mid.astype(F32)).astype(BF16)
    return hi, mid, lo


_TN = (((0,), (0,)), ((), ()))
_NT = (((1,), (1,)), ((), ()))


def _expand_heads(x, width):
    nh = x.shape[0]
    x3 = jnp.concatenate(_split3(x), axis=0)
    r = lax.broadcasted_iota(jnp.int32, (3 * nh, nh * width), 0)
    c = lax.broadcasted_iota(jnp.int32, (3 * nh, nh * width), 1)
    sel = jnp.where((r & (nh - 1)) == (c >> int(math.log2(width))), 1.0, 0.0).astype(BF16)
    return lax.dot_general(x3, sel, _TN, preferred_element_type=F32)


ML_STAT_ROWS = 6 * ML_HEADS


def _mlstm_gate_kernel(gr_ref, br_ref, st_ref, *, t, chunks):
    nh = ML_HEADS
    d = pl.program_id(1)
    fwd = d == 0
    sgn = 1 - 2 * d
    row = lax.broadcasted_iota(jnp.int32, (t, t), 0)
    col = lax.broadcasted_iota(jnp.int32, (t, t), 1)
    tri_t = jnp.where((row - col) * sgn <= 0, 1.0, 0.0).astype(BF16)
    lane = lax.broadcasted_iota(jnp.int32, (nh, t), 1)
    scan_pos = jnp.where(fwd, lane, t - 1 - lane)
    for c in range(chunks):
        cols = slice(c * t, (c + 1) * t)
        gr = gr_ref[0, :, cols] + br_ref[...]
        gr = jnp.where(fwd, gr[:2 * nh], gr[2 * nh:])
        ig = gr[:nh]
        fg = _log_sigmoid(gr[nh:])
        bs = jnp.dot(jnp.concatenate(_split3(fg), axis=0), tri_t, preferred_element_type=F32)
        b = bs[:nh] + bs[nh:2 * nh] + bs[2 * nh:]
        b_end = jnp.sum(fg, axis=1, keepdims=True)
        src = ig - b
        cmax = src
        shift = 1
        while shift < t:
            moved = jnp.where(fwd, pltpu.roll(cmax, shift, 1), pltpu.roll(cmax, t - shift, 1))
            cmax = jnp.maximum(cmax, jnp.where(scan_pos >= shift, moved, -jnp.inf))
            shift *= 2
        w_end = b_end + src
        m_loc = jnp.max(w_end, axis=1, keepdims=True)
        e_end = jnp.exp(w_end - m_loc)
        st_ref[0, 0, :, cols] = jnp.concatenate(
            [src, cmax, b, e_end, jnp.broadcast_to(b_end, (nh, t)), jnp.broadcast_to(m_loc, (nh, t))], axis=0)


def _mlstm_gate_stats(grow, gate_b):
    bsz, rows, n = grow.shape
    t = ML_CHUNK
    assert t & (t - 1) == 0
    width = TOKEN_TILE * 2 if n % (TOKEN_TILE * 2) == 0 else t
    bias_row = jnp.broadcast_to(gate_b.reshape(rows, 1).astype(F32), (rows, t))
    return pl.pallas_call(
        functools.partial(_mlstm_gate_kernel, t=t, chunks=width // t),
        grid=(bsz, 2, n // width),
        in_specs=[pl.BlockSpec((1, rows, width), lambda b, d, i: (b, 0, i)),
                  pl.BlockSpec((rows, t), lambda b, d, i: (0, 0))],
        out_specs=pl.BlockSpec((1, 1, ML_STAT_ROWS, width), lambda b, d, i: (b, d, 0, i)),
        out_shape=jax.ShapeDtypeStruct((bsz, 2, ML_STAT_ROWS, n), F32),
        compiler_params=_cparams(("arbitrary", "arbitrary", "arbitrary")),
    )(grow, bias_row)


def _mlstm_kernel(qf_ref, kf_ref, vf_ref, stf_ref, qb_ref, kb_ref, vb_ref, stb_ref, of_ref, ob_ref,
                  ct_ref, m_ref, *, t):
    @pl.when(pl.program_id(1) == 0)
    def _():
        ct_ref[...] = jnp.zeros_like(ct_ref)
        m_ref[...] = jnp.zeros_like(m_ref)

    _mlstm_chunk(qf_ref, kf_ref, vf_ref, stf_ref, of_ref, ct_ref.at[0], m_ref.at[0], t=t, forward=True)
    _mlstm_chunk(qb_ref, kb_ref, vb_ref, stb_ref, ob_ref, ct_ref.at[1], m_ref.at[1], t=t, forward=False)


def _mlstm_chunk(q_ref, k_ref, v_ref, st_ref, o_ref, ct_ref, m_ref, *, t, forward):
    nh = ML_HEADS
    row = lax.broadcasted_iota(jnp.int32, (t, t), 0)
    col = lax.broadcasted_iota(jnp.int32, (t, t), 1)
    mask = (col <= row) if forward else (col >= row)
    st = st_ref[0, 0]
    src, cmax, b, e_end = (st[j * nh:(j + 1) * nh] for j in range(4))
    b_end = st[4 * nh:5 * nh, 0:1]
    m_loc = st[5 * nh:6 * nh, 0:1]

    m_in = m_ref[:, 0:1]
    big_m = jnp.maximum(m_in, cmax)
    m_new = jnp.maximum(b_end + m_in, m_loc)
    a_dec = jnp.exp(b_end + m_in - m_new)
    s_dec = jnp.exp(m_loc - m_new)

    q = q_ref[0]
    k = k_ref[0] * (ML_QK ** -0.5)
    v = v_ref[0]
    q_in = (q.astype(F32) * _expand_heads(jnp.exp(m_in - big_m), ML_QK)).astype(BF16)
    k_end = (k.astype(F32) * _expand_heads(e_end, ML_QK)).astype(BF16)
    floor = _expand_heads(jnp.exp(-(b + big_m)), ML_V)

    ones3 = jnp.ones((3 * nh, t), BF16)
    dec_l = jnp.concatenate(list(_split3(-big_m)) + [ones3], axis=0)
    dec_r = jnp.concatenate([ones3] + list(_split3(src)), axis=0)
    head_of_row = lax.broadcasted_iota(jnp.int32, (6 * nh, t), 0) & (nh - 1)
    ones = jnp.ones((t, ML_V), BF16)
    nums, dens = [], []
    for h in range(nh):
        qk = slice(h * ML_QK, (h + 1) * ML_QK)
        va = jnp.concatenate([v[:, h * ML_V:(h + 1) * ML_V], ones], axis=1)
        s = lax.dot_general(q[:, qk], k[:, qk], _NT, preferred_element_type=F32)
        w_log = lax.dot_general(dec_l, jnp.where(head_of_row == h, dec_r, jnp.zeros_like(dec_r)), _TN,
                                preferred_element_type=F32)
        w = (jnp.where(mask, jnp.exp(w_log), 0.0) * s).astype(BF16)
        ct = ct_ref[h]
        r = (jnp.dot(w, va, preferred_element_type=F32)
             + jnp.dot(q_in[:, qk], ct.astype(BF16), preferred_element_type=F32))
        nums.append(r[:, :ML_V])
        dens.append(r[:, ML_V:])
        c_loc = lax.dot_general(k_end[:, qk], va, _TN, preferred_element_type=F32)
        ct_ref[h] = a_dec[h:h + 1] * ct + s_dec[h:h + 1] * c_loc
    num = jnp.concatenate(nums, axis=1)
    den = jnp.concatenate(dens, axis=1)
    o_ref[0] = (num / jnp.maximum(jnp.abs(den), floor)).astype(o_ref.dtype)
    m_ref[...] = jnp.broadcast_to(m_new, m_ref.shape)


def _mlstm_scan(q, k, v, grow, gate_b, n_ctx):
    bsz, n, _ = q.shape
    t = ML_CHUNK
    assert t & (t - 1) == 0
    nc = n // t
    ncc = n_ctx // t

    def back(i):
        return jnp.where(i < ncc, ncc - 1 - i, nc - 1 - (i - ncc))

    stats = _mlstm_gate_stats(grow, gate_b)
    fwd = lambda b, i: (b, i, 0)
    bwd = lambda b, i: (b, back(i), 0)
    specs = lambda tok, d: [
        pl.BlockSpec((1, t, ML_QKW), tok), pl.BlockSpec((1, t, ML_QKW), tok), pl.BlockSpec((1, t, D_MODEL), tok),
        pl.BlockSpec((1, 1, ML_STAT_ROWS, t), lambda b, i: (b, d, 0, tok(b, i)[1]))]
    out = jax.ShapeDtypeStruct((bsz, n, D_MODEL), BF16)
    return pl.pallas_call(
        functools.partial(_mlstm_kernel, t=t),
        grid=(bsz, nc),
        in_specs=specs(fwd, 0) + specs(bwd, 1),
        out_specs=[pl.BlockSpec((1, t, D_MODEL), fwd), pl.BlockSpec((1, t, D_MODEL), bwd)],
        out_shape=[out, out],
        scratch_shapes=[pltpu.VMEM((2, ML_HEADS, ML_QK, 2 * ML_V), F32), pltpu.VMEM((2, ML_HEADS, LANES), F32)],
        compiler_params=_cparams(("arbitrary", "arbitrary")),
    )(q, k, v, stats, q, k, v, stats)


def _diff_kernel(lam_ref, q_ref, kt_ref, va_ref, o_ref, acc_ref, m_ref, s_ref, smax_ref, *, tq, chunks, n_ctx,
                 lambda_init):
    qi = pl.program_id(2)
    q = q_ref[0]
    lane = lax.broadcasted_iota(jnp.int32, q.shape, 1)
    zero = jnp.zeros_like(q)
    qs = (jnp.where(lane < DF_HD, q, zero), jnp.where(lane < DF_HD, zero, q))
    acc_ref[...] = jnp.zeros_like(acc_ref)
    m_ref[...] = jnp.full_like(m_ref, NEG_INF)

    def scores(start, size, slot):
        kt = kt_ref[0, :, pl.ds(start, size)]
        for j in range(2):
            s = jnp.dot(qs[j], kt, preferred_element_type=F32)
            s_ref[slot, j, :, :size] = s
            smax_ref[slot, j] = jnp.max(s, axis=1, keepdims=True)

    def accumulate(start, size, slot):
        va = va_ref[0, pl.ds(start, size), :]
        ss = [s_ref[slot, j, :, :size] for j in range(2)]
        m_old = [m_ref[j] for j in range(2)]
        m_new = [jnp.maximum(m_old[j], smax_ref[slot, j]) for j in range(2)]
        ps = [jnp.exp2((ss[j] - m_new[j]).astype(BF16)) for j in range(2)]
        pv = [jnp.dot(ps[j], va, preferred_element_type=F32) for j in range(2)]
        for j in range(2):
            acc_ref[j] = jnp.exp2(m_old[j] - m_new[j]) * acc_ref[j] + pv[j]
            m_ref[j] = m_new[j]

    is_ctx_tile = qi * tq < n_ctx

    @pl.when(is_ctx_tile)
    def _():
        scores(0, n_ctx, 0)
        accumulate(0, n_ctx, 0)

    @pl.when(jnp.logical_not(is_ctx_tile))
    def _():
        scores(chunks[0][0], chunks[0][1], 0)
        for c, (start, size) in enumerate(chunks):
            if c + 1 < len(chunks):
                scores(chunks[c + 1][0], chunks[c + 1][1], (c + 1) % 2)
            accumulate(start, size, c % 2)

    lam = lam_ref[...]
    lam_full = (jnp.exp(jnp.sum(lam[0:1] * lam[1:2], axis=1, keepdims=True))
                - jnp.exp(jnp.sum(lam[2:3] * lam[3:4], axis=1, keepdims=True)) + lambda_init)
    o1 = acc_ref[0, :, :DF_VD] / acc_ref[0, :, DF_VD:]
    o2 = acc_ref[1, :, :DF_VD] / acc_ref[1, :, DF_VD:]
    o_ref[0] = (o1 - lam_full * o2).astype(o_ref.dtype)


def _diff_attention(q, k, va, lam, lambda_init, n_ctx):
    bsz, n, _ = q.shape
    tq = ATT_Q_TILE
    tk = next(t for t in ATT_KV_TILES if n % t == 0)
    chunks = tuple((c * tk, tk) for c in range(n // tk))
    assert n_ctx % tq == 0 and n % tq == 0 and n_ctx <= tk
    kt = jnp.swapaxes(k, 1, 2)
    kern = functools.partial(_diff_kernel, tq=tq, chunks=chunks, n_ctx=n_ctx, lambda_init=lambda_init)
    once = pl.Buffered(1)
    return pl.pallas_call(
        kern,
        grid=(bsz, DF_HEADS, n // tq),
        in_specs=[
            pl.BlockSpec(lam.shape, lambda b, h, i: (0, 0)),
            pl.BlockSpec((1, tq, 2 * DF_HD), lambda b, h, i: (b, i, h)),
            pl.BlockSpec((1, 2 * DF_HD, n), lambda b, h, i: (b, h, 0), pipeline_mode=once),
            pl.BlockSpec((1, n, 2 * DF_VD), lambda b, h, i: (b, 0, h), pipeline_mode=once),
        ],
        out_specs=pl.BlockSpec((1, tq, DF_VD), lambda b, h, i: (b, i, h)),
        out_shape=jax.ShapeDtypeStruct((bsz, n, D_MODEL), BF16),
        scratch_shapes=[pltpu.VMEM((2, tq, 2 * DF_VD), F32), pltpu.VMEM((2, tq, 1), F32),
                        pltpu.VMEM((2, 2, tq, tk), F32), pltpu.VMEM((2, 2, tq, 1), F32)],
        compiler_params=_cparams(("arbitrary", "arbitrary", "arbitrary")),
    )(lam.astype(F32), q, kt, va)


def _swa_mask_table(tq, bw):
    t = jnp.arange(tq, dtype=jnp.int32)[:, None]
    s = jnp.arange(bw, dtype=jnp.int32)[None, :]

    def one(off, first_ok):
        ok = jnp.logical_and(s >= first_ok, jnp.abs(t - s + off) <= SW_WIN)
        return jnp.where(ok, 0.0, NEG_INF).astype(F32)
    return jnp.stack([one(SW_WIN, 0), one(SW_WIN, SW_WIN), one(2 * SW_WIN, 0), one(0, bw)])


def _swa_kernel(sink_ref, q_ref, kt_ref, va_ref, mask_ref, o_ref, *, tq, n, n_ctx):
    g = pl.program_id(1)
    i = pl.program_id(2)
    bw = tq + 2 * SW_WIN
    q = q_ref[0]
    qh = jnp.concatenate([q[:, u * SW_HD:(u + 1) * SW_HD] for u in range(SW_GROUP)], axis=0)
    start = pl.multiple_of(jnp.clip(i * tq - SW_WIN, 0, n - bw), SW_WIN)
    s_c = jnp.dot(qh, kt_ref[0, :, 0:n_ctx], preferred_element_type=F32)
    s_b = jnp.dot(qh, kt_ref[0, :, pl.ds(start, bw)], preferred_element_type=F32)
    case = jnp.where(i * tq < n_ctx, 3, jnp.where(i * tq - SW_WIN < n_ctx, 1,
                                                   jnp.where(i * tq - SW_WIN > n - bw, 2, 0)))
    s_b = s_b + jnp.tile(mask_ref[case], (SW_GROUP, 1))
    head = lax.broadcasted_iota(jnp.int32, (SW_GROUP * tq, 1), 0) // tq
    sink = jnp.zeros((SW_GROUP * tq, 1), F32)
    for u in range(SW_GROUP):
        sink = jnp.where(head == u, sink_ref[g * SW_GROUP + u] * LOG2E, sink)
    m = jnp.maximum(sink, jnp.maximum(jnp.max(s_c, axis=1, keepdims=True), jnp.max(s_b, axis=1, keepdims=True)))
    acc = (jnp.dot(jnp.exp2((s_c - m).astype(BF16)), va_ref[0, 0:n_ctx, :], preferred_element_type=F32)
           + jnp.dot(jnp.exp2((s_b - m).astype(BF16)), va_ref[0, pl.ds(start, bw), :],
                     preferred_element_type=F32))
    out = acc[:, :SW_HD] / (acc[:, SW_HD:] + jnp.exp2(sink - m))
    o_ref[0] = jnp.concatenate([out[u * tq:(u + 1) * tq] for u in range(SW_GROUP)], axis=1).astype(o_ref.dtype)


def _swa_attention(q, k, va, sinks, n_ctx):
    bsz, n, _ = q.shape
    tq = SWA_Q_TILE
    assert tq & (tq - 1) == 0 and n_ctx % tq == 0 and n % tq == 0 and tq % SW_WIN == 0
    bw = tq + 2 * SW_WIN
    assert n - bw >= n_ctx + tq
    gw = SW_GROUP * SW_HD
    kt = jnp.swapaxes(k, 1, 2)
    return pl.pallas_call(
        functools.partial(_swa_kernel, tq=tq, n=n, n_ctx=n_ctx),
        grid=(bsz, SW_KV, n // tq),
        in_specs=[
            pl.BlockSpec(memory_space=pltpu.SMEM),
            pl.BlockSpec((1, tq, gw), lambda b, g, i: (b, i, g)),
            pl.BlockSpec((1, SW_HD, n), lambda b, g, i: (b, g, 0)),
            pl.BlockSpec((1, n, 2 * SW_HD), lambda b, g, i: (b, 0, g)),
            _const_spec((4, tq, bw)),
        ],
        out_specs=pl.BlockSpec((1, tq, gw), lambda b, g, i: (b, i, g)),
        out_shape=jax.ShapeDtypeStruct((bsz, n, D_MODEL), BF16),
        compiler_params=_cparams(("arbitrary", "arbitrary", "arbitrary")),
    )(sinks.astype(F32), q, kt, va, _swa_mask_table(tq, bw))


def _head_norm(y, g, width):
    parts = []
    for h in range(y.shape[1] // width):
        yh = y[:, h * width:(h + 1) * width]
        parts.append(yh * lax.rsqrt(jnp.mean(yh * yh, axis=-1, keepdims=True) + EPS))
    return jnp.concatenate(parts, axis=1) * g


def _mixer_update(it, mode, n_ctx, tm, post_scale):
    x_ref, gate_ref, w_ref = next(it), next(it), next(it)
    if mode == "mlstm":
        hf_ref, hb_ref, og_ref, g_ref = next(it), next(it), next(it), next(it)
        y = _head_norm(hf_ref[0].astype(F32) + hb_ref[0].astype(F32), g_ref[...], ML_V)
        z = y * _sigmoid(og_ref[0].astype(F32))
    elif mode == "diff":
        a_ref, g_ref = next(it), next(it)
        z = _head_norm(a_ref[0].astype(F32), g_ref[...], DF_VD) * post_scale
    else:
        a_ref = next(it)
        z = a_ref[0]
    y = jnp.dot(z.astype(BF16), w_ref[...], preferred_element_type=F32)
    gate = _row_select(gate_ref[0], pl.program_id(1) * tm, tm, n_ctx)
    return x_ref[0] + gate * y


def _mixer_operands(x, mix, tm):
    mode, gate_pair, w_out, extra, _ = mix
    d = x.shape[2]
    tok = lambda b, t: (b, t, 0)
    in_specs = [pl.BlockSpec((1, tm, d), tok), pl.BlockSpec((1, 2, d), lambda b, t: (b, 0, 0)),
                _const_spec(w_out.shape)]
    args = [x, gate_pair, w_out]
    if mode == "mlstm":
        hf, hb, og, g = extra
        in_specs += [pl.BlockSpec((1, tm, d), tok), pl.BlockSpec((1, tm, d), tok), pl.BlockSpec((1, tm, d), tok),
                     pl.BlockSpec((1, d), lambda b, t: (0, 0))]
        args += [hf, hb, og, g.reshape(1, d)]
    elif mode == "diff":
        a, g = extra
        in_specs += [pl.BlockSpec((1, tm, d), tok), pl.BlockSpec((1, d), lambda b, t: (0, 0))]
        args += [a, g.reshape(1, d)]
    else:
        (a,) = extra
        in_specs += [pl.BlockSpec((1, tm, d), tok)]
        args += [a]
    return in_specs, args


def _dense_ffn_kernel(*refs, mix_mode, post_scale, n_ctx, tm, bounds):
    it = iter(refs)
    x = _mixer_update(it, mix_mode, n_ctx, tm, post_scale)
    a_ref, b_ref, gate_ref, wa_ref, wb_ref, wo_ref, o_ref, h_ref, g_ref = (next(it) for _ in range(9))
    row0 = pl.program_id(1) * tm
    h_ref[...] = _norm_mod(x, a_ref[0], b_ref[0], row0, n_ctx).astype(BF16)
    for c0, c1 in zip(bounds[:-1], bounds[1:]):
        a = jnp.dot(h_ref[...], wa_ref[:, c0:c1], preferred_element_type=F32)
        b = jnp.dot(h_ref[...], wb_ref[:, c0:c1], preferred_element_type=F32)
        g_ref[:, c0:c1] = (a * _sigmoid(a) * b).astype(BF16)
    y = jnp.dot(g_ref[...], wo_ref[...], preferred_element_type=F32)
    o_ref[0] = x + _row_select(gate_ref[0], row0, tm, n_ctx) * y


def _dense_ffn(x, mix, a_pair, b_pair, gate_pair, wa, wb, wo, n_ctx):
    bsz, n, d = x.shape
    tm = TOKEN_TILE
    mix_specs, mix_args = _mixer_operands(x, mix, tm)
    ff = wa.shape[1]
    half = (pl.cdiv(ff // MXU_WIDTH, 2) * MXU_WIDTH) if ff % MXU_WIDTH == 0 else ff
    bounds = (0, half, ff) if half < ff else (0, ff)
    tok = lambda b, t: (b, t, 0)
    pair = pl.BlockSpec((1, 2, d), lambda b, t: (b, 0, 0))
    kern = functools.partial(_dense_ffn_kernel, mix_mode=mix[0], post_scale=mix[4], n_ctx=n_ctx, tm=tm,
                             bounds=bounds)
    return pl.pallas_call(
        kern,
        grid=(bsz, n // tm),
        in_specs=mix_specs + [pair, pair, pair,
                              _const_spec(wa.shape), _const_spec(wb.shape), _const_spec(wo.shape)],
        out_specs=pl.BlockSpec((1, tm, d), tok),
        out_shape=jax.ShapeDtypeStruct((bsz, n, d), F32),
        scratch_shapes=[pltpu.VMEM((tm, d), BF16), pltpu.VMEM((tm, ff), BF16)],
        compiler_params=_cparams(("arbitrary", "arbitrary")),
    )(*mix_args, a_pair, b_pair, gate_pair, wa, wb, wo)


def _moe_route_kernel(*refs, mix_mode, post_scale, n_ctx, tm):
    it = iter(refs)
    x = _mixer_update(it, mix_mode, n_ctx, tm, post_scale)
    a_ref, b_ref, r_ref, x_ref, h_ref, g_ref, i_ref = (next(it) for _ in range(7))
    row0 = pl.program_id(1) * tm
    x_ref[0] = x
    h = _norm_mod(x, a_ref[0], b_ref[0], row0, n_ctx)
    h_ref[0] = h.astype(BF16)
    logits = lax.dot_general(r_ref[...], h, (((1,), (1,)), ((), ())), preferred_element_type=F32,
                             precision=lax.Precision.HIGHEST)
    e_idx = lax.broadcasted_iota(jnp.int32, logits.shape, 0)
    v1 = jnp.max(logits, axis=0, keepdims=True)
    i1 = jnp.min(jnp.where(logits == v1, e_idx, N_EXPERTS), axis=0, keepdims=True)
    rest = jnp.where(e_idx == i1, -jnp.inf, logits)
    v2 = jnp.max(rest, axis=0, keepdims=True)
    i2 = jnp.min(jnp.where(rest == v2, e_idx, N_EXPERTS), axis=0, keepdims=True)
    p2 = jnp.exp(v2 - v1)
    den = 1.0 + p2
    g_ref[0] = jnp.concatenate([1.0 / den, p2 / den], axis=0)
    i_ref[0] = jnp.concatenate([i1, i2], axis=0)


def _moe_route(x, mix, a_pair, b_pair, router_t, n_ctx):
    bsz, n, d = x.shape
    tm = TOKEN_TILE
    tok = lambda b, t: (b, t, 0)
    pair = pl.BlockSpec((1, 2, d), lambda b, t: (b, 0, 0))
    mix_specs, mix_args = _mixer_operands(x, mix, tm)
    return pl.pallas_call(
        functools.partial(_moe_route_kernel, mix_mode=mix[0], post_scale=mix[4], n_ctx=n_ctx, tm=tm),
        grid=(bsz, n // tm),
        in_specs=mix_specs + [pair, pair, _const_spec(router_t.shape)],
        out_specs=[pl.BlockSpec((1, tm, d), tok),
                   pl.BlockSpec((1, tm, d), tok),
                   pl.BlockSpec((1, TOP_K, tm), lambda b, t: (b, 0, t)),
                   pl.BlockSpec((1, TOP_K, tm), lambda b, t: (b, 0, t))],
        out_shape=[jax.ShapeDtypeStruct((bsz, n, d), F32),
                   jax.ShapeDtypeStruct((bsz, n, d), BF16),
                   jax.ShapeDtypeStruct((bsz, TOP_K, n), F32),
                   jax.ShapeDtypeStruct((bsz, TOP_K, n), jnp.int32)],
        compiler_params=_cparams(("arbitrary", "arbitrary")),
    )(*mix_args, a_pair, b_pair, router_t)


def _moe_expert_kernel(te_ref, nu_ref, x_ref, wi_ref, wo_ref, o_ref, g_ref, *, ff, chunk):
    i = pl.program_id(0)

    @pl.when(i < nu_ref[0])
    def _():
        x = x_ref[...]
        for c0 in range(0, ff, chunk):
            a = jnp.dot(x, wi_ref[0, :, c0:c0 + chunk], preferred_element_type=F32)
            b = jnp.dot(x, wi_ref[0, :, ff + c0:ff + c0 + chunk], preferred_element_type=F32)
            g_ref[:, c0:c0 + chunk] = (a * _sigmoid(a) * b).astype(BF16)
        o_ref[...] = jnp.dot(g_ref[...], wo_ref[0], preferred_element_type=F32).astype(o_ref.dtype)

    @pl.when(i >= nu_ref[0])
    def _():
        o_ref[...] = jnp.zeros_like(o_ref)


def _moe_experts(xs, w_in, w_out, tile_expert, n_used):
    p, d = xs.shape
    tm = MOE_TILE
    ff = w_out.shape[1]
    chunk = MOE_FF_CHUNK
    assert ff % chunk == 0 and chunk % LANES == 0
    nt = p // tm

    def teff(i, nu):
        return jnp.minimum(i, jnp.maximum(nu[0] - 1, 0))

    grid_spec = pltpu.PrefetchScalarGridSpec(
        num_scalar_prefetch=2,
        grid=(nt,),
        in_specs=[
            pl.BlockSpec((tm, d), lambda i, te, nu: (teff(i, nu), 0)),
            pl.BlockSpec((1, d, 2 * ff), lambda i, te, nu: (te[teff(i, nu)], 0, 0),
                         pipeline_mode=pl.Buffered(1)),
            pl.BlockSpec((1, ff, d), lambda i, te, nu: (te[teff(i, nu)], 0, 0),
                         pipeline_mode=pl.Buffered(1)),
        ],
        out_specs=pl.BlockSpec((tm, d), lambda i, te, nu: (i, 0)),
        scratch_shapes=[pltpu.VMEM((tm, ff), BF16)],
    )
    return pl.pallas_call(
        functools.partial(_moe_expert_kernel, ff=ff, chunk=chunk),
        grid_spec=grid_spec,
        out_shape=jax.ShapeDtypeStruct((p, d), BF16),
        compiler_params=_cparams(("arbitrary",)),
    )(tile_expert, n_used, xs, w_in, w_out)


def _moe_combine_kernel(*refs, n_ctx, tm, skip, final):
    it = iter(refs)
    x_ref, gate_ref, y0_ref, y1_ref, pk_ref = next(it), next(it), next(it), next(it), next(it)
    fg_ref = next(it) if final else None
    o_ref = next(it)
    pk = pk_ref[0]
    y = pk[:, 0:1] * y0_ref[0, 0].astype(F32) + pk[:, 1:2] * y1_ref[0, 0].astype(F32)
    x = x_ref[0] + _row_select(gate_ref[0], (pl.program_id(1) + skip) * tm, tm, n_ctx) * y
    if final:
        x = x * lax.rsqrt(jnp.mean(x * x, axis=-1, keepdims=True) + EPS) * fg_ref[...]
    o_ref[0] = x


def _moe_combine(x, gate_pair, y, slot_w, n_ctx, final_g=None):
    bsz, n, d = x.shape
    final = final_g is not None
    tm = n_ctx if final else TOKEN_TILE
    skip = n_ctx // tm if final else 0
    assert n % tm == 0
    tok = lambda b, t: (b, t + skip, 0)
    slot = lambda k: pl.BlockSpec((1, 1, tm, d), lambda b, t: (k, b, t + skip, 0))
    in_specs = [pl.BlockSpec((1, tm, d), tok), pl.BlockSpec((1, 2, d), lambda b, t: (b, 0, 0)),
                slot(0), slot(1), pl.BlockSpec((1, tm, TOP_K), tok)]
    args = [x, gate_pair, y, y, slot_w]
    if final:
        in_specs.append(pl.BlockSpec((1, d), lambda b, t: (0, 0)))
        args.append(final_g.reshape(1, d))
    return pl.pallas_call(
        functools.partial(_moe_combine_kernel, n_ctx=n_ctx, tm=tm, skip=skip, final=final),
        grid=(bsz, n // tm - skip),
        in_specs=in_specs,
        out_specs=pl.BlockSpec((1, tm, d), lambda b, t: (b, t, 0)),
        out_shape=jax.ShapeDtypeStruct((bsz, n - skip * tm, d), F32),
        compiler_params=_cparams(("arbitrary", "arbitrary")),
    )(*args)


def _moe_layer(x, mix, a_pair, b_pair, gate_pair, router, w_in_all, w_out_all, layer, n_ctx, final_g):
    bsz, n, d = x.shape
    tm = MOE_TILE
    ntok = bsz * n
    x, h2, gates, idx = _moe_route(x, mix, a_pair, b_pair, router.T, n_ctx)
    w_in_all, w_out_all, h2 = lax.optimization_barrier((w_in_all, w_out_all, h2))
    w_in = w_in_all[layer].astype(BF16)
    w_out = w_out_all[layer].astype(BF16)
    e_flat = jnp.moveaxis(idx, 1, 0).reshape(TOP_K * ntok)
    onehot = (e_flat[:, None] == jnp.arange(N_EXPERTS, dtype=jnp.int32)[None, :]).astype(jnp.int32)
    csum = jnp.cumsum(onehot, axis=0)
    counts = csum[-1]
    rank = jnp.sum(onehot * csum, axis=1) - 1
    padded = ((counts + tm - 1) // tm) * tm
    ends = jnp.cumsum(padded)
    starts = ends - padded
    dest = jnp.sum(onehot * starts[None, :], axis=1) + rank
    n_rows = TOP_K * ntok + N_EXPERTS * tm
    n_rows = ((n_rows + tm - 1) // tm) * tm
    tile_start = jnp.arange(n_rows // tm, dtype=jnp.int32) * tm
    tile_expert = jnp.minimum(jnp.sum((tile_start[:, None] >= ends[None, :]).astype(jnp.int32), axis=1),
                              N_EXPERTS - 1).astype(jnp.int32)
    n_used = (ends[-1] // tm).astype(jnp.int32).reshape(1)
    order = jnp.argsort(e_flat, stable=True).astype(jnp.int32)
    local = (tile_start - starts[tile_expert])[:, None] + jnp.arange(tm, dtype=jnp.int32)[None, :]
    pos = (jnp.cumsum(counts) - counts)[tile_expert][:, None] + local
    filler = jnp.arange(n_rows, dtype=jnp.int32).reshape(-1, tm) % (TOP_K * ntok)
    pos = jnp.where(local < counts[tile_expert][:, None], pos, filler).reshape(n_rows)
    src = order.at[pos].get(mode="promise_in_bounds") % ntok
    rows = lambda a, i: a.at[i].get(mode="promise_in_bounds")
    xs = rows(h2.reshape(ntok, d), src)
    ys = _moe_experts(xs, w_in, w_out, tile_expert, n_used)
    y = rows(ys, dest).reshape(TOP_K, bsz, n, d)
    return _moe_combine(x, gate_pair, y, jnp.swapaxes(gates, 1, 2), n_ctx, final_g)


def _rope_table(n_ctx, seq):
    pos = jnp.arange(seq)
    row = (pos // GRID_W).astype(F32)
    col = (pos % GRID_W).astype(F32)
    nf = DF_HD // 4
    inv = jnp.power(ROPE_BASE, -jnp.arange(nf, dtype=F32) / nf)
    ang = jnp.concatenate([row[:, None] * inv, col[:, None] * inv], axis=-1)
    cos, sin = jnp.cos(ang), jnp.sin(ang)
    cos = jnp.concatenate([jnp.ones((n_ctx, 2 * nf), F32), cos], axis=0)
    sin = jnp.concatenate([jnp.zeros((n_ctx, 2 * nf), F32), sin], axis=0)
    c128 = jnp.tile(jnp.concatenate([cos, cos], axis=1), (1, 2))
    s128 = jnp.tile(jnp.concatenate([-sin, sin], axis=1), (1, 2))
    return jnp.concatenate([c128, s128], axis=1)


def _pair(v, bsz):
    return jnp.stack([v[:bsz], jnp.broadcast_to(v[bsz], (bsz, v.shape[1]))], axis=1)


def kernel(x, c, ctx, c_ctx, ada_w, ada_b, norm1, norm2, ml_w_in, ml_gate_b, ml_hnorm, ml_w_out,
           df_w_in, df_lam, df_hnorm, df_w_out, sw_w_in, sw_sinks, sw_w_out,
           ffn_w_in, ffn_w_out, moe_router, moe_w_in, moe_w_out, final_norm):
    bsz, seq, d = x.shape
    n_ctx = ctx.shape[1]
    depth = ada_w.shape[0]
    assert d == D_MODEL and n_ctx % ATT_Q_TILE == 0 and (n_ctx + seq) % TOKEN_TILE == 0

    xs = jnp.concatenate([ctx, x], axis=1)
    cond_rows = SUBLANES * ((bsz + 1 + SUBLANES - 1) // SUBLANES)
    cond = jnp.zeros((cond_rows, d), F32).at[:bsz].set(c).at[bsz].set(c_ctx)
    mods = _ada_mod(cond, ada_w, ada_b)
    cs = _rope_table(n_ctx, seq)

    for i in range(depth):
        m = [mods[i, :, k * d:(k + 1) * d] for k in range(6)]
        a1 = _pair(norm1[i][None, :] * (1.0 + m[1]), bsz)
        b1 = _pair(m[0], bsz)
        g1 = _pair(m[2], bsz)
        a2 = _pair(norm2[i][None, :] * (1.0 + m[4]), bsz)
        b2 = _pair(m[3], bsz)
        g2 = _pair(m[5], bsz)
        kind, j = i % N_MIXERS, i // N_MIXERS
        if kind == 0:
            w = ml_w_in[j]
            nmain = 2 * ML_QKW + 2 * D_MODEL
            w_main = w[:, :nmain].astype(BF16)
            wgt = w[:, nmain:].T.astype(BF16)
            segs = ((0, ML_QKW, False, 1.0, 0), (ML_QKW, 2 * ML_QKW, False, 1.0, 0),
                    (2 * ML_QKW, 2 * ML_QKW + d, False, 1.0, 0), (2 * ML_QKW + d, nmain, False, 1.0, 0))
            q, k, v, og, grow = _project(xs, a1, b1, w_main, segs, (BF16, BF16, BF16, BF16), n_ctx, wgt=wgt)
            hf, hb = _mlstm_scan(q, k, v, grow, ml_gate_b[j], n_ctx)
            mix = ("mlstm", g1, ml_w_out[j].astype(BF16), (hf, hb, og, ml_hnorm[j]), 1.0)
        elif kind == 1:
            lambda_init = 0.8 - 0.6 * math.exp(-0.3 * i)
            segs = ((0, DF_QKW, True, DF_HD ** -0.5 * LOG2E, 0), (DF_QKW, 2 * DF_QKW, True, 1.0, 0),
                    (2 * DF_QKW, 2 * DF_QKW + d, False, 1.0, DF_VD))
            q, k, v = _project(xs, a1, b1, df_w_in[j].astype(BF16), segs, (BF16, BF16, BF16), n_ctx, cs=cs)
            att = _diff_attention(q, k, v, df_lam[j], lambda_init, n_ctx)
            mix = ("diff", g1, df_w_out[j].astype(BF16), (att, df_hnorm[j]), 1.0 - lambda_init)
        else:
            nq = SW_HEADS * SW_HD
            nkv = SW_KV * SW_HD
            segs = ((0, nq, True, SW_HD ** -0.5 * LOG2E, 0), (nq, nq + nkv, True, 1.0, 0),
                    (nq + nkv, nq + 2 * nkv, False, 1.0, SW_HD))
            q, k, v = _project(xs, a1, b1, sw_w_in[j].astype(BF16), segs, (BF16, BF16, BF16), n_ctx, cs=cs)
            att = _swa_attention(q, k, v, sw_sinks[j], n_ctx)
            mix = ("plain", g1, sw_w_out[j].astype(BF16), (att,), 1.0)
        jf = i // 2
        last = i == depth - 1
        if i % 2 == 0:
            ff = ffn_w_out.shape[1]
            wi = ffn_w_in[jf].astype(BF16)
            xs = _dense_ffn(xs, mix, a2, b2, g2, wi[:, :ff], wi[:, ff:], ffn_w_out[jf].astype(BF16), n_ctx)
            if last:
                raise NotImplementedError("final norm is fused into the MoE combine")
        else:
            xs = _moe_layer(xs, mix, a2, b2, g2, moe_router[jf], moe_w_in, moe_w_out, jf, n_ctx,
                            final_norm if last else None)
    return xs
```

```python
import functools
import math

import jax
import jax.numpy as jnp
from jax import lax
from jax.experimental import pallas as pl
from jax.experimental.pallas import tpu as pltpu

F32 = jnp.float32
BF16 = jnp.bfloat16

D_MODEL = 1024
GRID_W = 64
EPS = 1e-6
NEG_INF = -1e30
ROPE_BASE = 10000.0
LOG2E = math.log2(math.e)
N_MIXERS = 3

ML_HEADS = 8
ML_V = D_MODEL // ML_HEADS
ML_QK = ML_V // 2
ML_QKW = ML_HEADS * ML_QK
ML_CHUNK = 256

DF_HD = 64
DF_HEADS = D_MODEL // (2 * DF_HD)
DF_VD = 2 * DF_HD
DF_QKW = 2 * DF_HEADS * DF_HD

SW_HD = 64
SW_HEADS = D_MODEL // SW_HD
SW_KV = 4
SW_GROUP = SW_HEADS // SW_KV
SW_WIN = 128

N_EXPERTS = 8
TOP_K = 2

LANES = 128
SUBLANES = 8
MXU_WIDTH = 256
V7X_VMEM_BYTES = 64 * 1024 * 1024
VMEM_LIMIT = V7X_VMEM_BYTES - 8 * 1024 * 1024

TOKEN_TILE = 640
ATT_Q_TILE = 256
ATT_KV_TILE = 6656
SWA_Q_TILE = 256
MOE_TILE = 512
MOE_FF_CHUNK = 1792


def _cparams(semantics):
    return pltpu.CompilerParams(dimension_semantics=semantics, vmem_limit_bytes=VMEM_LIMIT)


def _const_spec(shape):
    nd = len(shape)
    return pl.BlockSpec(shape, lambda *_: (0,) * nd, pipeline_mode=pl.Buffered(1))


def _sigmoid(x):
    return 1.0 / (1.0 + jnp.exp(-x))


def _log_sigmoid(x):
    return jnp.minimum(x, 0.0) - jnp.log(1.0 + jnp.exp(-jnp.abs(x)))


def _norm_mod(x, a_rows, b_rows, row0, n_ctx):
    xn = x * lax.rsqrt(jnp.mean(x * x, axis=-1, keepdims=True) + EPS)
    is_ctx = (row0 + lax.broadcasted_iota(jnp.int32, (x.shape[0], 1), 0)) < n_ctx
    a = jnp.where(is_ctx, a_rows[1:2], a_rows[0:1])
    b = jnp.where(is_ctx, b_rows[1:2], b_rows[0:1])
    return xn * a + b


def _row_select(rows, row0, n_rows, n_ctx):
    is_ctx = (row0 + lax.broadcasted_iota(jnp.int32, (n_rows, 1), 0)) < n_ctx
    return jnp.where(is_ctx, rows[1:2], rows[0:1])


def _ada_kernel(c_ref, w_ref, b_ref, o_ref):
    c = c_ref[...]
    s = c * _sigmoid(c)
    o_ref[0] = jnp.dot(s, w_ref[0], preferred_element_type=F32,
                       precision=lax.Precision.HIGHEST) + b_ref[0]


def _ada_mod(cond, ada_w, ada_b):
    depth, d, six_d = ada_w.shape
    rows = cond.shape[0]
    col = D_MODEL
    return pl.pallas_call(
        _ada_kernel,
        grid=(depth, six_d // col),
        in_specs=[
            pl.BlockSpec((rows, d), lambda i, j: (0, 0)),
            pl.BlockSpec((1, d, col), lambda i, j: (i, 0, j)),
            pl.BlockSpec((1, 1, col), lambda i, j: (i, 0, j)),
        ],
        out_specs=pl.BlockSpec((1, rows, col), lambda i, j: (i, 0, j)),
        out_shape=jax.ShapeDtypeStruct((depth, rows, six_d), F32),
        compiler_params=_cparams(("arbitrary", "arbitrary")),
    )(cond, ada_w, ada_b.reshape(depth, 1, six_d))


def _rope(r, cos, sin):
    w = r.shape[1]
    lane = lax.broadcasted_iota(jnp.int32, r.shape, 1)
    swapped = jnp.where((lane & 32) == 0, pltpu.roll(r, w - 32, 1), pltpu.roll(r, 32, 1))
    reps = w // LANES
    return r * jnp.tile(cos, (1, reps)) + swapped * jnp.tile(sin, (1, reps))


def _proj_kernel(*refs, segs, n_ctx, tm, has_rope, has_grow):
    it = iter(refs)
    x_ref, a_ref, b_ref, w_ref = next(it), next(it), next(it), next(it)
    cs_ref = next(it) if has_rope else None
    wgt_ref = next(it) if has_grow else None
    outs = [next(it) for _ in segs]
    grow_ref = next(it) if has_grow else None
    h_ref = next(it)

    row0 = pl.program_id(1) * tm
    h_ref[...] = _norm_mod(x_ref[0], a_ref[0], b_ref[0], row0, n_ctx).astype(BF16)
    for (c0, c1, rope, scale, ones_w), o_ref in zip(segs, outs):
        r = jnp.dot(h_ref[...], w_ref[:, c0:c1], preferred_element_type=F32)
        if rope:
            r = _rope(r, cs_ref[:, :LANES], cs_ref[:, LANES:])
        if scale != 1.0:
            r = r * scale
        r = r.astype(o_ref.dtype)
        if ones_w:
            ones = jnp.ones((tm, ones_w), o_ref.dtype)
            r = jnp.concatenate([piece for g in range((c1 - c0) // ones_w)
                                 for piece in (r[:, g * ones_w:(g + 1) * ones_w], ones)], axis=1)
        o_ref[0] = r
    if has_grow:
        grow_ref[0] = lax.dot_general(wgt_ref[...], h_ref[...], (((1,), (1,)), ((), ())),
                                      preferred_element_type=F32)


def _project(x, a_pair, b_pair, w, segs, out_dtypes, n_ctx, cs=None, wgt=None):
    bsz, n, d = x.shape
    tm = TOKEN_TILE
    assert n % tm == 0
    tok = lambda b, t: (b, t, 0)
    in_specs = [
        pl.BlockSpec((1, tm, d), tok),
        pl.BlockSpec((1, 2, d), lambda b, t: (b, 0, 0)),
        pl.BlockSpec((1, 2, d), lambda b, t: (b, 0, 0)),
        _const_spec(w.shape),
    ]
    args = [x, a_pair, b_pair, w]
    if cs is not None:
        in_specs.append(pl.BlockSpec((tm, 2 * LANES), lambda b, t: (t, 0)))
        args.append(cs)
    if wgt is not None:
        in_specs.append(_const_spec(wgt.shape))
        args.append(wgt)
    widths = [(c1 - c0) * (2 if ones_w else 1) for (c0, c1, _, _, ones_w) in segs]
    out_specs = [pl.BlockSpec((1, tm, w_), tok) for w_ in widths]
    out_shape = [jax.ShapeDtypeStruct((bsz, n, w_), dt) for w_, dt in zip(widths, out_dtypes)]
    if wgt is not None:
        out_specs.append(pl.BlockSpec((1, wgt.shape[0], tm), lambda b, t: (b, 0, t)))
        out_shape.append(jax.ShapeDtypeStruct((bsz, wgt.shape[0], n), F32))
    kern = functools.partial(_proj_kernel, segs=segs, n_ctx=n_ctx, tm=tm,
                             has_rope=cs is not None, has_grow=wgt is not None)
    return pl.pallas_call(
        kern,
        grid=(bsz, n // tm),
        in_specs=in_specs,
        out_specs=out_specs,
        out_shape=out_shape,
        scratch_shapes=[pltpu.VMEM((tm, d), BF16)],
        compiler_params=_cparams(("arbitrary", "arbitrary")),
    )(*args)


def _split3(x):
    hi = x.astype(BF16)
    r1 = x - hi.astype(F32)
    mid = r1.astype(BF16)
    lo = (r1 - mid.astype(F32)).astype(BF16)
    return hi, mid, lo


_TN = (((0,), (0,)), ((), ()))
_NT = (((1,), (1,)), ((), ()))


def _expand_heads(x, width):
    nh = x.shape[0]
    x3 = jnp.concatenate(_split3(x), axis=0)
    r = lax.broadcasted_iota(jnp.int32, (3 * nh, nh * width), 0)
    c = lax.broadcasted_iota(jnp.int32, (3 * nh, nh * width), 1)
    sel = jnp.where((r & (nh - 1)) == (c >> int(math.log2(width))), 1.0, 0.0).astype(BF16)
    return lax.dot_general(x3, sel, _TN, preferred_element_type=F32)


ML_STAT_ROWS = 6 * ML_HEADS


def _mlstm_gate_kernel(gr_ref, br_ref, st_ref, *, t, chunks):
    nh = ML_HEADS
    d = pl.program_id(1)
    fwd = d == 0
    sgn = 1 - 2 * d
    row = lax.broadcasted_iota(jnp.int32, (t, t), 0)
    col = lax.broadcasted_iota(jnp.int32, (t, t), 1)
    tri_t = jnp.where((row - col) * sgn <= 0, 1.0, 0.0).astype(BF16)
    lane = lax.broadcasted_iota(jnp.int32, (nh, t), 1)
    scan_pos = jnp.where(fwd, lane, t - 1 - lane)
    for c in range(chunks):
        cols = slice(c * t, (c + 1) * t)
        gr = gr_ref[0, :, cols] + br_ref[...]
        gr = jnp.where(fwd, gr[:2 * nh], gr[2 * nh:])
        ig = gr[:nh]
        fg = _log_sigmoid(gr[nh:])
        bs = jnp.dot(jnp.concatenate(_split3(fg), axis=0), tri_t, preferred_element_type=F32)
        b = bs[:nh] + bs[nh:2 * nh] + bs[2 * nh:]
        b_end = jnp.sum(fg, axis=1, keepdims=True)
        src = ig - b
        cmax = src
        shift = 1
        while shift < t:
            moved = jnp.where(fwd, pltpu.roll(cmax, shift, 1), pltpu.roll(cmax, t - shift, 1))
            cmax = jnp.maximum(cmax, jnp.where(scan_pos >= shift, moved, -jnp.inf))
            shift *= 2
        w_end = b_end + src
        m_loc = jnp.max(w_end, axis=1, keepdims=True)
        e_end = jnp.exp(w_end - m_loc)
        st_ref[0, 0, :, cols] = jnp.concatenate(
            [src, cmax, b, e_end, jnp.broadcast_to(b_end, (nh, t)), jnp.broadcast_to(m_loc, (nh, t))], axis=0)


def _mlstm_gate_stats(grow, gate_b):
    bsz, rows, n = grow.shape
    t = ML_CHUNK
    assert t & (t - 1) == 0
    width = TOKEN_TILE * 2 if n % (TOKEN_TILE * 2) == 0 else t
    bias_row = jnp.broadcast_to(gate_b.reshape(rows, 1).astype(F32), (rows, t))
    return pl.pallas_call(
        functools.partial(_mlstm_gate_kernel, t=t, chunks=width // t),
        grid=(bsz, 2, n // width),
        in_specs=[pl.BlockSpec((1, rows, width), lambda b, d, i: (b, 0, i)),
                  pl.BlockSpec((rows, t), lambda b, d, i: (0, 0))],
        out_specs=pl.BlockSpec((1, 1, ML_STAT_ROWS, width), lambda b, d, i: (b, d, 0, i)),
        out_shape=jax.ShapeDtypeStruct((bsz, 2, ML_STAT_ROWS, n), F32),
        compiler_params=_cparams(("arbitrary", "arbitrary", "arbitrary")),
    )(grow, bias_row)


def _mlstm_kernel(qf_ref, kf_ref, vf_ref, stf_ref, qb_ref, kb_ref, vb_ref, stb_ref, of_ref, ob_ref,
                  ct_ref, m_ref, *, t):
    @pl.when(pl.program_id(1) == 0)
    def _():
        ct_ref[...] = jnp.zeros_like(ct_ref)
        m_ref[...] = jnp.zeros_like(m_ref)

    _mlstm_chunk(qf_ref, kf_ref, vf_ref, stf_ref, of_ref, ct_ref.at[0], m_ref.at[0], t=t, forward=True)
    _mlstm_chunk(qb_ref, kb_ref, vb_ref, stb_ref, ob_ref, ct_ref.at[1], m_ref.at[1], t=t, forward=False)


def _mlstm_chunk(q_ref, k_ref, v_ref, st_ref, o_ref, ct_ref, m_ref, *, t, forward):
    nh = ML_HEADS
    row = lax.broadcasted_iota(jnp.int32, (t, t), 0)
    col = lax.broadcasted_iota(jnp.int32, (t, t), 1)
    mask = (col <= row) if forward else (col >= row)
    st = st_ref[0, 0]
    src, cmax, b, e_end = (st[j * nh:(j + 1) * nh] for j in range(4))
    b_end = st[4 * nh:5 * nh, 0:1]
    m_loc = st[5 * nh:6 * nh, 0:1]

    m_in = m_ref[:, 0:1]
    big_m = jnp.maximum(m_in, cmax)
    m_new = jnp.maximum(b_end + m_in, m_loc)
    a_dec = jnp.exp(b_end + m_in - m_new)
    s_dec = jnp.exp(m_loc - m_new)

    q = q_ref[0]
    k = k_ref[0] * (ML_QK ** -0.5)
    v = v_ref[0]
    q_in = (q.astype(F32) * _expand_heads(jnp.exp(m_in - big_m), ML_QK)).astype(BF16)
    k_end = (k.astype(F32) * _expand_heads(e_end, ML_QK)).astype(BF16)
    floor = _expand_heads(jnp.exp(-(b + big_m)), ML_V)

    ones3 = jnp.ones((3 * nh, t), BF16)
    dec_l = jnp.concatenate(list(_split3(-big_m)) + [ones3], axis=0)
    dec_r = jnp.concatenate([ones3] + list(_split3(src)), axis=0)
    head_of_row = lax.broadcasted_iota(jnp.int32, (6 * nh, t), 0) & (nh - 1)
    ones = jnp.ones((t, ML_V), BF16)
    nums, dens = [], []
    for h in range(nh):
        qk = slice(h * ML_QK, (h + 1) * ML_QK)
        va = jnp.concatenate([v[:, h * ML_V:(h + 1) * ML_V], ones], axis=1)
        s = lax.dot_general(q[:, qk], k[:, qk], _NT, preferred_element_type=F32)
        w_log = lax.dot_general(dec_l, jnp.where(head_of_row == h, dec_r, jnp.zeros_like(dec_r)), _TN,
                                preferred_element_type=F32)
        w = (jnp.where(mask, jnp.exp(w_log), 0.0) * s).astype(BF16)
        ct = ct_ref[h]
        r = (jnp.dot(w, va, preferred_element_type=F32)
             + jnp.dot(q_in[:, qk], ct.astype(BF16), preferred_element_type=F32))
        nums.append(r[:, :ML_V])
        dens.append(r[:, ML_V:])
        c_loc = lax.dot_general(k_end[:, qk], va, _TN, preferred_element_type=F32)
        ct_ref[h] = a_dec[h:h + 1] * ct + s_dec[h:h + 1] * c_loc
    num = jnp.concatenate(nums, axis=1)
    den = jnp.concatenate(dens, axis=1)
    o_ref[0] = (num / jnp.maximum(jnp.abs(den), floor)).astype(o_ref.dtype)
    m_ref[...] = jnp.broadcast_to(m_new, m_ref.shape)


def _mlstm_scan(q, k, v, grow, gate_b, n_ctx):
    bsz, n, _ = q.shape
    t = ML_CHUNK
    assert t & (t - 1) == 0
    nc = n // t
    ncc = n_ctx // t

    def back(i):
        return jnp.where(i < ncc, ncc - 1 - i, nc - 1 - (i - ncc))

    stats = _mlstm_gate_stats(grow, gate_b)
    fwd = lambda b, i: (b, i, 0)
    bwd = lambda b, i: (b, back(i), 0)
    specs = lambda tok, d: [
        pl.BlockSpec((1, t, ML_QKW), tok), pl.BlockSpec((1, t, ML_QKW), tok), pl.BlockSpec((1, t, D_MODEL), tok),
        pl.BlockSpec((1, 1, ML_STAT_ROWS, t), lambda b, i: (b, d, 0, tok(b, i)[1]))]
    out = jax.ShapeDtypeStruct((bsz, n, D_MODEL), BF16)
    return pl.pallas_call(
        functools.partial(_mlstm_kernel, t=t),
        grid=(bsz, nc),
        in_specs=specs(fwd, 0) + specs(bwd, 1),
        out_specs=[pl.BlockSpec((1, t, D_MODEL), fwd), pl.BlockSpec((1, t, D_MODEL), bwd)],
        out_shape=[out, out],
        scratch_shapes=[pltpu.VMEM((2, ML_HEADS, ML_QK, 2 * ML_V), F32), pltpu.VMEM((2, ML_HEADS, LANES), F32)],
        compiler_params=_cparams(("arbitrary", "arbitrary")),
    )(q, k, v, stats, q, k, v, stats)


def _diff_kernel(lam_ref, q_ref, kt_ref, va_ref, o_ref, acc_ref, m_ref, s_ref, smax_ref, *, tq, chunks, n_ctx,
                 lambda_init):
    qi = pl.program_id(2)
    q = q_ref[0]
    lane = lax.broadcasted_iota(jnp.int32, q.shape, 1)
    zero = jnp.zeros_like(q)
    qs = (jnp.where(lane < DF_HD, q, zero), jnp.where(lane < DF_HD, zero, q))
    acc_ref[...] = jnp.zeros_like(acc_ref)
    m_ref[...] = jnp.full_like(m_ref, NEG_INF)

    def scores(start, size, slot):
        kt = kt_ref[0, :, pl.ds(start, size)]
        for j in range(2):
            s = jnp.dot(qs[j], kt, preferred_element_type=F32)
            s_ref[slot, j, :, :size] = s
            smax_ref[slot, j] = jnp.max(s, axis=1, keepdims=True)

    def accumulate(start, size, slot):
        va = va_ref[0, pl.ds(start, size), :]
        ss = [s_ref[slot, j, :, :size] for j in range(2)]
        m_old = [m_ref[j] for j in range(2)]
        m_new = [jnp.maximum(m_old[j], smax_ref[slot, j]) for j in range(2)]
        ps = [jnp.exp2((ss[j] - m_new[j]).astype(BF16)) for j in range(2)]
        pv = [jnp.dot(ps[j], va, preferred_element_type=F32) for j in range(2)]
        for j in range(2):
            acc_ref[j] = jnp.exp2(m_old[j] - m_new[j]) * acc_ref[j] + pv[j]
            m_ref[j] = m_new[j]

    is_ctx_tile = qi * tq < n_ctx

    @pl.when(is_ctx_tile)
    def _():
        scores(0, n_ctx, 0)
        accumulate(0, n_ctx, 0)

    @pl.when(jnp.logical_not(is_ctx_tile))
    def _():
        scores(chunks[0][0], chunks[0][1], 0)
        for c, (start, size) in enumerate(chunks):
            if c + 1 < len(chunks):
                scores(chunks[c + 1][0], chunks[c + 1][1], (c + 1) % 2)
            accumulate(start, size, c % 2)

    lam = lam_ref[...]
    lam_full = (jnp.exp(jnp.sum(lam[0:1] * lam[1:2], axis=1, keepdims=True))
                - jnp.exp(jnp.sum(lam[2:3] * lam[3:4], axis=1, keepdims=True)) + lambda_init)
    o1 = acc_ref[0, :, :DF_VD] / acc_ref[0, :, DF_VD:]
    o2 = acc_ref[1, :, :DF_VD] / acc_ref[1, :, DF_VD:]
    o_ref[0] = (o1 - lam_full * o2).astype(o_ref.dtype)


def _diff_attention(q, k, va, lam, lambda_init, n_ctx):
    bsz, n, _ = q.shape
    tq = ATT_Q_TILE
    sizes = [ATT_KV_TILE] * (n // ATT_KV_TILE) + ([n % ATT_KV_TILE] if n % ATT_KV_TILE else [])
    chunks = tuple((sum(sizes[:c]), sizes[c]) for c in range(len(sizes)))
    tk = max(sizes)
    assert n_ctx % tq == 0 and n % tq == 0 and n_ctx <= sizes[0] and all(sz % MXU_WIDTH == 0 for sz in sizes)
    kt = jnp.swapaxes(k, 1, 2)
    kern = functools.partial(_diff_kernel, tq=tq, chunks=chunks, n_ctx=n_ctx, lambda_init=lambda_init)
    once = pl.Buffered(1)
    return pl.pallas_call(
        kern,
        grid=(bsz, DF_HEADS, n // tq),
        in_specs=[
            pl.BlockSpec(lam.shape, lambda b, h, i: (0, 0)),
            pl.BlockSpec((1, tq, 2 * DF_HD), lambda b, h, i: (b, i, h)),
            pl.BlockSpec((1, 2 * DF_HD, n), lambda b, h, i: (b, h, 0), pipeline_mode=once),
            pl.BlockSpec((1, n, 2 * DF_VD), lambda b, h, i: (b, 0, h), pipeline_mode=once),
        ],
        out_specs=pl.BlockSpec((1, tq, DF_VD), lambda b, h, i: (b, i, h)),
        out_shape=jax.ShapeDtypeStruct((bsz, n, D_MODEL), BF16),
        scratch_shapes=[pltpu.VMEM((2, tq, 2 * DF_VD), F32), pltpu.VMEM((2, tq, 1), F32),
                        pltpu.VMEM((2, 2, tq, tk), F32), pltpu.VMEM((2, 2, tq, 1), F32)],
        compiler_params=_cparams(("arbitrary", "arbitrary", "arbitrary")),
    )(lam.astype(F32), q, kt, va)


def _swa_mask_table(tq, bw):
    t = jnp.arange(tq, dtype=jnp.int32)[:, None]
    s = jnp.arange(bw, dtype=jnp.int32)[None, :]

    def one(off, first_ok):
        ok = jnp.logical_and(s >= first_ok, jnp.abs(t - s + off) <= SW_WIN)
        return jnp.where(ok, 0.0, NEG_INF).astype(F32)
    return jnp.stack([one(SW_WIN, 0), one(SW_WIN, SW_WIN), one(2 * SW_WIN, 0), one(0, bw)])


def _swa_kernel(sink_ref, q_ref, kt_ref, va_ref, mask_ref, o_ref, *, tq, n, n_ctx):
    g = pl.program_id(1)
    i = pl.program_id(2)
    bw = tq + 2 * SW_WIN
    q = q_ref[0]
    qh = jnp.concatenate([q[:, u * SW_HD:(u + 1) * SW_HD] for u in range(SW_GROUP)], axis=0)
    start = pl.multiple_of(jnp.clip(i * tq - SW_WIN, 0, n - bw), SW_WIN)
    s_c = jnp.dot(qh, kt_ref[0, :, 0:n_ctx], preferred_element_type=F32)
    s_b = jnp.dot(qh, kt_ref[0, :, pl.ds(start, bw)], preferred_element_type=F32)
    case = jnp.where(i * tq < n_ctx, 3, jnp.where(i * tq - SW_WIN < n_ctx, 1,
                                                   jnp.where(i * tq - SW_WIN > n - bw, 2, 0)))
    s_b = s_b + jnp.tile(mask_ref[case], (SW_GROUP, 1))
    head = lax.broadcasted_iota(jnp.int32, (SW_GROUP * tq, 1), 0) // tq
    sink = jnp.zeros((SW_GROUP * tq, 1), F32)
    for u in range(SW_GROUP):
        sink = jnp.where(head == u, sink_ref[g * SW_GROUP + u] * LOG2E, sink)
    m = jnp.maximum(sink, jnp.maximum(jnp.max(s_c, axis=1, keepdims=True), jnp.max(s_b, axis=1, keepdims=True)))
    acc = (jnp.dot(jnp.exp2((s_c - m).astype(BF16)), va_ref[0, 0:n_ctx, :], preferred_element_type=F32)
           + jnp.dot(jnp.exp2((s_b - m).astype(BF16)), va_ref[0, pl.ds(start, bw), :],
                     preferred_element_type=F32))
    out = acc[:, :SW_HD] / (acc[:, SW_HD:] + jnp.exp2(sink - m))
    o_ref[0] = jnp.concatenate([out[u * tq:(u + 1) * tq] for u in range(SW_GROUP)], axis=1).astype(o_ref.dtype)


def _swa_attention(q, k, va, sinks, n_ctx):
    bsz, n, _ = q.shape
    tq = SWA_Q_TILE
    assert tq & (tq - 1) == 0 and n_ctx % tq == 0 and n % tq == 0 and tq % SW_WIN == 0
    bw = tq + 2 * SW_WIN
    assert n - bw >= n_ctx + tq
    gw = SW_GROUP * SW_HD
    kt = jnp.swapaxes(k, 1, 2)
    return pl.pallas_call(
        functools.partial(_swa_kernel, tq=tq, n=n, n_ctx=n_ctx),
        grid=(bsz, SW_KV, n // tq),
        in_specs=[
            pl.BlockSpec(memory_space=pltpu.SMEM),
            pl.BlockSpec((1, tq, gw), lambda b, g, i: (b, i, g)),
            pl.BlockSpec((1, SW_HD, n), lambda b, g, i: (b, g, 0)),
            pl.BlockSpec((1, n, 2 * SW_HD), lambda b, g, i: (b, 0, g)),
            _const_spec((4, tq, bw)),
        ],
        out_specs=pl.BlockSpec((1, tq, gw), lambda b, g, i: (b, i, g)),
        out_shape=jax.ShapeDtypeStruct((bsz, n, D_MODEL), BF16),
        compiler_params=_cparams(("arbitrary", "arbitrary", "arbitrary")),
    )(sinks.astype(F32), q, kt, va, _swa_mask_table(tq, bw))


def _head_norm(y, g, width):
    parts = []
    for h in range(y.shape[1] // width):
        yh = y[:, h * width:(h + 1) * width]
        parts.append(yh * lax.rsqrt(jnp.mean(yh * yh, axis=-1, keepdims=True) + EPS))
    return jnp.concatenate(parts, axis=1) * g


def _mixer_update(it, mode, n_ctx, tm, post_scale):
    x_ref, gate_ref, w_ref = next(it), next(it), next(it)
    if mode == "mlstm":
        hf_ref, hb_ref, og_ref, g_ref = next(it), next(it), next(it), next(it)
        y = _head_norm(hf_ref[0].astype(F32) + hb_ref[0].astype(F32), g_ref[...], ML_V)
        z = y * _sigmoid(og_ref[0].astype(F32))
    elif mode == "diff":
        a_ref, g_ref = next(it), next(it)
        z = _head_norm(a_ref[0].astype(F32), g_ref[...], DF_VD) * post_scale
    else:
        a_ref = next(it)
        z = a_ref[0]
    y = jnp.dot(z.astype(BF16), w_ref[...], preferred_element_type=F32)
    gate = _row_select(gate_ref[0], pl.program_id(1) * tm, tm, n_ctx)
    return x_ref[0] + gate * y


def _mixer_operands(x, mix, tm):
    mode, gate_pair, w_out, extra, _ = mix
    d = x.shape[2]
    tok = lambda b, t: (b, t, 0)
    in_specs = [pl.BlockSpec((1, tm, d), tok), pl.BlockSpec((1, 2, d), lambda b, t: (b, 0, 0)),
                _const_spec(w_out.shape)]
    args = [x, gate_pair, w_out]
    if mode == "mlstm":
        hf, hb, og, g = extra
        in_specs += [pl.BlockSpec((1, tm, d), tok), pl.BlockSpec((1, tm, d), tok), pl.BlockSpec((1, tm, d), tok),
                     pl.BlockSpec((1, d), lambda b, t: (0, 0))]
        args += [hf, hb, og, g.reshape(1, d)]
    elif mode == "diff":
        a, g = extra
        in_specs += [pl.BlockSpec((1, tm, d), tok), pl.BlockSpec((1, d), lambda b, t: (0, 0))]
        args += [a, g.reshape(1, d)]
    else:
        (a,) = extra
        in_specs += [pl.BlockSpec((1, tm, d), tok)]
        args += [a]
    return in_specs, args


def _dense_ffn_kernel(*refs, mix_mode, post_scale, n_ctx, tm, bounds):
    it = iter(refs)
    x = _mixer_update(it, mix_mode, n_ctx, tm, post_scale)
    a_ref, b_ref, gate_ref, wa_ref, wb_ref, wo_ref, o_ref, h_ref, g_ref = (next(it) for _ in range(9))
    row0 = pl.program_id(1) * tm
    h_ref[...] = _norm_mod(x, a_ref[0], b_ref[0], row0, n_ctx).astype(BF16)
    for c0, c1 in zip(bounds[:-1], bounds[1:]):
        a = jnp.dot(h_ref[...], wa_ref[:, c0:c1], preferred_element_type=F32)
        b = jnp.dot(h_ref[...], wb_ref[:, c0:c1], preferred_element_type=F32)
        g_ref[:, c0:c1] = (a * _sigmoid(a) * b).astype(BF16)
    y = jnp.dot(g_ref[...], wo_ref[...], preferred_element_type=F32)
    o_ref[0] = x + _row_select(gate_ref[0], row0, tm, n_ctx) * y


def _dense_ffn(x, mix, a_pair, b_pair, gate_pair, wa, wb, wo, n_ctx):
    bsz, n, d = x.shape
    tm = TOKEN_TILE
    mix_specs, mix_args = _mixer_operands(x, mix, tm)
    ff = wa.shape[1]
    half = (pl.cdiv(ff // MXU_WIDTH, 2) * MXU_WIDTH) if ff % MXU_WIDTH == 0 else ff
    bounds = (0, half, ff) if half < ff else (0, ff)
    tok = lambda b, t: (b, t, 0)
    pair = pl.BlockSpec((1, 2, d), lambda b, t: (b, 0, 0))
    kern = functools.partial(_dense_ffn_kernel, mix_mode=mix[0], post_scale=mix[4], n_ctx=n_ctx, tm=tm,
                             bounds=bounds)
    return pl.pallas_call(
        kern,
        grid=(bsz, n // tm),
        in_specs=mix_specs + [pair, pair, pair,
                              _const_spec(wa.shape), _const_spec(wb.shape), _const_spec(wo.shape)],
        out_specs=pl.BlockSpec((1, tm, d), tok),
        out_shape=jax.ShapeDtypeStruct((bsz, n, d), F32),
        scratch_shapes=[pltpu.VMEM((tm, d), BF16), pltpu.VMEM((tm, ff), BF16)],
        compiler_params=_cparams(("arbitrary", "arbitrary")),
    )(*mix_args, a_pair, b_pair, gate_pair, wa, wb, wo)


def _moe_route_kernel(*refs, mix_mode, post_scale, n_ctx, tm):
    it = iter(refs)
    x = _mixer_update(it, mix_mode, n_ctx, tm, post_scale)
    a_ref, b_ref, r_ref, x_ref, h_ref, g_ref, i_ref = (next(it) for _ in range(7))
    row0 = pl.program_id(1) * tm
    x_ref[0] = x
    h = _norm_mod(x, a_ref[0], b_ref[0], row0, n_ctx)
    h_ref[0] = h.astype(BF16)
    logits = lax.dot_general(r_ref[...], h, (((1,), (1,)), ((), ())), preferred_element_type=F32,
                             precision=lax.Precision.HIGHEST)
    e_idx = lax.broadcasted_iota(jnp.int32, logits.shape, 0)
    v1 = jnp.max(logits, axis=0, keepdims=True)
    i1 = jnp.min(jnp.where(logits == v1, e_idx, N_EXPERTS), axis=0, keepdims=True)
    rest = jnp.where(e_idx == i1, -jnp.inf, logits)
    v2 = jnp.max(rest, axis=0, keepdims=True)
    i2 = jnp.min(jnp.where(rest == v2, e_idx, N_EXPERTS), axis=0, keepdims=True)
    p2 = jnp.exp(v2 - v1)
    den = 1.0 + p2
    g_ref[0] = jnp.concatenate([1.0 / den, p2 / den], axis=0)
    i_ref[0] = jnp.concatenate([i1, i2], axis=0)


def _moe_route(x, mix, a_pair, b_pair, router_t, n_ctx):
    bsz, n, d = x.shape
    tm = TOKEN_TILE
    tok = lambda b, t: (b, t, 0)
    pair = pl.BlockSpec((1, 2, d), lambda b, t: (b, 0, 0))
    mix_specs, mix_args = _mixer_operands(x, mix, tm)
    return pl.pallas_call(
        functools.partial(_moe_route_kernel, mix_mode=mix[0], post_scale=mix[4], n_ctx=n_ctx, tm=tm),
        grid=(bsz, n // tm),
        in_specs=mix_specs + [pair, pair, _const_spec(router_t.shape)],
        out_specs=[pl.BlockSpec((1, tm, d), tok),
                   pl.BlockSpec((1, tm, d), tok),
                   pl.BlockSpec((1, TOP_K, tm), lambda b, t: (b, 0, t)),
                   pl.BlockSpec((1, TOP_K, tm), lambda b, t: (b, 0, t))],
        out_shape=[jax.ShapeDtypeStruct((bsz, n, d), F32),
                   jax.ShapeDtypeStruct((bsz, n, d), BF16),
                   jax.ShapeDtypeStruct((bsz, TOP_K, n), F32),
                   jax.ShapeDtypeStruct((bsz, TOP_K, n), jnp.int32)],
        compiler_params=_cparams(("arbitrary", "arbitrary")),
    )(*mix_args, a_pair, b_pair, router_t)


def _moe_expert_kernel(te_ref, nu_ref, x_ref, wi_ref, wo_ref, o_ref, g_ref, *, ff, chunk):
    i = pl.program_id(0)

    @pl.when(i < nu_ref[0])
    def _():
        x = x_ref[...]
        for c0 in range(0, ff, chunk):
            a = jnp.dot(x, wi_ref[0, :, c0:c0 + chunk], preferred_element_type=F32)
            b = jnp.dot(x, wi_ref[0, :, ff + c0:ff + c0 + chunk], preferred_element_type=F32)
            g_ref[:, c0:c0 + chunk] = (a * _sigmoid(a) * b).astype(BF16)
        o_ref[...] = jnp.dot(g_ref[...], wo_ref[0], preferred_element_type=F32).astype(o_ref.dtype)

    @pl.when(i >= nu_ref[0])
    def _():
        o_ref[...] = jnp.zeros_like(o_ref)


def _moe_experts(xs, w_in, w_out, tile_expert, n_used):
    p, d = xs.shape
    tm = MOE_TILE
    ff = w_out.shape[1]
    chunk = MOE_FF_CHUNK
    assert ff % chunk == 0 and chunk % LANES == 0
    nt = p // tm

    def teff(i, nu):
        return jnp.minimum(i, jnp.maximum(nu[0] - 1, 0))

    grid_spec = pltpu.PrefetchScalarGridSpec(
        num_scalar_prefetch=2,
        grid=(nt,),
        in_specs=[
            pl.BlockSpec((tm, d), lambda i, te, nu: (teff(i, nu), 0)),
            pl.BlockSpec((1, d, 2 * ff), lambda i, te, nu: (te[teff(i, nu)], 0, 0)),
            pl.BlockSpec((1, ff, d), lambda i, te, nu: (te[teff(i, nu)], 0, 0)),
        ],
        out_specs=pl.BlockSpec((tm, d), lambda i, te, nu: (i, 0)),
        scratch_shapes=[pltpu.VMEM((tm, ff), BF16)],
    )
    return pl.pallas_call(
        functools.partial(_moe_expert_kernel, ff=ff, chunk=chunk),
        grid_spec=grid_spec,
        out_shape=jax.ShapeDtypeStruct((p, d), BF16),
        compiler_params=_cparams(("arbitrary",)),
    )(tile_expert, n_used, xs, w_in, w_out)


def _moe_combine_kernel(*refs, n_ctx, tm, skip, final):
    it = iter(refs)
    x_ref, gate_ref, y0_ref, y1_ref, pk_ref = next(it), next(it), next(it), next(it), next(it)
    fg_ref = next(it) if final else None
    o_ref = next(it)
    pk = pk_ref[0]
    y = pk[:, 0:1] * y0_ref[0, 0].astype(F32) + pk[:, 1:2] * y1_ref[0, 0].astype(F32)
    x = x_ref[0] + _row_select(gate_ref[0], (pl.program_id(1) + skip) * tm, tm, n_ctx) * y
    if final:
        x = x * lax.rsqrt(jnp.mean(x * x, axis=-1, keepdims=True) + EPS) * fg_ref[...]
    o_ref[0] = x


def _moe_combine(x, gate_pair, y, slot_w, n_ctx, final_g=None):
    bsz, n, d = x.shape
    final = final_g is not None
    tm = n_ctx if final else TOKEN_TILE
    skip = n_ctx // tm if final else 0
    assert n % tm == 0
    tok = lambda b, t: (b, t + skip, 0)
    slot = lambda k: pl.BlockSpec((1, 1, tm, d), lambda b, t: (k, b, t + skip, 0))
    in_specs = [pl.BlockSpec((1, tm, d), tok), pl.BlockSpec((1, 2, d), lambda b, t: (b, 0, 0)),
                slot(0), slot(1), pl.BlockSpec((1, tm, TOP_K), tok)]
    args = [x, gate_pair, y, y, slot_w]
    if final:
        in_specs.append(pl.BlockSpec((1, d), lambda b, t: (0, 0)))
        args.append(final_g.reshape(1, d))
    return pl.pallas_call(
        functools.partial(_moe_combine_kernel, n_ctx=n_ctx, tm=tm, skip=skip, final=final),
        grid=(bsz, n // tm - skip),
        in_specs=in_specs,
        out_specs=pl.BlockSpec((1, tm, d), lambda b, t: (b, t, 0)),
        out_shape=jax.ShapeDtypeStruct((bsz, n - skip * tm, d), F32),
        compiler_params=_cparams(("arbitrary", "arbitrary")),
    )(*args)


def _moe_layer(x, mix, a_pair, b_pair, gate_pair, router, w_in_all, w_out_all, layer, n_ctx, final_g):
    bsz, n, d = x.shape
    tm = MOE_TILE
    ntok = bsz * n
    x, h2, gates, idx = _moe_route(x, mix, a_pair, b_pair, router.T, n_ctx)
    w_in_all, w_out_all, h2 = lax.optimization_barrier((w_in_all, w_out_all, h2))
    w_in = w_in_all[layer].astype(BF16)
    w_out = w_out_all[layer].astype(BF16)
    e_flat = jnp.moveaxis(idx, 1, 0).reshape(TOP_K * ntok)
    onehot = (e_flat[:, None] == jnp.arange(N_EXPERTS, dtype=jnp.int32)[None, :]).astype(jnp.int32)
    csum = jnp.cumsum(onehot, axis=0)
    counts = csum[-1]
    rank = jnp.sum(onehot * csum, axis=1) - 1
    padded = ((counts + tm - 1) // tm) * tm
    ends = jnp.cumsum(padded)
    starts = ends - padded
    dest = jnp.sum(onehot * starts[None, :], axis=1) + rank
    n_rows = TOP_K * ntok + N_EXPERTS * tm
    n_rows = ((n_rows + tm - 1) // tm) * tm
    tile_start = jnp.arange(n_rows // tm, dtype=jnp.int32) * tm
    tile_expert = jnp.minimum(jnp.sum((tile_start[:, None] >= ends[None, :]).astype(jnp.int32), axis=1),
                              N_EXPERTS - 1).astype(jnp.int32)
    n_used = (ends[-1] // tm).astype(jnp.int32).reshape(1)
    order = jnp.argsort(e_flat, stable=True).astype(jnp.int32)
    local = (tile_start - starts[tile_expert])[:, None] + jnp.arange(tm, dtype=jnp.int32)[None, :]
    pos = (jnp.cumsum(counts) - counts)[tile_expert][:, None] + local
    filler = jnp.arange(n_rows, dtype=jnp.int32).reshape(-1, tm) % (TOP_K * ntok)
    pos = jnp.where(local < counts[tile_expert][:, None], pos, filler).reshape(n_rows)
    src = order.at[pos].get(mode="promise_in_bounds") % ntok
    rows = lambda a, i: a.at[i].get(mode="promise_in_bounds")
    xs = rows(h2.reshape(ntok, d), src)
    ys = _moe_experts(xs, w_in, w_out, tile_expert, n_used)
    y = rows(ys, dest).reshape(TOP_K, bsz, n, d)
    return _moe_combine(x, gate_pair, y, jnp.swapaxes(gates, 1, 2), n_ctx, final_g)


def _rope_table(n_ctx, seq):
    pos = jnp.arange(seq)
    row = (pos // GRID_W).astype(F32)
    col = (pos % GRID_W).astype(F32)
    nf = DF_HD // 4
    inv = jnp.power(ROPE_BASE, -jnp.arange(nf, dtype=F32) / nf)
    ang = jnp.concatenate([row[:, None] * inv, col[:, None] * inv], axis=-1)
    cos, sin = jnp.cos(ang), jnp.sin(ang)
    cos = jnp.concatenate([jnp.ones((n_ctx, 2 * nf), F32), cos], axis=0)
    sin = jnp.concatenate([jnp.zeros((n_ctx, 2 * nf), F32), sin], axis=0)
    c128 = jnp.tile(jnp.concatenate([cos, cos], axis=1), (1, 2))
    s128 = jnp.tile(jnp.concatenate([-sin, sin], axis=1), (1, 2))
    return jnp.concatenate([c128, s128], axis=1)


def _pair(v, bsz):
    return jnp.stack([v[:bsz], jnp.broadcast_to(v[bsz], (bsz, v.shape[1]))], axis=1)


def kernel(x, c, ctx, c_ctx, ada_w, ada_b, norm1, norm2, ml_w_in, ml_gate_b, ml_hnorm, ml_w_out,
           df_w_in, df_lam, df_hnorm, df_w_out, sw_w_in, sw_sinks, sw_w_out,
           ffn_w_in, ffn_w_out, moe_router, moe_w_in, moe_w_out, final_norm):
    bsz, seq, d = x.shape
    n_ctx = ctx.shape[1]
    depth = ada_w.shape[0]
    assert d == D_MODEL and n_ctx % ATT_Q_TILE == 0 and (n_ctx + seq) % TOKEN_TILE == 0

    xs = jnp.concatenate([ctx, x], axis=1)
    cond_rows = SUBLANES * ((bsz + 1 + SUBLANES - 1) // SUBLANES)
    cond = jnp.zeros((cond_rows, d), F32).at[:bsz].set(c).at[bsz].set(c_ctx)
    mods = _ada_mod(cond, ada_w, ada_b)
    cs = _rope_table(n_ctx, seq)

    for i in range(depth):
        m = [mods[i, :, k * d:(k + 1) * d] for k in range(6)]
        a1 = _pair(norm1[i][None, :] * (1.0 + m[1]), bsz)
        b1 = _pair(m[0], bsz)
        g1 = _pair(m[2], bsz)
        a2 = _pair(norm2[i][None, :] * (1.0 + m[4]), bsz)
        b2 = _pair(m[3], bsz)
        g2 = _pair(m[5], bsz)
        kind, j = i % N_MIXERS, i // N_MIXERS
        if kind == 0:
            w = ml_w_in[j]
            nmain = 2 * ML_QKW + 2 * D_MODEL
            w_main = w[:, :nmain].astype(BF16)
            wgt = w[:, nmain:].T.astype(BF16)
            segs = ((0, ML_QKW, False, 1.0, 0), (ML_QKW, 2 * ML_QKW, False, 1.0, 0),
                    (2 * ML_QKW, 2 * ML_QKW + d, False, 1.0, 0), (2 * ML_QKW + d, nmain, False, 1.0, 0))
            q, k, v, og, grow = _project(xs, a1, b1, w_main, segs, (BF16, BF16, BF16, BF16), n_ctx, wgt=wgt)
            hf, hb = _mlstm_scan(q, k, v, grow, ml_gate_b[j], n_ctx)
            mix = ("mlstm", g1, ml_w_out[j].astype(BF16), (hf, hb, og, ml_hnorm[j]), 1.0)
        elif kind == 1:
            lambda_init = 0.8 - 0.6 * math.exp(-0.3 * i)
            segs = ((0, DF_QKW, True, DF_HD ** -0.5 * LOG2E, 0), (DF_QKW, 2 * DF_QKW, True, 1.0, 0),
                    (2 * DF_QKW, 2 * DF_QKW + d, False, 1.0, DF_VD))
            q, k, v = _project(xs, a1, b1, df_w_in[j].astype(BF16), segs, (BF16, BF16, BF16), n_ctx, cs=cs)
            att = _diff_attention(q, k, v, df_lam[j], lambda_init, n_ctx)
            mix = ("diff", g1, df_w_out[j].astype(BF16), (att, df_hnorm[j]), 1.0 - lambda_init)
        else:
            nq = SW_HEADS * SW_HD
            nkv = SW_KV * SW_HD
            segs = ((0, nq, True, SW_HD ** -0.5 * LOG2E, 0), (nq, nq + nkv, True, 1.0, 0),
                    (nq + nkv, nq + 2 * nkv, False, 1.0, SW_HD))
            q, k, v = _project(xs, a1, b1, sw_w_in[j].astype(BF16), segs, (BF16, BF16, BF16), n_ctx, cs=cs)
            att = _swa_attention(q, k, v, sw_sinks[j], n_ctx)
            mix = ("plain", g1, sw_w_out[j].astype(BF16), (att,), 1.0)
        jf = i // 2
        last = i == depth - 1
        if i % 2 == 0:
            ff = ffn_w_out.shape[1]
            wi = ffn_w_in[jf].astype(BF16)
            xs = _dense_ffn(xs, mix, a2, b2, g2, wi[:, :ff], wi[:, ff:], ffn_w_out[jf].astype(BF16), n_ctx)
            if last:
                raise NotImplementedError("final norm is fused into the MoE combine")
        else:
            xs = _moe_layer(xs, mix, a2, b2, g2, moe_router[jf], moe_w_in, moe_w_out, jf, n_ctx,
                            final_norm if last else None)
    return xs
```

```python
import functools
import math

import jax
import jax.numpy as jnp
from jax import lax
from jax.experimental import pallas as pl
from jax.experimental.pallas import tpu as pltpu

F32 = jnp.float32
BF16 = jnp.bfloat16

D_MODEL = 1024
GRID_W = 64
EPS = 1e-6
NEG_INF = -1e30
ROPE_BASE = 10000.0
LOG2E = math.log2(math.e)
N_MIXERS = 3

ML_HEADS = 8
ML_V = D_MODEL // ML_HEADS
ML_QK = ML_V // 2
ML_QKW = ML_HEADS * ML_QK
ML_CHUNK = 256

DF_HD = 64
DF_HEADS = D_MODEL // (2 * DF_HD)
DF_VD = 2 * DF_HD
DF_QKW = 2 * DF_HEADS * DF_HD

SW_HD = 64
SW_HEADS = D_MODEL // SW_HD
SW_KV = 4
SW_GROUP = SW_HEADS // SW_KV
SW_WIN = 128

N_EXPERTS = 8
TOP_K = 2

LANES = 128
SUBLANES = 8
MXU_WIDTH = 256
V7X_VMEM_BYTES = 64 * 1024 * 1024
VMEM_LIMIT = V7X_VMEM_BYTES - 8 * 1024 * 1024

TOKEN_TILE = 640
ATT_Q_TILE = 256
ATT_KV_TILE = 8448
SWA_Q_TILE = 256
MOE_TILE = 512
MOE_FF_CHUNK = 1792


def _cparams(semantics):
    return pltpu.CompilerParams(dimension_semantics=semantics, vmem_limit_bytes=VMEM_LIMIT)


def _const_spec(shape):
    nd = len(shape)
    return pl.BlockSpec(shape, lambda *_: (0,) * nd, pipeline_mode=pl.Buffered(1))


def _sigmoid(x):
    return 1.0 / (1.0 + jnp.exp(-x))


def _log_sigmoid(x):
    return jnp.minimum(x, 0.0) - jnp.log(1.0 + jnp.exp(-jnp.abs(x)))


def _norm_mod(x, a_rows, b_rows, row0, n_ctx):
    xn = x * lax.rsqrt(jnp.mean(x * x, axis=-1, keepdims=True) + EPS)
    is_ctx = (row0 + lax.broadcasted_iota(jnp.int32, (x.shape[0], 1), 0)) < n_ctx
    a = jnp.where(is_ctx, a_rows[1:2], a_rows[0:1])
    b = jnp.where(is_ctx, b_rows[1:2], b_rows[0:1])
    return xn * a + b


def _row_select(rows, row0, n_rows, n_ctx):
    is_ctx = (row0 + lax.broadcasted_iota(jnp.int32, (n_rows, 1), 0)) < n_ctx
    return jnp.where(is_ctx, rows[1:2], rows[0:1])


def _ada_kernel(c_ref, w_ref, b_ref, o_ref):
    c = c_ref[...]
    s = c * _sigmoid(c)
    o_ref[0] = jnp.dot(s, w_ref[0], preferred_element_type=F32,
                       precision=lax.Precision.HIGHEST) + b_ref[0]


def _ada_mod(cond, ada_w, ada_b):
    depth, d, six_d = ada_w.shape
    rows = cond.shape[0]
    col = D_MODEL
    return pl.pallas_call(
        _ada_kernel,
        grid=(depth, six_d // col),
        in_specs=[
            pl.BlockSpec((rows, d), lambda i, j: (0, 0)),
            pl.BlockSpec((1, d, col), lambda i, j: (i, 0, j)),
            pl.BlockSpec((1, 1, col), lambda i, j: (i, 0, j)),
        ],
        out_specs=pl.BlockSpec((1, rows, col), lambda i, j: (i, 0, j)),
        out_shape=jax.ShapeDtypeStruct((depth, rows, six_d), F32),
        compiler_params=_cparams(("arbitrary", "arbitrary")),
    )(cond, ada_w, ada_b.reshape(depth, 1, six_d))


def _rope(r, cos, sin):
    w = r.shape[1]
    lane = lax.broadcasted_iota(jnp.int32, r.shape, 1)
    swapped = jnp.where((lane & 32) == 0, pltpu.roll(r, w - 32, 1), pltpu.roll(r, 32, 1))
    reps = w // LANES
    return r * jnp.tile(cos, (1, reps)) + swapped * jnp.tile(sin, (1, reps))


def _proj_kernel(*refs, segs, n_ctx, tm, has_rope, has_grow):
    it = iter(refs)
    x_ref, a_ref, b_ref, w_ref = next(it), next(it), next(it), next(it)
    cs_ref = next(it) if has_rope else None
    wgt_ref = next(it) if has_grow else None
    outs = [next(it) for _ in segs]
    grow_ref = next(it) if has_grow else None
    h_ref = next(it)

    row0 = pl.program_id(1) * tm
    h_ref[...] = _norm_mod(x_ref[0], a_ref[0], b_ref[0], row0, n_ctx).astype(BF16)
    for (c0, c1, rope, scale, ones_w), o_ref in zip(segs, outs):
        r = jnp.dot(h_ref[...], w_ref[:, c0:c1], preferred_element_type=F32)
        if rope:
            r = _rope(r, cs_ref[:, :LANES], cs_ref[:, LANES:])
        if scale != 1.0:
            r = r * scale
        r = r.astype(o_ref.dtype)
        if ones_w:
            ones = jnp.ones((tm, ones_w), o_ref.dtype)
            r = jnp.concatenate([piece for g in range((c1 - c0) // ones_w)
                                 for piece in (r[:, g * ones_w:(g + 1) * ones_w], ones)], axis=1)
        o_ref[0] = r
    if has_grow:
        grow_ref[0] = lax.dot_general(wgt_ref[...], h_ref[...], (((1,), (1,)), ((), ())),
                                      preferred_element_type=F32)


def _project(x, a_pair, b_pair, w, segs, out_dtypes, n_ctx, cs=None, wgt=None):
    bsz, n, d = x.shape
    tm = TOKEN_TILE
    assert n % tm == 0
    tok = lambda b, t: (b, t, 0)
    in_specs = [
        pl.BlockSpec((1, tm, d), tok),
        pl.BlockSpec((1, 2, d), lambda b, t: (b, 0, 0)),
        pl.BlockSpec((1, 2, d), lambda b, t: (b, 0, 0)),
        _const_spec(w.shape),
    ]
    args = [x, a_pair, b_pair, w]
    if cs is not None:
        in_specs.append(pl.BlockSpec((tm, 2 * LANES), lambda b, t: (t, 0)))
        args.append(cs)
    if wgt is not None:
        in_specs.append(_const_spec(wgt.shape))
        args.append(wgt)
    widths = [(c1 - c0) * (2 if ones_w else 1) for (c0, c1, _, _, ones_w) in segs]
    out_specs = [pl.BlockSpec((1, tm, w_), tok) for w_ in widths]
    out_shape = [jax.ShapeDtypeStruct((bsz, n, w_), dt) for w_, dt in zip(widths, out_dtypes)]
    if wgt is not None:
        out_specs.append(pl.BlockSpec((1, wgt.shape[0], tm), lambda b, t: (b, 0, t)))
        out_shape.append(jax.ShapeDtypeStruct((bsz, wgt.shape[0], n), F32))
    kern = functools.partial(_proj_kernel, segs=segs, n_ctx=n_ctx, tm=tm,
                             has_rope=cs is not None, has_grow=wgt is not None)
    return pl.pallas_call(
        kern,
        grid=(bsz, n // tm),
        in_specs=in_specs,
        out_specs=out_specs,
        out_shape=out_shape,
        scratch_shapes=[pltpu.VMEM((tm, d), BF16)],
        compiler_params=_cparams(("arbitrary", "arbitrary")),
    )(*args)


def _split3(x):
    hi = x.astype(BF16)
    r1 = x - hi.astype(F32)
    mid = r1.astype(BF16)
    lo = (r1 - mid.astype(F32)).astype(BF16)
    return hi, mid, lo


_TN = (((0,), (0,)), ((), ()))
_NT = (((1,), (1,)), ((), ()))


def _expand_heads(x, width):
    nh = x.shape[0]
    x3 = jnp.concatenate(_split3(x), axis=0)
    r = lax.broadcasted_iota(jnp.int32, (3 * nh, nh * width), 0)
    c = lax.broadcasted_iota(jnp.int32, (3 * nh, nh * width), 1)
    sel = jnp.where((r & (nh - 1)) == (c >> int(math.log2(width))), 1.0, 0.0).astype(BF16)
    return lax.dot_general(x3, sel, _TN, preferred_element_type=F32)


ML_STAT_ROWS = 6 * ML_HEADS


def _mlstm_gate_kernel(gr_ref, br_ref, st_ref, *, t, chunks):
    nh = ML_HEADS
    d = pl.program_id(1)
    fwd = d == 0
    sgn = 1 - 2 * d
    row = lax.broadcasted_iota(jnp.int32, (t, t), 0)
    col = lax.broadcasted_iota(jnp.int32, (t, t), 1)
    tri_t = jnp.where((row - col) * sgn <= 0, 1.0, 0.0).astype(BF16)
    lane = lax.broadcasted_iota(jnp.int32, (nh, t), 1)
    scan_pos = jnp.where(fwd, lane, t - 1 - lane)
    for c in range(chunks):
        cols = slice(c * t, (c + 1) * t)
        gr = gr_ref[0, :, cols] + br_ref[...]
        gr = jnp.where(fwd, gr[:2 * nh], gr[2 * nh:])
        ig = gr[:nh]
        fg = _log_sigmoid(gr[nh:])
        bs = jnp.dot(jnp.concatenate(_split3(fg), axis=0), tri_t, preferred_element_type=F32)
        b = bs[:nh] + bs[nh:2 * nh] + bs[2 * nh:]
        b_end = jnp.sum(fg, axis=1, keepdims=True)
        src = ig - b
        cmax = src
        shift = 1
        while shift < t:
            moved = jnp.where(fwd, pltpu.roll(cmax, shift, 1), pltpu.roll(cmax, t - shift, 1))
            cmax = jnp.maximum(cmax, jnp.where(scan_pos >= shift, moved, -jnp.inf))
            shift *= 2
        w_end = b_end + src
        m_loc = jnp.max(w_end, axis=1, keepdims=True)
        e_end = jnp.exp(w_end - m_loc)
        st_ref[0, 0, :, cols] = jnp.concatenate(
            [src, cmax, b, e_end, jnp.broadcast_to(b_end, (nh, t)), jnp.broadcast_to(m_loc, (nh, t))], axis=0)


def _mlstm_gate_stats(grow, gate_b):
    bsz, rows, n = grow.shape
    t = ML_CHUNK
    assert t & (t - 1) == 0
    width = TOKEN_TILE * 2 if n % (TOKEN_TILE * 2) == 0 else t
    bias_row = jnp.broadcast_to(gate_b.reshape(rows, 1).astype(F32), (rows, t))
    return pl.pallas_call(
        functools.partial(_mlstm_gate_kernel, t=t, chunks=width // t),
        grid=(bsz, 2, n // width),
        in_specs=[pl.BlockSpec((1, rows, width), lambda b, d, i: (b, 0, i)),
                  pl.BlockSpec((rows, t), lambda b, d, i: (0, 0))],
        out_specs=pl.BlockSpec((1, 1, ML_STAT_ROWS, width), lambda b, d, i: (b, d, 0, i)),
        out_shape=jax.ShapeDtypeStruct((bsz, 2, ML_STAT_ROWS, n), F32),
        compiler_params=_cparams(("arbitrary", "arbitrary", "arbitrary")),
    )(grow, bias_row)


def _mlstm_kernel(qf_ref, kf_ref, vf_ref, stf_ref, qb_ref, kb_ref, vb_ref, stb_ref, of_ref, ob_ref,
                  ct_ref, m_ref, *, t):
    @pl.when(pl.program_id(1) == 0)
    def _():
        ct_ref[...] = jnp.zeros_like(ct_ref)
        m_ref[...] = jnp.zeros_like(m_ref)

    _mlstm_chunk(qf_ref, kf_ref, vf_ref, stf_ref, of_ref, ct_ref.at[0], m_ref.at[0], t=t, forward=True)
    _mlstm_chunk(qb_ref, kb_ref, vb_ref, stb_ref, ob_ref, ct_ref.at[1], m_ref.at[1], t=t, forward=False)


def _mlstm_chunk(q_ref, k_ref, v_ref, st_ref, o_ref, ct_ref, m_ref, *, t, forward):
    nh = ML_HEADS
    row = lax.broadcasted_iota(jnp.int32, (t, t), 0)
    col = lax.broadcasted_iota(jnp.int32, (t, t), 1)
    mask = (col <= row) if forward else (col >= row)
    st = st_ref[0, 0]
    src, cmax, b, e_end = (st[j * nh:(j + 1) * nh] for j in range(4))
    b_end = st[4 * nh:5 * nh, 0:1]
    m_loc = st[5 * nh:6 * nh, 0:1]

    m_in = m_ref[:, 0:1]
    big_m = jnp.maximum(m_in, cmax)
    m_new = jnp.maximum(b_end + m_in, m_loc)
    a_dec = jnp.exp(b_end + m_in - m_new)
    s_dec = jnp.exp(m_loc - m_new)

    q = q_ref[0]
    k = k_ref[0] * (ML_QK ** -0.5)
    v = v_ref[0]
    q_in = (q.astype(F32) * _expand_heads(jnp.exp(m_in - big_m), ML_QK)).astype(BF16)
    k_end = (k.astype(F32) * _expand_heads(e_end, ML_QK)).astype(BF16)
    floor = _expand_heads(jnp.exp(-(b + big_m)), ML_V)

    ones3 = jnp.ones((3 * nh, t), BF16)
    dec_l = jnp.concatenate(list(_split3(-big_m)) + [ones3], axis=0)
    dec_r = jnp.concatenate([ones3] + list(_split3(src)), axis=0)
    head_of_row = lax.broadcasted_iota(jnp.int32, (6 * nh, t), 0) & (nh - 1)
    ones = jnp.ones((t, ML_V), BF16)
    nums, dens = [], []
    for h in range(nh):
        qk = slice(h * ML_QK, (h + 1) * ML_QK)
        va = jnp.concatenate([v[:, h * ML_V:(h + 1) * ML_V], ones], axis=1)
        s = lax.dot_general(q[:, qk], k[:, qk], _NT, preferred_element_type=F32)
        w_log = lax.dot_general(dec_l, jnp.where(head_of_row == h, dec_r, jnp.zeros_like(dec_r)), _TN,
                                preferred_element_type=F32)
        w = (jnp.where(mask, jnp.exp(w_log), 0.0) * s).astype(BF16)
        ct = ct_ref[h]
        r = (jnp.dot(w, va, preferred_element_type=F32)
             + jnp.dot(q_in[:, qk], ct.astype(BF16), preferred_element_type=F32))
        nums.append(r[:, :ML_V])
        dens.append(r[:, ML_V:])
        c_loc = lax.dot_general(k_end[:, qk], va, _TN, preferred_element_type=F32)
        ct_ref[h] = a_dec[h:h + 1] * ct + s_dec[h:h + 1] * c_loc
    num = jnp.concatenate(nums, axis=1)
    den = jnp.concatenate(dens, axis=1)
    o_ref[0] = (num / jnp.maximum(jnp.abs(den), floor)).astype(o_ref.dtype)
    m_ref[...] = jnp.broadcast_to(m_new, m_ref.shape)


def _mlstm_scan(q, k, v, grow, gate_b, n_ctx):
    bsz, n, _ = q.shape
    t = ML_CHUNK
    assert t & (t - 1) == 0
    nc = n // t
    ncc = n_ctx // t

    def back(i):
        return jnp.where(i < ncc, ncc - 1 - i, nc - 1 - (i - ncc))

    stats = _mlstm_gate_stats(grow, gate_b)
    fwd = lambda b, i: (b, i, 0)
    bwd = lambda b, i: (b, back(i), 0)
    specs = lambda tok, d: [
        pl.BlockSpec((1, t, ML_QKW), tok), pl.BlockSpec((1, t, ML_QKW), tok), pl.BlockSpec((1, t, D_MODEL), tok),
        pl.BlockSpec((1, 1, ML_STAT_ROWS, t), lambda b, i: (b, d, 0, tok(b, i)[1]))]
    out = jax.ShapeDtypeStruct((bsz, n, D_MODEL), BF16)
    return pl.pallas_call(
        functools.partial(_mlstm_kernel, t=t),
        grid=(bsz, nc),
        in_specs=specs(fwd, 0) + specs(bwd, 1),
        out_specs=[pl.BlockSpec((1, t, D_MODEL), fwd), pl.BlockSpec((1, t, D_MODEL), bwd)],
        out_shape=[out, out],
        scratch_shapes=[pltpu.VMEM((2, ML_HEADS, ML_QK, 2 * ML_V), F32), pltpu.VMEM((2, ML_HEADS, LANES), F32)],
        compiler_params=_cparams(("arbitrary", "arbitrary")),
    )(q, k, v, stats, q, k, v, stats)


def _diff_kernel(lam_ref, q_ref, kt_ref, va_ref, o_ref, acc_ref, m_ref, s_ref, smax_ref, *, tq, chunks, n_ctx,
                 lambda_init):
    qi = pl.program_id(2)
    q = q_ref[0]
    lane = lax.broadcasted_iota(jnp.int32, q.shape, 1)
    zero = jnp.zeros_like(q)
    qs = (jnp.where(lane < DF_HD, q, zero), jnp.where(lane < DF_HD, zero, q))
    acc_ref[...] = jnp.zeros_like(acc_ref)
    m_ref[...] = jnp.full_like(m_ref, NEG_INF)

    def scores(start, size, slot):
        kt = kt_ref[0, :, pl.ds(start, size)]
        for j in range(2):
            s = jnp.dot(qs[j], kt, preferred_element_type=F32)
            s_ref[slot, j, :, :size] = s
            smax_ref[slot, j] = jnp.max(s, axis=1, keepdims=True)

    def accumulate(start, size, slot):
        va = va_ref[0, pl.ds(start, size), :]
        ss = [s_ref[slot, j, :, :size] for j in range(2)]
        m_old = [m_ref[j] for j in range(2)]
        m_new = [jnp.maximum(m_old[j], smax_ref[slot, j]) for j in range(2)]
        ps = [jnp.exp2((ss[j] - m_new[j]).astype(BF16)) for j in range(2)]
        pv = [jnp.dot(ps[j], va, preferred_element_type=F32) for j in range(2)]
        for j in range(2):
            acc_ref[j] = jnp.exp2(m_old[j] - m_new[j]) * acc_ref[j] + pv[j]
            m_ref[j] = m_new[j]

    is_ctx_tile = qi * tq < n_ctx

    @pl.when(is_ctx_tile)
    def _():
        scores(0, n_ctx, 0)
        accumulate(0, n_ctx, 0)

    @pl.when(jnp.logical_not(is_ctx_tile))
    def _():
        scores(chunks[0][0], chunks[0][1], 0)
        for c, (start, size) in enumerate(chunks):
            if c + 1 < len(chunks):
                scores(chunks[c + 1][0], chunks[c + 1][1], (c + 1) % 2)
            accumulate(start, size, c % 2)

    lam = lam_ref[...]
    lam_full = (jnp.exp(jnp.sum(lam[0:1] * lam[1:2], axis=1, keepdims=True))
                - jnp.exp(jnp.sum(lam[2:3] * lam[3:4], axis=1, keepdims=True)) + lambda_init)
    o1 = acc_ref[0, :, :DF_VD] / acc_ref[0, :, DF_VD:]
    o2 = acc_ref[1, :, :DF_VD] / acc_ref[1, :, DF_VD:]
    o_ref[0] = (o1 - lam_full * o2).astype(o_ref.dtype)


def _diff_attention(q, k, va, lam, lambda_init, n_ctx):
    bsz, n, _ = q.shape
    tq = ATT_Q_TILE
    sizes = [ATT_KV_TILE] * (n // ATT_KV_TILE) + ([n % ATT_KV_TILE] if n % ATT_KV_TILE else [])
    chunks = tuple((sum(sizes[:c]), sizes[c]) for c in range(len(sizes)))
    tk = max(sizes)
    assert n_ctx % tq == 0 and n % tq == 0 and n_ctx <= sizes[0] and all(sz % MXU_WIDTH == 0 for sz in sizes)
    kt = jnp.swapaxes(k, 1, 2)
    kern = functools.partial(_diff_kernel, tq=tq, chunks=chunks, n_ctx=n_ctx, lambda_init=lambda_init)
    once = pl.Buffered(1)
    return pl.pallas_call(
        kern,
        grid=(bsz, DF_HEADS, n // tq),
        in_specs=[
            pl.BlockSpec(lam.shape, lambda b, h, i: (0, 0)),
            pl.BlockSpec((1, tq, 2 * DF_HD), lambda b, h, i: (b, i, h)),
            pl.BlockSpec((1, 2 * DF_HD, n), lambda b, h, i: (b, h, 0), pipeline_mode=once),
            pl.BlockSpec((1, n, 2 * DF_VD), lambda b, h, i: (b, 0, h), pipeline_mode=once),
        ],
        out_specs=pl.BlockSpec((1, tq, DF_VD), lambda b, h, i: (b, i, h)),
        out_shape=jax.ShapeDtypeStruct((bsz, n, D_MODEL), BF16),
        scratch_shapes=[pltpu.VMEM((2, tq, 2 * DF_VD), F32), pltpu.VMEM((2, tq, 1), F32),
                        pltpu.VMEM((2, 2, tq, tk), F32), pltpu.VMEM((2, 2, tq, 1), F32)],
        compiler_params=_cparams(("arbitrary", "arbitrary", "arbitrary")),
    )(lam.astype(F32), q, kt, va)


def _swa_mask_table(tq, bw):
    t = jnp.arange(tq, dtype=jnp.int32)[:, None]
    s = jnp.arange(bw, dtype=jnp.int32)[None, :]

    def one(off, first_ok):
        ok = jnp.logical_and(s >= first_ok, jnp.abs(t - s + off) <= SW_WIN)
        return jnp.where(ok, 0.0, NEG_INF).astype(F32)
    return jnp.stack([one(SW_WIN, 0), one(SW_WIN, SW_WIN), one(2 * SW_WIN, 0), one(0, bw)])


def _swa_kernel(sink_ref, q_ref, kt_ref, va_ref, mask_ref, o_ref, *, tq, n, n_ctx):
    g = pl.program_id(1)
    i = pl.program_id(2)
    bw = tq + 2 * SW_WIN
    q = q_ref[0]
    qh = jnp.concatenate([q[:, u * SW_HD:(u + 1) * SW_HD] for u in range(SW_GROUP)], axis=0)
    start = pl.multiple_of(jnp.clip(i * tq - SW_WIN, 0, n - bw), SW_WIN)
    s_c = jnp.dot(qh, kt_ref[0, :, 0:n_ctx], preferred_element_type=F32)
    s_b = jnp.dot(qh, kt_ref[0, :, pl.ds(start, bw)], preferred_element_type=F32)
    case = jnp.where(i * tq < n_ctx, 3, jnp.where(i * tq - SW_WIN < n_ctx, 1,
                                                   jnp.where(i * tq - SW_WIN > n - bw, 2, 0)))
    s_b = s_b + jnp.tile(mask_ref[case], (SW_GROUP, 1))
    head = lax.broadcasted_iota(jnp.int32, (SW_GROUP * tq, 1), 0) // tq
    sink = jnp.zeros((SW_GROUP * tq, 1), F32)
    for u in range(SW_GROUP):
        sink = jnp.where(head == u, sink_ref[g * SW_GROUP + u] * LOG2E, sink)
    m = jnp.maximum(sink, jnp.maximum(jnp.max(s_c, axis=1, keepdims=True), jnp.max(s_b, axis=1, keepdims=True)))
    acc = (jnp.dot(jnp.exp2((s_c - m).astype(BF16)), va_ref[0, 0:n_ctx, :], preferred_element_type=F32)
           + jnp.dot(jnp.exp2((s_b - m).astype(BF16)), va_ref[0, pl.ds(start, bw), :],
                     preferred_element_type=F32))
    out = acc[:, :SW_HD] / (acc[:, SW_HD:] + jnp.exp2(sink - m))
    o_ref[0] = jnp.concatenate([out[u * tq:(u + 1) * tq] for u in range(SW_GROUP)], axis=1).astype(o_ref.dtype)


def _swa_attention(q, k, va, sinks, n_ctx):
    bsz, n, _ = q.shape
    tq = SWA_Q_TILE
    assert tq & (tq - 1) == 0 and n_ctx % tq == 0 and n % tq == 0 and tq % SW_WIN == 0
    bw = tq + 2 * SW_WIN
    assert n - bw >= n_ctx + tq
    gw = SW_GROUP * SW_HD
    kt = jnp.swapaxes(k, 1, 2)
    return pl.pallas_call(
        functools.partial(_swa_kernel, tq=tq, n=n, n_ctx=n_ctx),
        grid=(bsz, SW_KV, n // tq),
        in_specs=[
            pl.BlockSpec(memory_space=pltpu.SMEM),
            pl.BlockSpec((1, tq, gw), lambda b, g, i: (b, i, g)),
            pl.BlockSpec((1, SW_HD, n), lambda b, g, i: (b, g, 0)),
            pl.BlockSpec((1, n, 2 * SW_HD), lambda b, g, i: (b, 0, g)),
            _const_spec((4, tq, bw)),
        ],
        out_specs=pl.BlockSpec((1, tq, gw), lambda b, g, i: (b, i, g)),
        out_shape=jax.ShapeDtypeStruct((bsz, n, D_MODEL), BF16),
        compiler_params=_cparams(("arbitrary", "arbitrary", "arbitrary")),
    )(sinks.astype(F32), q, kt, va, _swa_mask_table(tq, bw))


def _head_norm(y, g, width):
    parts = []
    for h in range(y.shape[1] // width):
        yh = y[:, h * width:(h + 1) * width]
        parts.append(yh * lax.rsqrt(jnp.mean(yh * yh, axis=-1, keepdims=True) + EPS))
    return jnp.concatenate(parts, axis=1) * g


def _mixer_update(it, mode, n_ctx, tm, post_scale):
    x_ref, gate_ref, w_ref = next(it), next(it), next(it)
    if mode == "mlstm":
        hf_ref, hb_ref, og_ref, g_ref = next(it), next(it), next(it), next(it)
        y = _head_norm(hf_ref[0].astype(F32) + hb_ref[0].astype(F32), g_ref[...], ML_V)
        z = y * _sigmoid(og_ref[0].astype(F32))
    elif mode == "diff":
        a_ref, g_ref = next(it), next(it)
        z = _head_norm(a_ref[0].astype(F32), g_ref[...], DF_VD) * post_scale
    else:
        a_ref = next(it)
        z = a_ref[0]
    y = jnp.dot(z.astype(BF16), w_ref[...], preferred_element_type=F32)
    gate = _row_select(gate_ref[0], pl.program_id(1) * tm, tm, n_ctx)
    return x_ref[0] + gate * y


def _mixer_operands(x, mix, tm):
    mode, gate_pair, w_out, extra, _ = mix
    d = x.shape[2]
    tok = lambda b, t: (b, t, 0)
    in_specs = [pl.BlockSpec((1, tm, d), tok), pl.BlockSpec((1, 2, d), lambda b, t: (b, 0, 0)),
                _const_spec(w_out.shape)]
    args = [x, gate_pair, w_out]
    if mode == "mlstm":
        hf, hb, og, g = extra
        in_specs += [pl.BlockSpec((1, tm, d), tok), pl.BlockSpec((1, tm, d), tok), pl.BlockSpec((1, tm, d), tok),
                     pl.BlockSpec((1, d), lambda b, t: (0, 0))]
        args += [hf, hb, og, g.reshape(1, d)]
    elif mode == "diff":
        a, g = extra
        in_specs += [pl.BlockSpec((1, tm, d), tok), pl.BlockSpec((1, d), lambda b, t: (0, 0))]
        args += [a, g.reshape(1, d)]
    else:
        (a,) = extra
        in_specs += [pl.BlockSpec((1, tm, d), tok)]
        args += [a]
    return in_specs, args


def _dense_ffn_kernel(*refs, mix_mode, post_scale, n_ctx, tm, bounds):
    it = iter(refs)
    x = _mixer_update(it, mix_mode, n_ctx, tm, post_scale)
    a_ref, b_ref, gate_ref, wa_ref, wb_ref, wo_ref, o_ref, h_ref, g_ref = (next(it) for _ in range(9))
    row0 = pl.program_id(1) * tm
    h_ref[...] = _norm_mod(x, a_ref[0], b_ref[0], row0, n_ctx).astype(BF16)
    for c0, c1 in zip(bounds[:-1], bounds[1:]):
        a = jnp.dot(h_ref[...], wa_ref[:, c0:c1], preferred_element_type=F32)
        b = jnp.dot(h_ref[...], wb_ref[:, c0:c1], preferred_element_type=F32)
        g_ref[:, c0:c1] = (a * _sigmoid(a) * b).astype(BF16)
    y = jnp.dot(g_ref[...], wo_ref[...], preferred_element_type=F32)
    o_ref[0] = x + _row_select(gate_ref[0], row0, tm, n_ctx) * y


def _dense_ffn(x, mix, a_pair, b_pair, gate_pair, wa, wb, wo, n_ctx):
    bsz, n, d = x.shape
    tm = TOKEN_TILE
    mix_specs, mix_args = _mixer_operands(x, mix, tm)
    ff = wa.shape[1]
    half = (pl.cdiv(ff // MXU_WIDTH, 2) * MXU_WIDTH) if ff % MXU_WIDTH == 0 else ff
    bounds = (0, half, ff) if half < ff else (0, ff)
    tok = lambda b, t: (b, t, 0)
    pair = pl.BlockSpec((1, 2, d), lambda b, t: (b, 0, 0))
    kern = functools.partial(_dense_ffn_kernel, mix_mode=mix[0], post_scale=mix[4], n_ctx=n_ctx, tm=tm,
                             bounds=bounds)
    return pl.pallas_call(
        kern,
        grid=(bsz, n // tm),
        in_specs=mix_specs + [pair, pair, pair,
                              _const_spec(wa.shape), _const_spec(wb.shape), _const_spec(wo.shape)],
        out_specs=pl.BlockSpec((1, tm, d), tok),
        out_shape=jax.ShapeDtypeStruct((bsz, n, d), F32),
        scratch_shapes=[pltpu.VMEM((tm, d), BF16), pltpu.VMEM((tm, ff), BF16)],
        compiler_params=_cparams(("arbitrary", "arbitrary")),
    )(*mix_args, a_pair, b_pair, gate_pair, wa, wb, wo)


def _moe_route_kernel(*refs, mix_mode, post_scale, n_ctx, tm):
    it = iter(refs)
    x = _mixer_update(it, mix_mode, n_ctx, tm, post_scale)
    a_ref, b_ref, r_ref, x_ref, h_ref, g_ref, i_ref = (next(it) for _ in range(7))
    row0 = pl.program_id(1) * tm
    x_ref[0] = x
    h = _norm_mod(x, a_ref[0], b_ref[0], row0, n_ctx)
    h_ref[0] = h.astype(BF16)
    logits = lax.dot_general(r_ref[...], h, (((1,), (1,)), ((), ())), preferred_element_type=F32,
                             precision=lax.Precision.HIGHEST)
    e_idx = lax.broadcasted_iota(jnp.int32, logits.shape, 0)
    v1 = jnp.max(logits, axis=0, keepdims=True)
    i1 = jnp.min(jnp.where(logits == v1, e_idx, N_EXPERTS), axis=0, keepdims=True)
    rest = jnp.where(e_idx == i1, -jnp.inf, logits)
    v2 = jnp.max(rest, axis=0, keepdims=True)
    i2 = jnp.min(jnp.where(rest == v2, e_idx, N_EXPERTS), axis=0, keepdims=True)
    p2 = jnp.exp(v2 - v1)
    den = 1.0 + p2
    g_ref[0] = jnp.concatenate([1.0 / den, p2 / den], axis=0)
    i_ref[0] = jnp.concatenate([i1, i2], axis=0)


def _moe_route(x, mix, a_pair, b_pair, router_t, n_ctx):
    bsz, n, d = x.shape
    tm = TOKEN_TILE
    tok = lambda b, t: (b, t, 0)
    pair = pl.BlockSpec((1, 2, d), lambda b, t: (b, 0, 0))
    mix_specs, mix_args = _mixer_operands(x, mix, tm)
    return pl.pallas_call(
        functools.partial(_moe_route_kernel, mix_mode=mix[0], post_scale=mix[4], n_ctx=n_ctx, tm=tm),
        grid=(bsz, n // tm),
        in_specs=mix_specs + [pair, pair, _const_spec(router_t.shape)],
        out_specs=[pl.BlockSpec((1, tm, d), tok),
                   pl.BlockSpec((1, tm, d), tok),
                   pl.BlockSpec((1, TOP_K, tm), lambda b, t: (b, 0, t)),
                   pl.BlockSpec((1, TOP_K, tm), lambda b, t: (b, 0, t))],
        out_shape=[jax.ShapeDtypeStruct((bsz, n, d), F32),
                   jax.ShapeDtypeStruct((bsz, n, d), BF16),
                   jax.ShapeDtypeStruct((bsz, TOP_K, n), F32),
                   jax.ShapeDtypeStruct((bsz, TOP_K, n), jnp.int32)],
        compiler_params=_cparams(("arbitrary", "arbitrary")),
    )(*mix_args, a_pair, b_pair, router_t)


def _moe_expert_kernel(te_ref, nu_ref, x_ref, wi_ref, wo_ref, o_ref, g_ref, *, ff, chunk):
    i = pl.program_id(0)

    @pl.when(i < nu_ref[0])
    def _():
        x = x_ref[...]
        for c0 in range(0, ff, chunk):
            a = jnp.dot(x, wi_ref[0, :, c0:c0 + chunk], preferred_element_type=F32)
            b = jnp.dot(x, wi_ref[0, :, ff + c0:ff + c0 + chunk], preferred_element_type=F32)
            g_ref[:, c0:c0 + chunk] = (a * _sigmoid(a) * b).astype(BF16)
        o_ref[...] = jnp.dot(g_ref[...], wo_ref[0], preferred_element_type=F32).astype(o_ref.dtype)

    @pl.when(i >= nu_ref[0])
    def _():
        o_ref[...] = jnp.zeros_like(o_ref)


def _moe_experts(xs, w_in, w_out, tile_expert, n_used):
    p, d = xs.shape
    tm = MOE_TILE
    ff = w_out.shape[1]
    chunk = MOE_FF_CHUNK
    assert ff % chunk == 0 and chunk % LANES == 0
    nt = p // tm

    def teff(i, nu):
        return jnp.minimum(i, jnp.maximum(nu[0] - 1, 0))

    grid_spec = pltpu.PrefetchScalarGridSpec(
        num_scalar_prefetch=2,
        grid=(nt,),
        in_specs=[
            pl.BlockSpec((tm, d), lambda i, te, nu: (teff(i, nu), 0)),
            pl.BlockSpec((1, d, 2 * ff), lambda i, te, nu: (te[teff(i, nu)], 0, 0)),
            pl.BlockSpec((1, ff, d), lambda i, te, nu: (te[teff(i, nu)], 0, 0)),
        ],
        out_specs=pl.BlockSpec((tm, d), lambda i, te, nu: (i, 0)),
        scratch_shapes=[pltpu.VMEM((tm, ff), BF16)],
    )
    return pl.pallas_call(
        functools.partial(_moe_expert_kernel, ff=ff, chunk=chunk),
        grid_spec=grid_spec,
        out_shape=jax.ShapeDtypeStruct((p, d), BF16),
        compiler_params=_cparams(("arbitrary",)),
    )(tile_expert, n_used, xs, w_in, w_out)


def _moe_combine_kernel(*refs, n_ctx, tm, skip, final):
    it = iter(refs)
    x_ref, gate_ref, y0_ref, y1_ref, pk_ref = next(it), next(it), next(it), next(it), next(it)
    fg_ref = next(it) if final else None
    o_ref = next(it)
    pk = pk_ref[0]
    y = pk[:, 0:1] * y0_ref[0, 0].astype(F32) + pk[:, 1:2] * y1_ref[0, 0].astype(F32)
    x = x_ref[0] + _row_select(gate_ref[0], (pl.program_id(1) + skip) * tm, tm, n_ctx) * y
    if final:
        x = x * lax.rsqrt(jnp.mean(x * x, axis=-1, keepdims=True) + EPS) * fg_ref[...]
    o_ref[0] = x


def _moe_combine(x, gate_pair, y, slot_w, n_ctx, final_g=None):
    bsz, n, d = x.shape
    final = final_g is not None
    tm = n_ctx if final else TOKEN_TILE
    skip = n_ctx // tm if final else 0
    assert n % tm == 0
    tok = lambda b, t: (b, t + skip, 0)
    slot = lambda k: pl.BlockSpec((1, 1, tm, d), lambda b, t: (k, b, t + skip, 0))
    in_specs = [pl.BlockSpec((1, tm, d), tok), pl.BlockSpec((1, 2, d), lambda b, t: (b, 0, 0)),
                slot(0), slot(1), pl.BlockSpec((1, tm, TOP_K), tok)]
    args = [x, gate_pair, y, y, slot_w]
    if final:
        in_specs.append(pl.BlockSpec((1, d), lambda b, t: (0, 0)))
        args.append(final_g.reshape(1, d))
    return pl.pallas_call(
        functools.partial(_moe_combine_kernel, n_ctx=n_ctx, tm=tm, skip=skip, final=final),
        grid=(bsz, n // tm - skip),
        in_specs=in_specs,
        out_specs=pl.BlockSpec((1, tm, d), lambda b, t: (b, t, 0)),
        out_shape=jax.ShapeDtypeStruct((bsz, n - skip * tm, d), F32),
        compiler_params=_cparams(("arbitrary", "arbitrary")),
    )(*args)


def _moe_layer(x, mix, a_pair, b_pair, gate_pair, router, w_in_all, w_out_all, layer, n_ctx, final_g):
    bsz, n, d = x.shape
    tm = MOE_TILE
    ntok = bsz * n
    x, h2, gates, idx = _moe_route(x, mix, a_pair, b_pair, router.T, n_ctx)
    w_in_all, w_out_all, h2 = lax.optimization_barrier((w_in_all, w_out_all, h2))
    w_in = w_in_all[layer].astype(BF16)
    w_out = w_out_all[layer].astype(BF16)
    e_flat = jnp.moveaxis(idx, 1, 0).reshape(TOP_K * ntok)
    onehot = (e_flat[:, None] == jnp.arange(N_EXPERTS, dtype=jnp.int32)[None, :]).astype(jnp.int32)
    csum = jnp.cumsum(onehot, axis=0)
    counts = csum[-1]
    rank = jnp.sum(onehot * csum, axis=1) - 1
    padded = ((counts + tm - 1) // tm) * tm
    ends = jnp.cumsum(padded)
    starts = ends - padded
    dest = jnp.sum(onehot * starts[None, :], axis=1) + rank
    n_rows = TOP_K * ntok + N_EXPERTS * tm
    n_rows = ((n_rows + tm - 1) // tm) * tm
    tile_start = jnp.arange(n_rows // tm, dtype=jnp.int32) * tm
    tile_expert = jnp.minimum(jnp.sum((tile_start[:, None] >= ends[None, :]).astype(jnp.int32), axis=1),
                              N_EXPERTS - 1).astype(jnp.int32)
    n_used = (ends[-1] // tm).astype(jnp.int32).reshape(1)
    order = jnp.argsort(e_flat, stable=True).astype(jnp.int32)
    local = (tile_start - starts[tile_expert])[:, None] + jnp.arange(tm, dtype=jnp.int32)[None, :]
    pos = (jnp.cumsum(counts) - counts)[tile_expert][:, None] + local
    filler = jnp.arange(n_rows, dtype=jnp.int32).reshape(-1, tm) % (TOP_K * ntok)
    pos = jnp.where(local < counts[tile_expert][:, None], pos, filler).reshape(n_rows)
    src = order.at[pos].get(mode="promise_in_bounds") % ntok
    rows = lambda a, i: a.at[i].get(mode="promise_in_bounds")
    xs = rows(h2.reshape(ntok, d), src)
    ys = _moe_experts(xs, w_in, w_out, tile_expert, n_used)
    y = rows(ys, dest).reshape(TOP_K, bsz, n, d)
    return _moe_combine(x, gate_pair, y, jnp.swapaxes(gates, 1, 2), n_ctx, final_g)


def _rope_table(n_ctx, seq):
    pos = jnp.arange(seq)
    row = (pos // GRID_W).astype(F32)
    col = (pos % GRID_W).astype(F32)
    nf = DF_HD // 4
    inv = jnp.power(ROPE_BASE, -jnp.arange(nf, dtype=F32) / nf)
    ang = jnp.concatenate([row[:, None] * inv, col[:, None] * inv], axis=-1)
    cos, sin = jnp.cos(ang), jnp.sin(ang)
    cos = jnp.concatenate([jnp.ones((n_ctx, 2 * nf), F32), cos], axis=0)
    sin = jnp.concatenate([jnp.zeros((n_ctx, 2 * nf), F32), sin], axis=0)
    c128 = jnp.tile(jnp.concatenate([cos, cos], axis=1), (1, 2))
    s128 = jnp.tile(jnp.concatenate([-sin, sin], axis=1), (1, 2))
    return jnp.concatenate([c128, s128], axis=1)


def _pair(v, bsz):
    return jnp.stack([v[:bsz], jnp.broadcast_to(v[bsz], (bsz, v.shape[1]))], axis=1)


def kernel(x, c, ctx, c_ctx, ada_w, ada_b, norm1, norm2, ml_w_in, ml_gate_b, ml_hnorm, ml_w_out,
           df_w_in, df_lam, df_hnorm, df_w_out, sw_w_in, sw_sinks, sw_w_out,
           ffn_w_in, ffn_w_out, moe_router, moe_w_in, moe_w_out, final_norm):
    bsz, seq, d = x.shape
    n_ctx = ctx.shape[1]
    depth = ada_w.shape[0]
    assert d == D_MODEL and n_ctx % ATT_Q_TILE == 0 and (n_ctx + seq) % TOKEN_TILE == 0

    xs = jnp.concatenate([ctx, x], axis=1)
    cond_rows = SUBLANES * ((bsz + 1 + SUBLANES - 1) // SUBLANES)
    cond = jnp.zeros((cond_rows, d), F32).at[:bsz].set(c).at[bsz].set(c_ctx)
    mods = _ada_mod(cond, ada_w, ada_b)
    cs = _rope_table(n_ctx, seq)

    for i in range(depth):
        m = [mods[i, :, k * d:(k + 1) * d] for k in range(6)]
        a1 = _pair(norm1[i][None, :] * (1.0 + m[1]), bsz)
        b1 = _pair(m[0], bsz)
        g1 = _pair(m[2], bsz)
        a2 = _pair(norm2[i][None, :] * (1.0 + m[4]), bsz)
        b2 = _pair(m[3], bsz)
        g2 = _pair(m[5], bsz)
        kind, j = i % N_MIXERS, i // N_MIXERS
        if kind == 0:
            w = ml_w_in[j]
            nmain = 2 * ML_QKW + 2 * D_MODEL
            w_main = w[:, :nmain].astype(BF16)
            wgt = w[:, nmain:].T.astype(BF16)
            segs = ((0, ML_QKW, False, 1.0, 0), (ML_QKW, 2 * ML_QKW, False, 1.0, 0),
                    (2 * ML_QKW, 2 * ML_QKW + d, False, 1.0, 0), (2 * ML_QKW + d, nmain, False, 1.0, 0))
            q, k, v, og, grow = _project(xs, a1, b1, w_main, segs, (BF16, BF16, BF16, BF16), n_ctx, wgt=wgt)
            hf, hb = _mlstm_scan(q, k, v, grow, ml_gate_b[j], n_ctx)
            mix = ("mlstm", g1, ml_w_out[j].astype(BF16), (hf, hb, og, ml_hnorm[j]), 1.0)
        elif kind == 1:
            lambda_init = 0.8 - 0.6 * math.exp(-0.3 * i)
            segs = ((0, DF_QKW, True, DF_HD ** -0.5 * LOG2E, 0), (DF_QKW, 2 * DF_QKW, True, 1.0, 0),
                    (2 * DF_QKW, 2 * DF_QKW + d, False, 1.0, DF_VD))
            q, k, v = _project(xs, a1, b1, df_w_in[j].astype(BF16), segs, (BF16, BF16, BF16), n_ctx, cs=cs)
            att = _diff_attention(q, k, v, df_lam[j], lambda_init, n_ctx)
            mix = ("diff", g1, df_w_out[j].astype(BF16), (att, df_hnorm[j]), 1.0 - lambda_init)
        else:
            nq = SW_HEADS * SW_HD
            nkv = SW_KV * SW_HD
            segs = ((0, nq, True, SW_HD ** -0.5 * LOG2E, 0), (nq, nq + nkv, True, 1.0, 0),
                    (nq + nkv, nq + 2 * nkv, False, 1.0, SW_HD))
            q, k, v = _project(xs, a1, b1, sw_w_in[j].astype(BF16), segs, (BF16, BF16, BF16), n_ctx, cs=cs)
            att = _swa_attention(q, k, v, sw_sinks[j], n_ctx)
            mix = ("plain", g1, sw_w_out[j].astype(BF16), (att,), 1.0)
        jf = i // 2
        last = i == depth - 1
        if i % 2 == 0:
            ff = ffn_w_out.shape[1]
            wi = ffn_w_in[jf].astype(BF16)
            xs = _dense_ffn(xs, mix, a2, b2, g2, wi[:, :ff], wi[:, ff:], ffn_w_out[jf].astype(BF16), n_ctx)
            if last:
                raise NotImplementedError("final norm is fused into the MoE combine")
        else:
            xs = _moe_layer(xs, mix, a2, b2, g2, moe_router[jf], moe_w_in, moe_w_out, jf, n_ctx,
                            final_norm if last else None)
    return xs
```

```python
import functools
import math

import jax
import jax.numpy as jnp
from jax import lax
from jax.experimental import pallas as pl
from jax.experimental.pallas import tpu as pltpu

F32 = jnp.float32
BF16 = jnp.bfloat16

D_MODEL = 1024
GRID_W = 64
EPS = 1e-6
NEG_INF = -1e30
ROPE_BASE = 10000.0
LOG2E = math.log2(math.e)
N_MIXERS = 3

ML_HEADS = 8
ML_V = D_MODEL // ML_HEADS
ML_QK = ML_V // 2
ML_QKW = ML_HEADS * ML_QK
ML_CHUNK = 256

DF_HD = 64
DF_HEADS = D_MODEL // (2 * DF_HD)
DF_VD = 2 * DF_HD
DF_QKW = 2 * DF_HEADS * DF_HD

SW_HD = 64
SW_HEADS = D_MODEL // SW_HD
SW_KV = 4
SW_GROUP = SW_HEADS // SW_KV
SW_WIN = 128

N_EXPERTS = 8
TOP_K = 2

LANES = 128
SUBLANES = 8
MXU_WIDTH = 256
V7X_VMEM_BYTES = 64 * 1024 * 1024
VMEM_LIMIT = V7X_VMEM_BYTES - 8 * 1024 * 1024

TOKEN_TILE = 640
ATT_Q_TILE = 256
ATT_KV_TILE = 16640
SWA_Q_TILE = 256
MOE_TILE = 512
MOE_FF_CHUNK = 1792


def _cparams(semantics):
    return pltpu.CompilerParams(dimension_semantics=semantics, vmem_limit_bytes=VMEM_LIMIT)


def _const_spec(shape):
    nd = len(shape)
    return pl.BlockSpec(shape, lambda *_: (0,) * nd, pipeline_mode=pl.Buffered(1))


def _sigmoid(x):
    return 1.0 / (1.0 + jnp.exp(-x))


def _log_sigmoid(x):
    return jnp.minimum(x, 0.0) - jnp.log(1.0 + jnp.exp(-jnp.abs(x)))


def _norm_mod(x, a_rows, b_rows, row0, n_ctx):
    xn = x * lax.rsqrt(jnp.mean(x * x, axis=-1, keepdims=True) + EPS)
    is_ctx = (row0 + lax.broadcasted_iota(jnp.int32, (x.shape[0], 1), 0)) < n_ctx
    a = jnp.where(is_ctx, a_rows[1:2], a_rows[0:1])
    b = jnp.where(is_ctx, b_rows[1:2], b_rows[0:1])
    return xn * a + b


def _row_select(rows, row0, n_rows, n_ctx):
    is_ctx = (row0 + lax.broadcasted_iota(jnp.int32, (n_rows, 1), 0)) < n_ctx
    return jnp.where(is_ctx, rows[1:2], rows[0:1])


def _ada_kernel(c_ref, w_ref, b_ref, o_ref):
    c = c_ref[...]
    s = c * _sigmoid(c)
    o_ref[0] = jnp.dot(s, w_ref[0], preferred_element_type=F32,
                       precision=lax.Precision.HIGHEST) + b_ref[0]


def _ada_mod(cond, ada_w, ada_b):
    depth, d, six_d = ada_w.shape
    rows = cond.shape[0]
    col = D_MODEL
    return pl.pallas_call(
        _ada_kernel,
        grid=(depth, six_d // col),
        in_specs=[
            pl.BlockSpec((rows, d), lambda i, j: (0, 0)),
            pl.BlockSpec((1, d, col), lambda i, j: (i, 0, j)),
            pl.BlockSpec((1, 1, col), lambda i, j: (i, 0, j)),
        ],
        out_specs=pl.BlockSpec((1, rows, col), lambda i, j: (i, 0, j)),
        out_shape=jax.ShapeDtypeStruct((depth, rows, six_d), F32),
        compiler_params=_cparams(("arbitrary", "arbitrary")),
    )(cond, ada_w, ada_b.reshape(depth, 1, six_d))


def _rope(r, cos, sin):
    w = r.shape[1]
    lane = lax.broadcasted_iota(jnp.int32, r.shape, 1)
    swapped = jnp.where((lane & 32) == 0, pltpu.roll(r, w - 32, 1), pltpu.roll(r, 32, 1))
    reps = w // LANES
    return r * jnp.tile(cos, (1, reps)) + swapped * jnp.tile(sin, (1, reps))


def _proj_kernel(*refs, segs, n_ctx, tm, has_rope, has_grow):
    it = iter(refs)
    x_ref, a_ref, b_ref, w_ref = next(it), next(it), next(it), next(it)
    cs_ref = next(it) if has_rope else None
    wgt_ref = next(it) if has_grow else None
    outs = [next(it) for _ in segs]
    grow_ref = next(it) if has_grow else None
    h_ref = next(it)

    row0 = pl.program_id(1) * tm
    h_ref[...] = _norm_mod(x_ref[0], a_ref[0], b_ref[0], row0, n_ctx).astype(BF16)
    for (c0, c1, rope, scale, ones_w), o_ref in zip(segs, outs):
        r = jnp.dot(h_ref[...], w_ref[:, c0:c1], preferred_element_type=F32)
        if rope:
            r = _rope(r, cs_ref[:, :LANES], cs_ref[:, LANES:])
        if scale != 1.0:
            r = r * scale
        r = r.astype(o_ref.dtype)
        if ones_w:
            ones = jnp.ones((tm, ones_w), o_ref.dtype)
            r = jnp.concatenate([piece for g in range((c1 - c0) // ones_w)
                                 for piece in (r[:, g * ones_w:(g + 1) * ones_w], ones)], axis=1)
        o_ref[0] = r
    if has_grow:
        grow_ref[0] = lax.dot_general(wgt_ref[...], h_ref[...], (((1,), (1,)), ((), ())),
                                      preferred_element_type=F32)


def _project(x, a_pair, b_pair, w, segs, out_dtypes, n_ctx, cs=None, wgt=None):
    bsz, n, d = x.shape
    tm = TOKEN_TILE
    assert n % tm == 0
    tok = lambda b, t: (b, t, 0)
    in_specs = [
        pl.BlockSpec((1, tm, d), tok),
        pl.BlockSpec((1, 2, d), lambda b, t: (b, 0, 0)),
        pl.BlockSpec((1, 2, d), lambda b, t: (b, 0, 0)),
        _const_spec(w.shape),
    ]
    args = [x, a_pair, b_pair, w]
    if cs is not None:
        in_specs.append(pl.BlockSpec((tm, 2 * LANES), lambda b, t: (t, 0)))
        args.append(cs)
    if wgt is not None:
        in_specs.append(_const_spec(wgt.shape))
        args.append(wgt)
    widths = [(c1 - c0) * (2 if ones_w else 1) for (c0, c1, _, _, ones_w) in segs]
    out_specs = [pl.BlockSpec((1, tm, w_), tok) for w_ in widths]
    out_shape = [jax.ShapeDtypeStruct((bsz, n, w_), dt) for w_, dt in zip(widths, out_dtypes)]
    if wgt is not None:
        out_specs.append(pl.BlockSpec((1, wgt.shape[0], tm), lambda b, t: (b, 0, t)))
        out_shape.append(jax.ShapeDtypeStruct((bsz, wgt.shape[0], n), F32))
    kern = functools.partial(_proj_kernel, segs=segs, n_ctx=n_ctx, tm=tm,
                             has_rope=cs is not None, has_grow=wgt is not None)
    return pl.pallas_call(
        kern,
        grid=(bsz, n // tm),
        in_specs=in_specs,
        out_specs=out_specs,
        out_shape=out_shape,
        scratch_shapes=[pltpu.VMEM((tm, d), BF16)],
        compiler_params=_cparams(("arbitrary", "arbitrary")),
    )(*args)


def _split3(x):
    hi = x.astype(BF16)
    r1 = x - hi.astype(F32)
    mid = r1.astype(BF16)
    lo = (r1 - mid.astype(F32)).astype(BF16)
    return hi, mid, lo


_TN = (((0,), (0,)), ((), ()))
_NT = (((1,), (1,)), ((), ()))


def _expand_heads(x, width):
    nh = x.shape[0]
    x3 = jnp.concatenate(_split3(x), axis=0)
    r = lax.broadcasted_iota(jnp.int32, (3 * nh, nh * width), 0)
    c = lax.broadcasted_iota(jnp.int32, (3 * nh, nh * width), 1)
    sel = jnp.where((r & (nh - 1)) == (c >> int(math.log2(width))), 1.0, 0.0).astype(BF16)
    return lax.dot_general(x3, sel, _TN, preferred_element_type=F32)


ML_STAT_ROWS = 6 * ML_HEADS


def _mlstm_gate_kernel(gr_ref, br_ref, st_ref, *, t, chunks):
    nh = ML_HEADS
    d = pl.program_id(1)
    fwd = d == 0
    sgn = 1 - 2 * d
    row = lax.broadcasted_iota(jnp.int32, (t, t), 0)
    col = lax.broadcasted_iota(jnp.int32, (t, t), 1)
    tri_t = jnp.where((row - col) * sgn <= 0, 1.0, 0.0).astype(BF16)
    lane = lax.broadcasted_iota(jnp.int32, (nh, t), 1)
    scan_pos = jnp.where(fwd, lane, t - 1 - lane)
    for c in range(chunks):
        cols = slice(c * t, (c + 1) * t)
        gr = gr_ref[0, :, cols] + br_ref[...]
        gr = jnp.where(fwd, gr[:2 * nh], gr[2 * nh:])
        ig = gr[:nh]
        fg = _log_sigmoid(gr[nh:])
        bs = jnp.dot(jnp.concatenate(_split3(fg), axis=0), tri_t, preferred_element_type=F32)
        b = bs[:nh] + bs[nh:2 * nh] + bs[2 * nh:]
        b_end = jnp.sum(fg, axis=1, keepdims=True)
        src = ig - b
        cmax = src
        shift = 1
        while shift < t:
            moved = jnp.where(fwd, pltpu.roll(cmax, shift, 1), pltpu.roll(cmax, t - shift, 1))
            cmax = jnp.maximum(cmax, jnp.where(scan_pos >= shift, moved, -jnp.inf))
            shift *= 2
        w_end = b_end + src
        m_loc = jnp.max(w_end, axis=1, keepdims=True)
        e_end = jnp.exp(w_end - m_loc)
        st_ref[0, 0, :, cols] = jnp.concatenate(
            [src, cmax, b, e_end, jnp.broadcast_to(b_end, (nh, t)), jnp.broadcast_to(m_loc, (nh, t))], axis=0)


def _mlstm_gate_stats(grow, gate_b):
    bsz, rows, n = grow.shape
    t = ML_CHUNK
    assert t & (t - 1) == 0
    width = TOKEN_TILE * 2 if n % (TOKEN_TILE * 2) == 0 else t
    bias_row = jnp.broadcast_to(gate_b.reshape(rows, 1).astype(F32), (rows, t))
    return pl.pallas_call(
        functools.partial(_mlstm_gate_kernel, t=t, chunks=width // t),
        grid=(bsz, 2, n // width),
        in_specs=[pl.BlockSpec((1, rows, width), lambda b, d, i: (b, 0, i)),
                  pl.BlockSpec((rows, t), lambda b, d, i: (0, 0))],
        out_specs=pl.BlockSpec((1, 1, ML_STAT_ROWS, width), lambda b, d, i: (b, d, 0, i)),
        out_shape=jax.ShapeDtypeStruct((bsz, 2, ML_STAT_ROWS, n), F32),
        compiler_params=_cparams(("arbitrary", "arbitrary", "arbitrary")),
    )(grow, bias_row)


def _mlstm_kernel(qf_ref, kf_ref, vf_ref, stf_ref, qb_ref, kb_ref, vb_ref, stb_ref, of_ref, ob_ref,
                  ct_ref, m_ref, *, t):
    @pl.when(pl.program_id(1) == 0)
    def _():
        ct_ref[...] = jnp.zeros_like(ct_ref)
        m_ref[...] = jnp.zeros_like(m_ref)

    _mlstm_chunk(qf_ref, kf_ref, vf_ref, stf_ref, of_ref, ct_ref.at[0], m_ref.at[0], t=t, forward=True)
    _mlstm_chunk(qb_ref, kb_ref, vb_ref, stb_ref, ob_ref, ct_ref.at[1], m_ref.at[1], t=t, forward=False)


def _mlstm_chunk(q_ref, k_ref, v_ref, st_ref, o_ref, ct_ref, m_ref, *, t, forward):
    nh = ML_HEADS
    row = lax.broadcasted_iota(jnp.int32, (t, t), 0)
    col = lax.broadcasted_iota(jnp.int32, (t, t), 1)
    mask = (col <= row) if forward else (col >= row)
    st = st_ref[0, 0]
    src, cmax, b, e_end = (st[j * nh:(j + 1) * nh] for j in range(4))
    b_end = st[4 * nh:5 * nh, 0:1]
    m_loc = st[5 * nh:6 * nh, 0:1]

    m_in = m_ref[:, 0:1]
    big_m = jnp.maximum(m_in, cmax)
    m_new = jnp.maximum(b_end + m_in, m_loc)
    a_dec = jnp.exp(b_end + m_in - m_new)
    s_dec = jnp.exp(m_loc - m_new)

    q = q_ref[0]
    k = k_ref[0] * (ML_QK ** -0.5)
    v = v_ref[0]
    q_in = (q.astype(F32) * _expand_heads(jnp.exp(m_in - big_m), ML_QK)).astype(BF16)
    k_end = (k.astype(F32) * _expand_heads(e_end, ML_QK)).astype(BF16)
    floor = _expand_heads(jnp.exp(-(b + big_m)), ML_V)

    ones3 = jnp.ones((3 * nh, t), BF16)
    dec_l = jnp.concatenate(list(_split3(-big_m)) + [ones3], axis=0)
    dec_r = jnp.concatenate([ones3] + list(_split3(src)), axis=0)
    head_of_row = lax.broadcasted_iota(jnp.int32, (6 * nh, t), 0) & (nh - 1)
    ones = jnp.ones((t, ML_V), BF16)
    nums, dens = [], []
    for h in range(nh):
        qk = slice(h * ML_QK, (h + 1) * ML_QK)
        va = jnp.concatenate([v[:, h * ML_V:(h + 1) * ML_V], ones], axis=1)
        s = lax.dot_general(q[:, qk], k[:, qk], _NT, preferred_element_type=F32)
        w_log = lax.dot_general(dec_l, jnp.where(head_of_row == h, dec_r, jnp.zeros_like(dec_r)), _TN,
                                preferred_element_type=F32)
        w = (jnp.where(mask, jnp.exp(w_log), 0.0) * s).astype(BF16)
        ct = ct_ref[h]
        r = (jnp.dot(w, va, preferred_element_type=F32)
             + jnp.dot(q_in[:, qk], ct.astype(BF16), preferred_element_type=F32))
        nums.append(r[:, :ML_V])
        dens.append(r[:, ML_V:])
        c_loc = lax.dot_general(k_end[:, qk], va, _TN, preferred_element_type=F32)
        ct_ref[h] = a_dec[h:h + 1] * ct + s_dec[h:h + 1] * c_loc
    num = jnp.concatenate(nums, axis=1)
    den = jnp.concatenate(dens, axis=1)
    o_ref[0] = (num / jnp.maximum(jnp.abs(den), floor)).astype(o_ref.dtype)
    m_ref[...] = jnp.broadcast_to(m_new, m_ref.shape)


def _mlstm_scan(q, k, v, grow, gate_b, n_ctx):
    bsz, n, _ = q.shape
    t = ML_CHUNK
    assert t & (t - 1) == 0
    nc = n // t
    ncc = n_ctx // t

    def back(i):
        return jnp.where(i < ncc, ncc - 1 - i, nc - 1 - (i - ncc))

    stats = _mlstm_gate_stats(grow, gate_b)
    fwd = lambda b, i: (b, i, 0)
    bwd = lambda b, i: (b, back(i), 0)
    specs = lambda tok, d: [
        pl.BlockSpec((1, t, ML_QKW), tok), pl.BlockSpec((1, t, ML_QKW), tok), pl.BlockSpec((1, t, D_MODEL), tok),
        pl.BlockSpec((1, 1, ML_STAT_ROWS, t), lambda b, i: (b, d, 0, tok(b, i)[1]))]
    out = jax.ShapeDtypeStruct((bsz, n, D_MODEL), BF16)
    return pl.pallas_call(
        functools.partial(_mlstm_kernel, t=t),
        grid=(bsz, nc),
        in_specs=specs(fwd, 0) + specs(bwd, 1),
        out_specs=[pl.BlockSpec((1, t, D_MODEL), fwd), pl.BlockSpec((1, t, D_MODEL), bwd)],
        out_shape=[out, out],
        scratch_shapes=[pltpu.VMEM((2, ML_HEADS, ML_QK, 2 * ML_V), F32), pltpu.VMEM((2, ML_HEADS, LANES), F32)],
        compiler_params=_cparams(("arbitrary", "arbitrary")),
    )(q, k, v, stats, q, k, v, stats)


def _diff_kernel(lam_ref, q_ref, kt_ref, va_ref, o_ref, acc_ref, m_ref, s_ref, smax_ref, *, tq, chunks, n_ctx,
                 lambda_init):
    n_slots = s_ref.shape[0]
    qi = pl.program_id(2)
    q = q_ref[0]
    lane = lax.broadcasted_iota(jnp.int32, q.shape, 1)
    zero = jnp.zeros_like(q)
    qs = (jnp.where(lane < DF_HD, q, zero), jnp.where(lane < DF_HD, zero, q))
    acc_ref[...] = jnp.zeros_like(acc_ref)
    m_ref[...] = jnp.full_like(m_ref, NEG_INF)

    def scores(start, size, slot):
        kt = kt_ref[0, :, pl.ds(start, size)]
        for j in range(2):
            s = jnp.dot(qs[j], kt, preferred_element_type=F32)
            s_ref[slot, j, :, :size] = s
            smax_ref[slot, j] = jnp.max(s, axis=1, keepdims=True)

    def accumulate(start, size, slot):
        va = va_ref[0, pl.ds(start, size), :]
        ss = [s_ref[slot, j, :, :size] for j in range(2)]
        m_old = [m_ref[j] for j in range(2)]
        m_new = [jnp.maximum(m_old[j], smax_ref[slot, j]) for j in range(2)]
        ps = [jnp.exp2((ss[j] - m_new[j]).astype(BF16)) for j in range(2)]
        pv = [jnp.dot(ps[j], va, preferred_element_type=F32) for j in range(2)]
        for j in range(2):
            acc_ref[j] = jnp.exp2(m_old[j] - m_new[j]) * acc_ref[j] + pv[j]
            m_ref[j] = m_new[j]

    is_ctx_tile = qi * tq < n_ctx

    @pl.when(is_ctx_tile)
    def _():
        scores(0, n_ctx, 0)
        accumulate(0, n_ctx, 0)

    @pl.when(jnp.logical_not(is_ctx_tile))
    def _():
        scores(chunks[0][0], chunks[0][1], 0)
        for c, (start, size) in enumerate(chunks):
            if c + 1 < len(chunks):
                scores(chunks[c + 1][0], chunks[c + 1][1], (c + 1) % n_slots)
            accumulate(start, size, c % n_slots)

    lam = lam_ref[...]
    lam_full = (jnp.exp(jnp.sum(lam[0:1] * lam[1:2], axis=1, keepdims=True))
                - jnp.exp(jnp.sum(lam[2:3] * lam[3:4], axis=1, keepdims=True)) + lambda_init)
    o1 = acc_ref[0, :, :DF_VD] / acc_ref[0, :, DF_VD:]
    o2 = acc_ref[1, :, :DF_VD] / acc_ref[1, :, DF_VD:]
    o_ref[0] = (o1 - lam_full * o2).astype(o_ref.dtype)


def _diff_attention(q, k, va, lam, lambda_init, n_ctx):
    bsz, n, _ = q.shape
    tq = ATT_Q_TILE
    sizes = [ATT_KV_TILE] * (n // ATT_KV_TILE) + ([n % ATT_KV_TILE] if n % ATT_KV_TILE else [])
    chunks = tuple((sum(sizes[:c]), sizes[c]) for c in range(len(sizes)))
    tk = max(sizes)
    n_slots = min(2, len(sizes))
    assert n_ctx % tq == 0 and n % tq == 0 and n_ctx <= sizes[0] and all(sz % MXU_WIDTH == 0 for sz in sizes)
    kt = jnp.swapaxes(k, 1, 2)
    kern = functools.partial(_diff_kernel, tq=tq, chunks=chunks, n_ctx=n_ctx, lambda_init=lambda_init)
    once = pl.Buffered(1)
    return pl.pallas_call(
        kern,
        grid=(bsz, DF_HEADS, n // tq),
        in_specs=[
            pl.BlockSpec(lam.shape, lambda b, h, i: (0, 0)),
            pl.BlockSpec((1, tq, 2 * DF_HD), lambda b, h, i: (b, i, h)),
            pl.BlockSpec((1, 2 * DF_HD, n), lambda b, h, i: (b, h, 0), pipeline_mode=once),
            pl.BlockSpec((1, n, 2 * DF_VD), lambda b, h, i: (b, 0, h), pipeline_mode=once),
        ],
        out_specs=pl.BlockSpec((1, tq, DF_VD), lambda b, h, i: (b, i, h)),
        out_shape=jax.ShapeDtypeStruct((bsz, n, D_MODEL), BF16),
        scratch_shapes=[pltpu.VMEM((2, tq, 2 * DF_VD), F32), pltpu.VMEM((2, tq, 1), F32),
                        pltpu.VMEM((n_slots, 2, tq, tk), F32), pltpu.VMEM((n_slots, 2, tq, 1), F32)],
        compiler_params=_cparams(("arbitrary", "arbitrary", "arbitrary")),
    )(lam.astype(F32), q, kt, va)


def _swa_mask_table(tq, bw):
    t = jnp.arange(tq, dtype=jnp.int32)[:, None]
    s = jnp.arange(bw, dtype=jnp.int32)[None, :]

    def one(off, first_ok):
        ok = jnp.logical_and(s >= first_ok, jnp.abs(t - s + off) <= SW_WIN)
        return jnp.where(ok, 0.0, NEG_INF).astype(F32)
    return jnp.stack([one(SW_WIN, 0), one(SW_WIN, SW_WIN), one(2 * SW_WIN, 0), one(0, bw)])


def _swa_kernel(sink_ref, q_ref, kt_ref, va_ref, mask_ref, o_ref, *, tq, n, n_ctx):
    g = pl.program_id(1)
    i = pl.program_id(2)
    bw = tq + 2 * SW_WIN
    q = q_ref[0]
    qh = jnp.concatenate([q[:, u * SW_HD:(u + 1) * SW_HD] for u in range(SW_GROUP)], axis=0)
    start = pl.multiple_of(jnp.clip(i * tq - SW_WIN, 0, n - bw), SW_WIN)
    s_c = jnp.dot(qh, kt_ref[0, :, 0:n_ctx], preferred_element_type=F32)
    s_b = jnp.dot(qh, kt_ref[0, :, pl.ds(start, bw)], preferred_element_type=F32)
    case = jnp.where(i * tq < n_ctx, 3, jnp.where(i * tq - SW_WIN < n_ctx, 1,
                                                   jnp.where(i * tq - SW_WIN > n - bw, 2, 0)))
    s_b = s_b + jnp.tile(mask_ref[case], (SW_GROUP, 1))
    head = lax.broadcasted_iota(jnp.int32, (SW_GROUP * tq, 1), 0) // tq
    sink = jnp.zeros((SW_GROUP * tq, 1), F32)
    for u in range(SW_GROUP):
        sink = jnp.where(head == u, sink_ref[g * SW_GROUP + u] * LOG2E, sink)
    m = jnp.maximum(sink, jnp.maximum(jnp.max(s_c, axis=1, keepdims=True), jnp.max(s_b, axis=1, keepdims=True)))
    acc = (jnp.dot(jnp.exp2((s_c - m).astype(BF16)), va_ref[0, 0:n_ctx, :], preferred_element_type=F32)
           + jnp.dot(jnp.exp2((s_b - m).astype(BF16)), va_ref[0, pl.ds(start, bw), :],
                     preferred_element_type=F32))
    out = acc[:, :SW_HD] / (acc[:, SW_HD:] + jnp.exp2(sink - m))
    o_ref[0] = jnp.concatenate([out[u * tq:(u + 1) * tq] for u in range(SW_GROUP)], axis=1).astype(o_ref.dtype)


def _swa_attention(q, k, va, sinks, n_ctx):
    bsz, n, _ = q.shape
    tq = SWA_Q_TILE
    assert tq & (tq - 1) == 0 and n_ctx % tq == 0 and n % tq == 0 and tq % SW_WIN == 0
    bw = tq + 2 * SW_WIN
    assert n - bw >= n_ctx + tq
    gw = SW_GROUP * SW_HD
    kt = jnp.swapaxes(k, 1, 2)
    return pl.pallas_call(
        functools.partial(_swa_kernel, tq=tq, n=n, n_ctx=n_ctx),
        grid=(bsz, SW_KV, n // tq),
        in_specs=[
            pl.BlockSpec(memory_space=pltpu.SMEM),
            pl.BlockSpec((1, tq, gw), lambda b, g, i: (b, i, g)),
            pl.BlockSpec((1, SW_HD, n), lambda b, g, i: (b, g, 0)),
            pl.BlockSpec((1, n, 2 * SW_HD), lambda b, g, i: (b, 0, g)),
            _const_spec((4, tq, bw)),
        ],
        out_specs=pl.BlockSpec((1, tq, gw), lambda b, g, i: (b, i, g)),
        out_shape=jax.ShapeDtypeStruct((bsz, n, D_MODEL), BF16),
        compiler_params=_cparams(("arbitrary", "arbitrary", "arbitrary")),
    )(sinks.astype(F32), q, kt, va, _swa_mask_table(tq, bw))


def _head_norm(y, g, width):
    parts = []
    for h in range(y.shape[1] // width):
        yh = y[:, h * width:(h + 1) * width]
        parts.append(yh * lax.rsqrt(jnp.mean(yh * yh, axis=-1, keepdims=True) + EPS))
    return jnp.concatenate(parts, axis=1) * g


def _mixer_update(it, mode, n_ctx, tm, post_scale):
    x_ref, gate_ref, w_ref = next(it), next(it), next(it)
    if mode == "mlstm":
        hf_ref, hb_ref, og_ref, g_ref = next(it), next(it), next(it), next(it)
        y = _head_norm(hf_ref[0].astype(F32) + hb_ref[0].astype(F32), g_ref[...], ML_V)
        z = y * _sigmoid(og_ref[0].astype(F32))
    elif mode == "diff":
        a_ref, g_ref = next(it), next(it)
        z = _head_norm(a_ref[0].astype(F32), g_ref[...], DF_VD) * post_scale
    else:
        a_ref = next(it)
        z = a_ref[0]
    y = jnp.dot(z.astype(BF16), w_ref[...], preferred_element_type=F32)
    gate = _row_select(gate_ref[0], pl.program_id(1) * tm, tm, n_ctx)
    return x_ref[0] + gate * y


def _mixer_operands(x, mix, tm):
    mode, gate_pair, w_out, extra, _ = mix
    d = x.shape[2]
    tok = lambda b, t: (b, t, 0)
    in_specs = [pl.BlockSpec((1, tm, d), tok), pl.BlockSpec((1, 2, d), lambda b, t: (b, 0, 0)),
                _const_spec(w_out.shape)]
    args = [x, gate_pair, w_out]
    if mode == "mlstm":
        hf, hb, og, g = extra
        in_specs += [pl.BlockSpec((1, tm, d), tok), pl.BlockSpec((1, tm, d), tok), pl.BlockSpec((1, tm, d), tok),
                     pl.BlockSpec((1, d), lambda b, t: (0, 0))]
        args += [hf, hb, og, g.reshape(1, d)]
    elif mode == "diff":
        a, g = extra
        in_specs += [pl.BlockSpec((1, tm, d), tok), pl.BlockSpec((1, d), lambda b, t: (0, 0))]
        args += [a, g.reshape(1, d)]
    else:
        (a,) = extra
        in_specs += [pl.BlockSpec((1, tm, d), tok)]
        args += [a]
    return in_specs, args


def _dense_ffn_kernel(*refs, mix_mode, post_scale, n_ctx, tm, bounds):
    it = iter(refs)
    x = _mixer_update(it, mix_mode, n_ctx, tm, post_scale)
    a_ref, b_ref, gate_ref, wa_ref, wb_ref, wo_ref, o_ref, h_ref, g_ref = (next(it) for _ in range(9))
    row0 = pl.program_id(1) * tm
    h_ref[...] = _norm_mod(x, a_ref[0], b_ref[0], row0, n_ctx).astype(BF16)
    for c0, c1 in zip(bounds[:-1], bounds[1:]):
        a = jnp.dot(h_ref[...], wa_ref[:, c0:c1], preferred_element_type=F32)
        b = jnp.dot(h_ref[...], wb_ref[:, c0:c1], preferred_element_type=F32)
        g_ref[:, c0:c1] = (a * _sigmoid(a) * b).astype(BF16)
    y = jnp.dot(g_ref[...], wo_ref[...], preferred_element_type=F32)
    o_ref[0] = x + _row_select(gate_ref[0], row0, tm, n_ctx) * y


def _dense_ffn(x, mix, a_pair, b_pair, gate_pair, wa, wb, wo, n_ctx):
    bsz, n, d = x.shape
    tm = TOKEN_TILE
    mix_specs, mix_args = _mixer_operands(x, mix, tm)
    ff = wa.shape[1]
    half = (pl.cdiv(ff // MXU_WIDTH, 2) * MXU_WIDTH) if ff % MXU_WIDTH == 0 else ff
    bounds = (0, half, ff) if half < ff else (0, ff)
    tok = lambda b, t: (b, t, 0)
    pair = pl.BlockSpec((1, 2, d), lambda b, t: (b, 0, 0))
    kern = functools.partial(_dense_ffn_kernel, mix_mode=mix[0], post_scale=mix[4], n_ctx=n_ctx, tm=tm,
                             bounds=bounds)
    return pl.pallas_call(
        kern,
        grid=(bsz, n // tm),
        in_specs=mix_specs + [pair, pair, pair,
                              _const_spec(wa.shape), _const_spec(wb.shape), _const_spec(wo.shape)],
        out_specs=pl.BlockSpec((1, tm, d), tok),
        out_shape=jax.ShapeDtypeStruct((bsz, n, d), F32),
        scratch_shapes=[pltpu.VMEM((tm, d), BF16), pltpu.VMEM((tm, ff), BF16)],
        compiler_params=_cparams(("arbitrary", "arbitrary")),
    )(*mix_args, a_pair, b_pair, gate_pair, wa, wb, wo)


def _moe_route_kernel(*refs, mix_mode, post_scale, n_ctx, tm):
    it = iter(refs)
    x = _mixer_update(it, mix_mode, n_ctx, tm, post_scale)
    a_ref, b_ref, r_ref, x_ref, h_ref, g_ref, i_ref = (next(it) for _ in range(7))
    row0 = pl.program_id(1) * tm
    x_ref[0] = x
    h = _norm_mod(x, a_ref[0], b_ref[0], row0, n_ctx)
    h_ref[0] = h.astype(BF16)
    logits = lax.dot_general(r_ref[...], h, (((1,), (1,)), ((), ())), preferred_element_type=F32,
                             precision=lax.Precision.HIGHEST)
    e_idx = lax.broadcasted_iota(jnp.int32, logits.shape, 0)
    v1 = jnp.max(logits, axis=0, keepdims=True)
    i1 = jnp.min(jnp.where(logits == v1, e_idx, N_EXPERTS), axis=0, keepdims=True)
    rest = jnp.where(e_idx == i1, -jnp.inf, logits)
    v2 = jnp.max(rest, axis=0, keepdims=True)
    i2 = jnp.min(jnp.where(rest == v2, e_idx, N_EXPERTS), axis=0, keepdims=True)
    p2 = jnp.exp(v2 - v1)
    den = 1.0 + p2
    g_ref[0] = jnp.concatenate([1.0 / den, p2 / den], axis=0)
    i_ref[0] = jnp.concatenate([i1, i2], axis=0)


def _moe_route(x, mix, a_pair, b_pair, router_t, n_ctx):
    bsz, n, d = x.shape
    tm = TOKEN_TILE
    tok = lambda b, t: (b, t, 0)
    pair = pl.BlockSpec((1, 2, d), lambda b, t: (b, 0, 0))
    mix_specs, mix_args = _mixer_operands(x, mix, tm)
    return pl.pallas_call(
        functools.partial(_moe_route_kernel, mix_mode=mix[0], post_scale=mix[4], n_ctx=n_ctx, tm=tm),
        grid=(bsz, n // tm),
        in_specs=mix_specs + [pair, pair, _const_spec(router_t.shape)],
        out_specs=[pl.BlockSpec((1, tm, d), tok),
                   pl.BlockSpec((1, tm, d), tok),
                   pl.BlockSpec((1, TOP_K, tm), lambda b, t: (b, 0, t)),
                   pl.BlockSpec((1, TOP_K, tm), lambda b, t: (b, 0, t))],
        out_shape=[jax.ShapeDtypeStruct((bsz, n, d), F32),
                   jax.ShapeDtypeStruct((bsz, n, d), BF16),
                   jax.ShapeDtypeStruct((bsz, TOP_K, n), F32),
                   jax.ShapeDtypeStruct((bsz, TOP_K, n), jnp.int32)],
        compiler_params=_cparams(("arbitrary", "arbitrary")),
    )(*mix_args, a_pair, b_pair, router_t)


def _moe_expert_kernel(te_ref, nu_ref, x_ref, wi_ref, wo_ref, o_ref, g_ref, *, ff, chunk):
    i = pl.program_id(0)

    @pl.when(i < nu_ref[0])
    def _():
        x = x_ref[...]
        for c0 in range(0, ff, chunk):
            a = jnp.dot(x, wi_ref[0, :, c0:c0 + chunk], preferred_element_type=F32)
            b = jnp.dot(x, wi_ref[0, :, ff + c0:ff + c0 + chunk], preferred_element_type=F32)
            g_ref[:, c0:c0 + chunk] = (a * _sigmoid(a) * b).astype(BF16)
        o_ref[...] = jnp.dot(g_ref[...], wo_ref[0], preferred_element_type=F32).astype(o_ref.dtype)

    @pl.when(i >= nu_ref[0])
    def _():
        o_ref[...] = jnp.zeros_like(o_ref)


def _moe_experts(xs, w_in, w_out, tile_expert, n_used):
    p, d = xs.shape
    tm = MOE_TILE
    ff = w_out.shape[1]
    chunk = MOE_FF_CHUNK
    assert ff % chunk == 0 and chunk % LANES == 0
    nt = p // tm

    def teff(i, nu):
        return jnp.minimum(i, jnp.maximum(nu[0] - 1, 0))

    grid_spec = pltpu.PrefetchScalarGridSpec(
        num_scalar_prefetch=2,
        grid=(nt,),
        in_specs=[
            pl.BlockSpec((tm, d), lambda i, te, nu: (teff(i, nu), 0)),
            pl.BlockSpec((1, d, 2 * ff), lambda i, te, nu: (te[teff(i, nu)], 0, 0)),
            pl.BlockSpec((1, ff, d), lambda i, te, nu: (te[teff(i, nu)], 0, 0)),
        ],
        out_specs=pl.BlockSpec((tm, d), lambda i, te, nu: (i, 0)),
        scratch_shapes=[pltpu.VMEM((tm, ff), BF16)],
    )
    return pl.pallas_call(
        functools.partial(_moe_expert_kernel, ff=ff, chunk=chunk),
        grid_spec=grid_spec,
        out_shape=jax.ShapeDtypeStruct((p, d), BF16),
        compiler_params=_cparams(("arbitrary",)),
    )(tile_expert, n_used, xs, w_in, w_out)


def _moe_combine_kernel(*refs, n_ctx, tm, skip, final):
    it = iter(refs)
    x_ref, gate_ref, y0_ref, y1_ref, pk_ref = next(it), next(it), next(it), next(it), next(it)
    fg_ref = next(it) if final else None
    o_ref = next(it)
    pk = pk_ref[0]
    y = pk[:, 0:1] * y0_ref[0, 0].astype(F32) + pk[:, 1:2] * y1_ref[0, 0].astype(F32)
    x = x_ref[0] + _row_select(gate_ref[0], (pl.program_id(1) + skip) * tm, tm, n_ctx) * y
    if final:
        x = x * lax.rsqrt(jnp.mean(x * x, axis=-1, keepdims=True) + EPS) * fg_ref[...]
    o_ref[0] = x


def _moe_combine(x, gate_pair, y, slot_w, n_ctx, final_g=None):
    bsz, n, d = x.shape
    final = final_g is not None
    tm = n_ctx if final else TOKEN_TILE
    skip = n_ctx // tm if final else 0
    assert n % tm == 0
    tok = lambda b, t: (b, t + skip, 0)
    slot = lambda k: pl.BlockSpec((1, 1, tm, d), lambda b, t: (k, b, t + skip, 0))
    in_specs = [pl.BlockSpec((1, tm, d), tok), pl.BlockSpec((1, 2, d), lambda b, t: (b, 0, 0)),
                slot(0), slot(1), pl.BlockSpec((1, tm, TOP_K), tok)]
    args = [x, gate_pair, y, y, slot_w]
    if final:
        in_specs.append(pl.BlockSpec((1, d), lambda b, t: (0, 0)))
        args.append(final_g.reshape(1, d))
    return pl.pallas_call(
        functools.partial(_moe_combine_kernel, n_ctx=n_ctx, tm=tm, skip=skip, final=final),
        grid=(bsz, n // tm - skip),
        in_specs=in_specs,
        out_specs=pl.BlockSpec((1, tm, d), lambda b, t: (b, t, 0)),
        out_shape=jax.ShapeDtypeStruct((bsz, n - skip * tm, d), F32),
        compiler_params=_cparams(("arbitrary", "arbitrary")),
    )(*args)


def _moe_layer(x, mix, a_pair, b_pair, gate_pair, router, w_in_all, w_out_all, layer, n_ctx, final_g):
    bsz, n, d = x.shape
    tm = MOE_TILE
    ntok = bsz * n
    x, h2, gates, idx = _moe_route(x, mix, a_pair, b_pair, router.T, n_ctx)
    w_in_all, w_out_all, h2 = lax.optimization_barrier((w_in_all, w_out_all, h2))
    w_in = w_in_all[layer].astype(BF16)
    w_out = w_out_all[layer].astype(BF16)
    e_flat = jnp.moveaxis(idx, 1, 0).reshape(TOP_K * ntok)
    onehot = (e_flat[:, None] == jnp.arange(N_EXPERTS, dtype=jnp.int32)[None, :]).astype(jnp.int32)
    csum = jnp.cumsum(onehot, axis=0)
    counts = csum[-1]
    rank = jnp.sum(onehot * csum, axis=1) - 1
    padded = ((counts + tm - 1) // tm) * tm
    ends = jnp.cumsum(padded)
    starts = ends - padded
    dest = jnp.sum(onehot * starts[None, :], axis=1) + rank
    n_rows = TOP_K * ntok + N_EXPERTS * tm
    n_rows = ((n_rows + tm - 1) // tm) * tm
    tile_start = jnp.arange(n_rows // tm, dtype=jnp.int32) * tm
    tile_expert = jnp.minimum(jnp.sum((tile_start[:, None] >= ends[None, :]).astype(jnp.int32), axis=1),
                              N_EXPERTS - 1).astype(jnp.int32)
    n_used = (ends[-1] // tm).astype(jnp.int32).reshape(1)
    order = jnp.argsort(e_flat, stable=True).astype(jnp.int32)
    local = (tile_start - starts[tile_expert])[:, None] + jnp.arange(tm, dtype=jnp.int32)[None, :]
    pos = (jnp.cumsum(counts) - counts)[tile_expert][:, None] + local
    filler = jnp.arange(n_rows, dtype=jnp.int32).reshape(-1, tm) % (TOP_K * ntok)
    pos = jnp.where(local < counts[tile_expert][:, None], pos, filler).reshape(n_rows)
    src = order.at[pos].get(mode="promise_in_bounds") % ntok
    rows = lambda a, i: a.at[i].get(mode="promise_in_bounds")
    xs = rows(h2.reshape(ntok, d), src)
    ys = _moe_experts(xs, w_in, w_out, tile_expert, n_used)
    y = rows(ys, dest).reshape(TOP_K, bsz, n, d)
    return _moe_combine(x, gate_pair, y, jnp.swapaxes(gates, 1, 2), n_ctx, final_g)


def _rope_table(n_ctx, seq):
    pos = jnp.arange(seq)
    row = (pos // GRID_W).astype(F32)
    col = (pos % GRID_W).astype(F32)
    nf = DF_HD // 4
    inv = jnp.power(ROPE_BASE, -jnp.arange(nf, dtype=F32) / nf)
    ang = jnp.concatenate([row[:, None] * inv, col[:, None] * inv], axis=-1)
    cos, sin = jnp.cos(ang), jnp.sin(ang)
    cos = jnp.concatenate([jnp.ones((n_ctx, 2 * nf), F32), cos], axis=0)
    sin = jnp.concatenate([jnp.zeros((n_ctx, 2 * nf), F32), sin], axis=0)
    c128 = jnp.tile(jnp.concatenate([cos, cos], axis=1), (1, 2))
    s128 = jnp.tile(jnp.concatenate([-sin, sin], axis=1), (1, 2))
    return jnp.concatenate([c128, s128], axis=1)


def _pair(v, bsz):
    return jnp.stack([v[:bsz], jnp.broadcast_to(v[bsz], (bsz, v.shape[1]))], axis=1)


def kernel(x, c, ctx, c_ctx, ada_w, ada_b, norm1, norm2, ml_w_in, ml_gate_b, ml_hnorm, ml_w_out,
           df_w_in, df_lam, df_hnorm, df_w_out, sw_w_in, sw_sinks, sw_w_out,
           ffn_w_in, ffn_w_out, moe_router, moe_w_in, moe_w_out, final_norm):
    bsz, seq, d = x.shape
    n_ctx = ctx.shape[1]
    depth = ada_w.shape[0]
    assert d == D_MODEL and n_ctx % ATT_Q_TILE == 0 and (n_ctx + seq) % TOKEN_TILE == 0

    xs = jnp.concatenate([ctx, x], axis=1)
    cond_rows = SUBLANES * ((bsz + 1 + SUBLANES - 1) // SUBLANES)
    cond = jnp.zeros((cond_rows, d), F32).at[:bsz].set(c).at[bsz].set(c_ctx)
    mods = _ada_mod(cond, ada_w, ada_b)
    cs = _rope_table(n_ctx, seq)

    for i in range(depth):
        m = [mods[i, :, k * d:(k + 1) * d] for k in range(6)]
        a1 = _pair(norm1[i][None, :] * (1.0 + m[1]), bsz)
        b1 = _pair(m[0], bsz)
        g1 = _pair(m[2], bsz)
        a2 = _pair(norm2[i][None, :] * (1.0 + m[4]), bsz)
        b2 = _pair(m[3], bsz)
        g2 = _pair(m[5], bsz)
        kind, j = i % N_MIXERS, i // N_MIXERS
        if kind == 0:
            w = ml_w_in[j]
            nmain = 2 * ML_QKW + 2 * D_MODEL
            w_main = w[:, :nmain].astype(BF16)
            wgt = w[:, nmain:].T.astype(BF16)
            segs = ((0, ML_QKW, False, 1.0, 0), (ML_QKW, 2 * ML_QKW, False, 1.0, 0),
                    (2 * ML_QKW, 2 * ML_QKW + d, False, 1.0, 0), (2 * ML_QKW + d, nmain, False, 1.0, 0))
            q, k, v, og, grow = _project(xs, a1, b1, w_main, segs, (BF16, BF16, BF16, BF16), n_ctx, wgt=wgt)
            hf, hb = _mlstm_scan(q, k, v, grow, ml_gate_b[j], n_ctx)
            mix = ("mlstm", g1, ml_w_out[j].astype(BF16), (hf, hb, og, ml_hnorm[j]), 1.0)
        elif kind == 1:
            lambda_init = 0.8 - 0.6 * math.exp(-0.3 * i)
            segs = ((0, DF_QKW, True, DF_HD ** -0.5 * LOG2E, 0), (DF_QKW, 2 * DF_QKW, True, 1.0, 0),
                    (2 * DF_QKW, 2 * DF_QKW + d, False, 1.0, DF_VD))
            q, k, v = _project(xs, a1, b1, df_w_in[j].astype(BF16), segs, (BF16, BF16, BF16), n_ctx, cs=cs)
            att = _diff_attention(q, k, v, df_lam[j], lambda_init, n_ctx)
            mix = ("diff", g1, df_w_out[j].astype(BF16), (att, df_hnorm[j]), 1.0 - lambda_init)
        else:
            nq = SW_HEADS * SW_HD
            nkv = SW_KV * SW_HD
            segs = ((0, nq, True, SW_HD ** -0.5 * LOG2E, 0), (nq, nq + nkv, True, 1.0, 0),
                    (nq + nkv, nq + 2 * nkv, False, 1.0, SW_HD))
            q, k, v = _project(xs, a1, b1, sw_w_in[j].astype(BF16), segs, (BF16, BF16, BF16), n_ctx, cs=cs)
            att = _swa_attention(q, k, v, sw_sinks[j], n_ctx)
            mix = ("plain", g1, sw_w_out[j].astype(BF16), (att,), 1.0)
        jf = i // 2
        last = i == depth - 1
        if i % 2 == 0:
            ff = ffn_w_out.shape[1]
            wi = ffn_w_in[jf].astype(BF16)
            xs = _dense_ffn(xs, mix, a2, b2, g2, wi[:, :ff], wi[:, ff:], ffn_w_out[jf].astype(BF16), n_ctx)
            if last:
                raise NotImplementedError("final norm is fused into the MoE combine")
        else:
            xs = _moe_layer(xs, mix, a2, b2, g2, moe_router[jf], moe_w_in, moe_w_out, jf, n_ctx,
                            final_norm if last else None)
    return xs
```
